```python
import math
import jax, jax.numpy as jnp
from jax import lax
import numpy as np

D_MODEL = 2048
BATCH = 4
SEQ = 4096
DEPTH = 2

CHUNK = 64
NORM_EPS = 1e-6
ROPE_THETA = 500000.0
N_BRANCH = 4
BRANCH_W = D_MODEL // 2

GLA_HEADS = 4
GLA_DK = BRANCH_W // 2 // GLA_HEADS
GLA_DV = BRANCH_W // GLA_HEADS
GLA_RANK = 16
GLA_TAU = 16.0

DSA_HEADS = 8
DSA_HD = BRANCH_W // DSA_HEADS
DSA_ROT = DSA_HD // 4
IDX_HEADS = 8
IDX_HD = 64
IDX_ROT = IDX_HD // 4
TOPK_MAX = 256
DSA_QBLOCK = 64

RWKV_HD = 64
RWKV_HEADS = BRANCH_W // RWKV_HD
RWKV_DECAY_RANK = 96
RWKV_A_RANK = 96
RWKV_DECAY_SCALE = 0.606531
RWKV_GN_EPS = 64e-5
RWKV_KK_EPS = 1e-12

SGU_CHUNK = 128
SGU_GROUPS = 8
SGU_GW = BRANCH_W // SGU_GROUPS
SGU_LN_EPS = 1e-5

A_COLS = (GLA_HEADS * GLA_DK, GLA_HEADS * GLA_DK, BRANCH_W, GLA_RANK, BRANCH_W)
B_COLS = (BRANCH_W, BRANCH_W, BRANCH_W, IDX_HEADS * IDX_HD, IDX_HD, IDX_HEADS, BRANCH_W)
C_COLS = (BRANCH_W, BRANCH_W, BRANCH_W, RWKV_DECAY_RANK, RWKV_A_RANK, BRANCH_W)
D_COLS = (2 * BRANCH_W, BRANCH_W)
GROUP_COLS = (sum(A_COLS), sum(B_COLS), sum(C_COLS), sum(D_COLS), N_BRANCH * D_MODEL)
N_IN_COLS = sum(GROUP_COLS)

kernel_name = "hybrid_gla_dsa_rwkv7_sgu_gated_merge"

F32 = jnp.float32


def _split(t, sizes):
    return jnp.split(t, np.cumsum(sizes)[:-1].tolist(), axis=-1)


def _rmsnorm(x, g):
    x32 = x.astype(F32)
    y = x32 * lax.rsqrt(jnp.mean(x32 * x32, axis=-1, keepdims=True) + NORM_EPS)
    return (y * g.astype(F32)).astype(x.dtype)


def _layernorm(x, g, b, eps):
    x32 = x.astype(F32)
    mu = jnp.mean(x32, axis=-1, keepdims=True)
    xc = x32 - mu
    var = jnp.mean(xc * xc, axis=-1, keepdims=True)
    return xc * lax.rsqrt(var + eps) * g.astype(F32) + b.astype(F32)


def _rope_tables(positions, rot_dim):
    inv = ROPE_THETA ** (-jnp.arange(0, rot_dim, 2, dtype=F32) / rot_dim)
    ang = positions.astype(F32)[..., None] * inv
    return jnp.cos(ang), jnp.sin(ang)


def _partial_rope(t, cos, sin):
    half = cos.shape[-1]
    t32 = t.astype(F32)
    x1, x2, rest = t32[..., :half], t32[..., half:2 * half], t32[..., 2 * half:]
    c, s = cos[:, :, None, :], sin[:, :, None, :]
    return jnp.concatenate([x1 * c - x2 * s, x2 * c + x1 * s, rest], axis=-1).astype(t.dtype)


def _token_shift(t):
    return jnp.pad(t[:, :-1], ((0, 0), (1, 0), (0, 0)))


def _gla_branch(q, k, v, g_lr, gate, w_g2, b_g, head_g):
    bsz, seq, _ = q.shape
    nc = seq // CHUNK
    log_a = jax.nn.log_sigmoid((g_lr @ w_g2 + b_g).astype(F32)) / GLA_TAU

    def chunks(t, hd):
        return t.astype(F32).reshape(bsz, nc, CHUNK, GLA_HEADS, hd).swapaxes(0, 1)

    qc = chunks(q, GLA_DK) * GLA_DK ** -0.5
    kc, lc = chunks(k, GLA_DK), chunks(log_a, GLA_DK)
    vc = chunks(v, GLA_DV)

    def step(state, inp):
        q_c, k_c, v_c, l_c = inp
        cum = jnp.cumsum(l_c, axis=1)
        total = cum[:, -1]
        k_dec = k_c * jnp.exp(total[:, None] - cum)
        state = state * jnp.exp(total)[..., None] + jnp.einsum('bchk,bchv->bhkv', k_dec, v_c)
        out = jnp.einsum('bchk,bhkv->bchv', q_c, state)
        return state, out

    state0 = jnp.zeros((bsz, GLA_HEADS, GLA_DK, GLA_DV), F32)
    _, o = lax.scan(step, state0, (qc, kc, vc, lc))
    o = o.swapaxes(0, 1).reshape(bsz, seq, GLA_HEADS, GLA_DV)
    o = _rmsnorm(o, head_g)
    return o.reshape(bsz, seq, BRANCH_W) * jax.nn.silu(gate.astype(F32))


def _dsa_branch(q, k, v, iq, ik, iw, gate, cos_a, sin_a, cos_i, sin_i):
    bsz, seq, _ = q.shape
    topk = min(TOPK_MAX, seq // 4)
    q = _partial_rope(q.reshape(bsz, seq, DSA_HEADS, DSA_HD), cos_a, sin_a)
    k = _partial_rope(k.reshape(bsz, seq, DSA_HEADS, DSA_HD), cos_a, sin_a)
    v = v.reshape(bsz, seq, DSA_HEADS, DSA_HD)
    iq = _partial_rope(iq.reshape(bsz, seq, IDX_HEADS, IDX_HD), cos_i, sin_i)
    ik = _partial_rope(ik[:, :, None, :], cos_i, sin_i)[:, :, 0]
    iw = iw.astype(F32) * (IDX_HEADS * IDX_HD) ** -0.5
    nqb = seq // DSA_QBLOCK
    key_chunk = jnp.arange(seq) // CHUNK

    def blocks(t):
        return t.reshape(bsz, nqb, DSA_QBLOCK, *t.shape[2:]).swapaxes(0, 1)

    def attend(args):
        qb, iqb, iwb, q0 = args
        rel = jax.nn.relu(jnp.einsum('bqhd,bsd->bqhs', iqb, ik).astype(F32))
        score = jnp.einsum('bqh,bqhs->bqs', iwb, rel)
        q_chunk = (q0 + jnp.arange(DSA_QBLOCK)) // CHUNK
        admissible = key_chunk[None, :] <= q_chunk[:, None]
        score = jnp.where(admissible[None], score, -jnp.inf)
        top_val, top_idx = lax.top_k(score, topk)
        valid = jnp.isfinite(top_val)
        k_sel = jax.vmap(lambda t, i: t[i])(k, top_idx)
        v_sel = jax.vmap(lambda t, i: t[i])(v, top_idx)
        logits = jnp.einsum('bqhd,bqkhd->bqhk', qb, k_sel).astype(F32) * DSA_HD ** -0.5
        logits = jnp.where(valid[:, :, None, :], logits, -jnp.inf)
        probs = jax.nn.softmax(logits, axis=-1)
        return jnp.einsum('bqhk,bqkhd->bqhd', probs, v_sel.astype(F32))

    starts = jnp.arange(nqb, dtype=jnp.int32) * DSA_QBLOCK
    o = lax.map(attend, (blocks(q), blocks(iq), blocks(iw), starts))
    o = o.swapaxes(0, 1).reshape(bsz, seq, BRANCH_W)
    return o * jax.nn.silu(gate.astype(F32))


def _rwkv7_branch(c, mu, w0, w_w2, a0, w_a2, k_k, k_a, r_k, lnx_g, lnx_b):
    bsz, seq, _ = c.shape
    c = c.astype(F32)
    c = c + (_token_shift(c) - c) * mu.astype(F32)
    r, k, v, w_lr, a_lr, gate = _split(c, C_COLS)
    w_log = -RWKV_DECAY_SCALE * jax.nn.sigmoid(w0 + jnp.tanh(w_lr) @ w_w2.astype(F32))
    a = jax.nn.sigmoid(a0 + a_lr @ w_a2.astype(F32))
    kk = k * k_k
    k = k * (1.0 + (a - 1.0) * k_a)

    def heads(t):
        return t.reshape(bsz, seq, RWKV_HEADS, RWKV_HD)

    r, k, v, w_log, a, kk = (heads(t) for t in (r, k, v, w_log, a, kk))
    kk = kk * lax.rsqrt(jnp.sum(kk * kk, axis=-1, keepdims=True) + RWKV_KK_EPS)

    def step(state, inp):
        r_t, w_t, k_t, v_t, kk_t, a_t = inp
        sa = jnp.einsum('bhvk,bhk->bhv', state, -kk_t)
        state = (state * w_t[:, :, None, :]
                 + sa[..., None] * (kk_t * a_t)[:, :, None, :]
                 + v_t[..., None] * k_t[:, :, None, :])
        return state, jnp.einsum('bhvk,bhk->bhv', state, r_t)

    xs = tuple(jnp.moveaxis(t, 1, 0) for t in (r, jnp.exp(w_log), k, v, kk, a))
    state0 = jnp.zeros((bsz, RWKV_HEADS, RWKV_HD, RWKV_HD), F32)
    _, y = lax.scan(step, state0, xs)
    y = jnp.moveaxis(y, 0, 1)
    y = _layernorm(y, lnx_g.reshape(RWKV_HEADS, RWKV_HD), lnx_b.reshape(RWKV_HEADS, RWKV_HD), RWKV_GN_EPS)
    y = y + jnp.sum(r * k * r_k.astype(F32), axis=-1, keepdims=True) * v
    return y.reshape(bsz, seq, BRANCH_W) * jax.nn.silu(gate)


def _sgu_branch(z, gate, ln_g, ln_b, w_s, b_s):
    bsz, seq, _ = z.shape
    z = jax.nn.gelu(z.astype(F32), approximate=False)
    u, v = jnp.split(z, 2, axis=-1)
    v = _layernorm(v, ln_g, ln_b, SGU_LN_EPS)
    nch = seq // SGU_CHUNK
    v = v.reshape(bsz, nch, SGU_CHUNK, SGU_GROUPS, SGU_GW)
    pos_chunk = jnp.arange(SGU_CHUNK) // CHUNK
    mask = pos_chunk[None, :] <= pos_chunk[:, None]
    ws = jnp.where(mask[None], w_s.astype(F32), 0.0)
    sv = jnp.einsum('gij,bcjgd->bcigd', ws, v) + b_s.astype(F32).T[None, None, :, :, None]
    return u * sv.reshape(bsz, seq, BRANCH_W) * jax.nn.silu(gate.astype(F32))


def setup_inputs(seed: int = 0) -> dict:
    key = jax.random.key(seed)
    ks = jax.random.split(key, 32)
    L, W = DEPTH, BRANCH_W

    def nrm(k, shape, scale):
        return jax.random.normal(k, shape, F32) * scale

    def gain(k, shape):
        return 1.0 + nrm(k, shape, 0.1)

    positions = (jax.random.randint(ks[1], (BATCH, 1), 0, 4096) * CHUNK
                 + jnp.arange(SEQ)[None, :]).astype(jnp.int32)
    return {
        "x": nrm(ks[0], (BATCH, SEQ, D_MODEL), 1.0),
        "positions": positions,
        "norm_pre": gain(ks[2], (L, D_MODEL)),
        "norm_post": gain(ks[3], (L, D_MODEL)),
        "w_in": nrm(ks[4], (L, D_MODEL, N_IN_COLS), D_MODEL ** -0.5),
        "gla_w_g2": nrm(ks[5], (L, GLA_RANK, GLA_HEADS * GLA_DK), GLA_RANK ** -0.5),
        "gla_b_g": 0.5 + nrm(ks[6], (L, GLA_HEADS * GLA_DK), 0.1),
        "gla_head_g": gain(ks[7], (L, GLA_DV)),
        "rwkv_mu": jax.random.uniform(ks[8], (L, sum(C_COLS)), F32),
        "rwkv_w0": nrm(ks[9], (L, W), 1.0),
        "rwkv_w_w2": nrm(ks[10], (L, RWKV_DECAY_RANK, W), 0.5 * RWKV_DECAY_RANK ** -0.5),
        "rwkv_a0": nrm(ks[11], (L, W), 0.5),
        "rwkv_w_a2": nrm(ks[12], (L, RWKV_A_RANK, W), 0.5 * RWKV_A_RANK ** -0.5),
        "rwkv_k_k": 0.85 + nrm(ks[13], (L, W), 0.05),
        "rwkv_k_a": gain(ks[14], (L, W)),
        "rwkv_r_k": nrm(ks[15], (L, RWKV_HEADS, RWKV_HD), 0.1),
        "rwkv_lnx_g": gain(ks[16], (L, W)),
        "rwkv_lnx_b": nrm(ks[17], (L, W), 0.02),
        "sgu_ln_g": gain(ks[18], (L, W)),
        "sgu_ln_b": nrm(ks[19], (L, W), 0.02),
        "sgu_w_s": nrm(ks[20], (L, SGU_GROUPS, SGU_CHUNK, SGU_CHUNK), SGU_CHUNK ** -0.5),
        "sgu_b_s": gain(ks[21], (L, SGU_GROUPS, SGU_CHUNK)),
        "w_proj": nrm(ks[22], (L, N_BRANCH, W, D_MODEL), W ** -0.5),
        "w_out": nrm(ks[23], (L, D_MODEL, D_MODEL), D_MODEL ** -0.5),
    }


def reference(x, positions, norm_pre, norm_post, w_in, gla_w_g2, gla_b_g, gla_head_g,
              rwkv_mu, rwkv_w0, rwkv_w_w2, rwkv_a0, rwkv_w_a2, rwkv_k_k, rwkv_k_a, rwkv_r_k,
              rwkv_lnx_g, rwkv_lnx_b, sgu_ln_g, sgu_ln_b, sgu_w_s, sgu_b_s, w_proj, w_out):
    bsz, seq, _ = x.shape
    cos_a, sin_a = _rope_tables(positions, DSA_ROT)
    cos_i, sin_i = _rope_tables(positions, IDX_ROT)
    for l in range(DEPTH):
        h = _rmsnorm(x, norm_pre[l])
        p = h @ w_in[l]
        pa, pb, pc, pd, pm = _split(p, GROUP_COLS)
        ya = _gla_branch(*_split(pa, A_COLS), gla_w_g2[l], gla_b_g[l], gla_head_g[l])
        yb = _dsa_branch(*_split(pb, B_COLS), cos_a, sin_a, cos_i, sin_i)
        yc = _rwkv7_branch(pc, rwkv_mu[l], rwkv_w0[l], rwkv_w_w2[l], rwkv_a0[l], rwkv_w_a2[l],
                           rwkv_k_k[l], rwkv_k_a[l], rwkv_r_k[l], rwkv_lnx_g[l], rwkv_lnx_b[l])
        yd = _sgu_branch(*_split(pd, D_COLS), sgu_ln_g[l], sgu_ln_b[l], sgu_w_s[l], sgu_b_s[l])
        ys = jnp.stack([ya, yb, yc, yd], axis=2).astype(x.dtype)
        proj = jnp.einsum('bsgw,gwd->bsgd', ys, w_proj[l])
        gates = jax.nn.sigmoid(pm.reshape(bsz, seq, N_BRANCH, D_MODEL).astype(F32))
        merged = jnp.einsum('bsgd,bsgd->bsd', gates, proj.astype(F32)).astype(x.dtype)
        x = x + _rmsnorm(merged @ w_out[l], norm_post[l])
    return x
```

```python
import functools

import numpy as np
import jax
import jax.numpy as jnp
from jax import lax
from jax.experimental import pallas as pl
from jax.experimental.pallas import tpu as pltpu

F32 = jnp.float32
BF16 = jnp.bfloat16
I32 = jnp.int32
HIGHEST = lax.Precision.HIGHEST

D_MODEL = 2048
CHUNK = 64
NORM_EPS = 1e-6
ROPE_THETA = 500000.0
N_BRANCH = 4
BRANCH_W = D_MODEL // 2

GLA_HEADS = 4
GLA_DK = BRANCH_W // 2 // GLA_HEADS
GLA_DV = BRANCH_W // GLA_HEADS
GLA_RANK = 16
GLA_TAU = 16.0

DSA_HEADS = 8
DSA_HD = BRANCH_W // DSA_HEADS
DSA_ROT = DSA_HD // 4
IDX_HEADS = 8
IDX_HD = 64
IDX_ROT = IDX_HD // 4
TOPK_MAX = 256

RWKV_HD = 64
RWKV_HEADS = BRANCH_W // RWKV_HD
RWKV_DECAY_RANK = 96
RWKV_A_RANK = 96
RWKV_DECAY_SCALE = 0.606531
RWKV_GN_EPS = 64e-5
RWKV_KK_EPS = 1e-12

SGU_CHUNK = 128
SGU_GROUPS = 8
SGU_GW = BRANCH_W // SGU_GROUPS
SGU_LN_EPS = 1e-5

A_COLS = (GLA_HEADS * GLA_DK, GLA_HEADS * GLA_DK, BRANCH_W, GLA_RANK, BRANCH_W)
B_COLS = (BRANCH_W, BRANCH_W, BRANCH_W, IDX_HEADS * IDX_HD, IDX_HD, IDX_HEADS, BRANCH_W)
C_COLS = (BRANCH_W, BRANCH_W, BRANCH_W, RWKV_DECAY_RANK, RWKV_A_RANK, BRANCH_W)
D_COLS = (2 * BRANCH_W, BRANCH_W)
GROUP_COLS = (sum(A_COLS), sum(B_COLS), sum(C_COLS), sum(D_COLS), N_BRANCH * D_MODEL)

LANES = 128
VMEM_LIMIT = 56 * 1024 * 1024
NEG_BIG = -1e30
INT_MIN = -(2 ** 31)

NORM_TM = 512
MM_TM = 1024
GLA_TB = 512
DSA_PREP_TB = 512
DSA_TQ = 128
DSA_KB = 512
RWKV_PREP_TB = 256
RWKV_L = 64
RWKV_POST_TB = 512
SGU_TB = 512
MERGE_TM = 512
MERGE_TN = 1024
OUT_TM = 512


def _cparams(sem):
    return pltpu.CompilerParams(dimension_semantics=sem, vmem_limit_bytes=VMEM_LIMIT)


def _dot(a, b, precision=None):
    return jnp.dot(a, b, preferred_element_type=F32, precision=precision)


def _dot_nt(a, b, precision=None):
    return lax.dot_general(a, b, (((1,), (1,)), ((), ())), preferred_element_type=F32,
                           precision=precision)


def _dot_tn(a, b, precision=None):
    return lax.dot_general(a, b, (((0,), (0,)), ((), ())), preferred_element_type=F32,
                           precision=precision)


def _silu(x):
    return x * jax.nn.sigmoid(x)


def _rmsnorm_cast_kernel(x_ref, g_ref, o_ref):
    x = x_ref[...]
    ms = jnp.mean(x * x, axis=-1, keepdims=True)
    o_ref[...] = (x * lax.rsqrt(ms + NORM_EPS) * g_ref[...]).astype(o_ref.dtype)


def _rmsnorm_cast(x2, g):
    m, d = x2.shape
    return pl.pallas_call(
        _rmsnorm_cast_kernel,
        grid=(m // NORM_TM,),
        in_specs=[pl.BlockSpec((NORM_TM, d), lambda i: (i, 0)),
                  pl.BlockSpec((1, d), lambda i: (0, 0))],
        out_specs=pl.BlockSpec((NORM_TM, d), lambda i: (i, 0)),
        out_shape=jax.ShapeDtypeStruct((m, d), BF16),
        compiler_params=_cparams(("parallel",)),
    )(x2, g.reshape(1, d))


def _matmul_kernel(a_ref, b_ref, o_ref):
    o_ref[...] = _dot(a_ref[...], b_ref[...]).astype(o_ref.dtype)


def _matmul(a, b, tn, out_dtype=F32):
    m, k = a.shape
    n = b.shape[1]
    tm = min(MM_TM, m)
    return pl.pallas_call(
        _matmul_kernel,
        grid=(m // tm, n // tn),
        in_specs=[pl.BlockSpec((tm, k), lambda i, j: (i, 0)),
                  pl.BlockSpec((k, tn), lambda i, j: (0, j))],
        out_specs=pl.BlockSpec((tm, tn), lambda i, j: (i, j)),
        out_shape=jax.ShapeDtypeStruct((m, n), out_dtype),
        compiler_params=_cparams(("parallel", "parallel")),
    )(a, b)


def _gla_kernel(q_ref, k_ref, v_ref, gate_ref, glr_ref, wg2_ref, bg_ref, hg_ref, o_ref, state_ref):
    @pl.when(pl.program_id(2) == 0)
    def _():
        state_ref[...] = jnp.zeros_like(state_ref)

    ii = lax.broadcasted_iota(I32, (CHUNK, CHUNK), 0)
    jj = lax.broadcasted_iota(I32, (CHUNK, CHUNK), 1)
    tril = (ii >= jj).astype(F32)
    for c in range(GLA_TB // CHUNK):
        sl = pl.ds(c * CHUNK, CHUNK)
        q = q_ref[sl, :] * (GLA_DK ** -0.5)
        k = k_ref[sl, :]
        v = v_ref[sl, :]
        z = _dot(glr_ref[sl, :], wg2_ref[...], HIGHEST) + bg_ref[...]
        log_a = (jnp.minimum(z, 0.0) - jnp.log1p(jnp.exp(-jnp.abs(z)))) / GLA_TAU
        cum = _dot(tril, log_a, HIGHEST)
        total = cum[CHUNK - 1:CHUNK, :]
        k_dec = k * jnp.exp(total - cum)
        st = state_ref[...] * jnp.exp(total) + _dot_tn(v, k_dec, HIGHEST)
        state_ref[...] = st
        o = _dot_nt(q, st, HIGHEST)
        o = o * lax.rsqrt(jnp.mean(o * o, axis=-1, keepdims=True) + NORM_EPS) * hg_ref[...]
        o_ref[sl, :] = (o * _silu(gate_ref[sl, :])).astype(o_ref.dtype)


def _gla(pa, wg2p, bg, hg, bsz, seq):
    m = pa.shape[0]
    nt = seq // GLA_TB
    row = lambda b, h, t: b * nt + t
    kb = GLA_HEADS
    vb = (2 * GLA_HEADS * GLA_DK) // GLA_DV
    gb = vb + GLA_HEADS
    lb = (2 * GLA_HEADS * GLA_DK + 2 * BRANCH_W) // LANES
    return pl.pallas_call(
        _gla_kernel,
        grid=(bsz, GLA_HEADS, nt),
        in_specs=[
            pl.BlockSpec((GLA_TB, GLA_DK), lambda b, h, t: (row(b, h, t), h)),
            pl.BlockSpec((GLA_TB, GLA_DK), lambda b, h, t: (row(b, h, t), kb + h)),
            pl.BlockSpec((GLA_TB, GLA_DV), lambda b, h, t: (row(b, h, t), vb + h)),
            pl.BlockSpec((GLA_TB, GLA_DV), lambda b, h, t: (row(b, h, t), gb + h)),
            pl.BlockSpec((GLA_TB, LANES), lambda b, h, t: (row(b, h, t), lb)),
            pl.BlockSpec((LANES, GLA_DK), lambda b, h, t: (0, h)),
            pl.BlockSpec((1, GLA_DK), lambda b, h, t: (0, h)),
            pl.BlockSpec((1, GLA_DV), lambda b, h, t: (0, 0)),
        ],
        out_specs=pl.BlockSpec((GLA_TB, GLA_DV), lambda b, h, t: (row(b, h, t), h)),
        out_shape=jax.ShapeDtypeStruct((m, BRANCH_W), BF16),
        scratch_shapes=[pltpu.VMEM((GLA_DV, GLA_DK), F32)],
        compiler_params=_cparams(("parallel", "parallel", "arbitrary")),
    )(pa, pa, pa, pa, pa, wg2p, bg.reshape(1, -1), hg.reshape(1, -1))


def _rope_tile(x, cos_t, sin_t, half, width):
    lane = lax.broadcasted_iota(I32, x.shape, 1) % width
    partner = jnp.where(lane < half, pltpu.roll(x, LANES - half, 1), pltpu.roll(x, half, 1))
    return x * cos_t + partner * sin_t


def _dsa_prep_kernel(q_ref, k_ref, v_ref, idx_ref, ca_ref, sa_ref, ci_ref, si_ref,
                     qo_ref, ko_ref, vo_ref, iqo_ref, iko_ref, iwo_ref):
    ca, sa = ca_ref[...], sa_ref[...]
    ci, si = ci_ref[...], si_ref[...]
    for h in range(DSA_HEADS):
        sl = slice(h * DSA_HD, (h + 1) * DSA_HD)
        qo_ref[:, sl] = _rope_tile(q_ref[:, sl], ca, sa, DSA_ROT // 2, DSA_HD).astype(qo_ref.dtype)
        ko_ref[:, sl] = _rope_tile(k_ref[:, sl], ca, sa, DSA_ROT // 2, DSA_HD).astype(ko_ref.dtype)
    vo_ref[...] = v_ref[...].astype(vo_ref.dtype)
    niq = IDX_HEADS * IDX_HD
    for j in range(niq // LANES):
        sl = slice(j * LANES, (j + 1) * LANES)
        iqo_ref[:, sl] = _rope_tile(idx_ref[:, sl], ci, si, IDX_ROT // 2, IDX_HD)
    iko_ref[...] = _rope_tile(idx_ref[:, niq:niq + LANES], ci, si, IDX_ROT // 2, IDX_HD)
    iwo_ref[...] = idx_ref[:, niq + LANES:niq + 2 * LANES] * (niq ** -0.5)


def _dsa_prep(pb, pbi, ca, sa, ci, si):
    m = pb.shape[0]
    tb = DSA_PREP_TB
    w = BRANCH_W
    niq = IDX_HEADS * IDX_HD
    rows = lambda i: (i, 0)
    return pl.pallas_call(
        _dsa_prep_kernel,
        grid=(m // tb,),
        in_specs=[pl.BlockSpec((tb, w), lambda i: (i, 0)),
                  pl.BlockSpec((tb, w), lambda i: (i, 1)),
                  pl.BlockSpec((tb, w), lambda i: (i, 2)),
                  pl.BlockSpec((tb, niq + 2 * LANES), rows),
                  pl.BlockSpec((tb, LANES), rows), pl.BlockSpec((tb, LANES), rows),
                  pl.BlockSpec((tb, LANES), rows), pl.BlockSpec((tb, LANES), rows)],
        out_specs=[pl.BlockSpec((tb, w), rows), pl.BlockSpec((tb, w), rows), pl.BlockSpec((tb, w), rows),
                   pl.BlockSpec((tb, niq), rows), pl.BlockSpec((tb, LANES), rows),
                   pl.BlockSpec((tb, LANES), rows)],
        out_shape=[jax.ShapeDtypeStruct((m, w), BF16), jax.ShapeDtypeStruct((m, w), BF16),
                   jax.ShapeDtypeStruct((m, w), BF16), jax.ShapeDtypeStruct((m, niq), F32),
                   jax.ShapeDtypeStruct((m, LANES), F32), jax.ShapeDtypeStruct((m, LANES), F32)],
        compiler_params=_cparams(("parallel",)),
    )(pb, pb, pb, pbi, ca, sa, ci, si)


def _dsa_kernel(q_ref, k_ref, v_ref, iq_ref, ik_ref, iw_ref, gate_ref, o_ref, key_scr, bias_scr, *, topk):
    tq, kb_w = DSA_TQ, DSA_KB
    q0 = pl.program_id(1) * tq
    nkb = (q0 + tq + kb_w - 1) // kb_w
    row = lax.broadcasted_iota(I32, (tq, 1), 0)
    qlim = ((q0 + row) // CHUNK + 1) * CHUNK
    lane_k = lax.broadcasted_iota(I32, (1, kb_w), 1)
    iq = iq_ref[...]
    iw = iw_ref[...]

    def score_body(kb, carry):
        ik = ik_ref[pl.ds(pl.multiple_of(kb * kb_w, kb_w), kb_w), :][:, :IDX_HD]
        acc = jnp.zeros((tq, kb_w), F32)
        for h in range(IDX_HEADS):
            d = _dot_nt(iq[:, h * IDX_HD:(h + 1) * IDX_HD], ik, HIGHEST)
            acc = acc + iw[:, h:h + 1] * jnp.maximum(d, 0.0)
        acc = acc + 0.0
        bits = lax.bitcast_convert_type(acc, I32)
        key = bits ^ ((bits >> 31) & 0x7FFFFFFF)
        sidx = kb * kb_w + lane_k
        key_scr[kb] = jnp.where(sidx < qlim, key, INT_MIN)
        return carry

    lax.fori_loop(0, nkb, score_body, 0)

    def count(pred):
        def body(kb, c):
            m = pred(key_scr[kb], kb * kb_w + lane_k).astype(I32)
            for j in range(kb_w // LANES):
                c = c + m[:, j * LANES:(j + 1) * LANES]
            return c
        c = lax.fori_loop(0, nkb, body, jnp.zeros((tq, LANES), I32))
        return jnp.sum(c.astype(F32), axis=1, keepdims=True).astype(I32)

    def bit_body(it, res):
        trial = res | lax.shift_left(jnp.int32(1), 31 - it)
        cand = trial ^ INT_MIN
        cnt = count(lambda kt, s: kt >= cand)
        return jnp.where(cnt >= topk, trial, res)

    res = lax.fori_loop(0, 32, bit_body, jnp.zeros((tq, 1), I32))
    thr = res ^ INT_MIN
    need = topk - count(lambda kt, s: kt > thr)
    nbits = max(1, int(np.ceil(np.log2(key_scr.shape[0] * kb_w))))

    def tie_body(it, resj):
        trial = resj | lax.shift_left(jnp.int32(1), nbits - 1 - it)
        c = count(lambda kt, s: (kt == thr) & (s < trial))
        return jnp.where(c < need, trial, resj)

    last_tie = lax.fori_loop(0, nbits, tie_body, jnp.zeros((tq, 1), I32))

    def bias_body(kb, carry):
        kt = key_scr[kb]
        sidx = kb * kb_w + lane_k
        sel = ((kt > thr) | ((kt == thr) & (sidx <= last_tie))) & (kt != INT_MIN)
        bias_scr[kb] = jnp.where(sel, 0.0, NEG_BIG)
        return carry

    lax.fori_loop(0, nkb, bias_body, 0)

    scale = DSA_HD ** -0.5
    for h in range(DSA_HEADS):
        sl = slice(h * DSA_HD, (h + 1) * DSA_HD)
        qh = q_ref[:, sl]

        def att_body(kb, carry, sl=sl, qh=qh):
            m_i, l_i, acc = carry
            rows = pl.ds(pl.multiple_of(kb * kb_w, kb_w), kb_w)
            s = _dot_nt(qh, k_ref[rows, sl]) * scale + bias_scr[kb]
            m_n = jnp.maximum(m_i, jnp.max(s, axis=1, keepdims=True))
            p = jnp.exp(s - m_n)
            alpha = jnp.exp(m_i - m_n)
            l_n = alpha * l_i + jnp.sum(p, axis=1, keepdims=True)
            acc = alpha * acc + _dot(p.astype(BF16), v_ref[rows, sl])
            return m_n, l_n, acc

        init = (jnp.full((tq, 1), NEG_BIG, F32), jnp.zeros((tq, 1), F32), jnp.zeros((tq, DSA_HD), F32))
        _, l_f, acc_f = lax.fori_loop(0, nkb, att_body, init)
        o = acc_f / l_f
        o_ref[:, sl] = (o * _silu(gate_ref[:, sl])).astype(o_ref.dtype)


def _dsa(qr, kr, vb, iqr, ikr, iws, pb, bsz, seq):
    m = qr.shape[0]
    w = BRANCH_W
    nq = seq // DSA_TQ
    nkb = seq // DSA_KB
    topk = min(TOPK_MAX, seq // 4)
    qrow = lambda b, i: (b * nq + i, 0)
    return pl.pallas_call(
        functools.partial(_dsa_kernel, topk=topk),
        grid=(bsz, nq),
        in_specs=[pl.BlockSpec((DSA_TQ, w), qrow),
                  pl.BlockSpec((seq, w), lambda b, i: (b, 0)),
                  pl.BlockSpec((seq, w), lambda b, i: (b, 0)),
                  pl.BlockSpec((DSA_TQ, IDX_HEADS * IDX_HD), qrow),
                  pl.BlockSpec((seq, LANES), lambda b, i: (b, 0)),
                  pl.BlockSpec((DSA_TQ, LANES), qrow),
                  pl.BlockSpec((DSA_TQ, w), lambda b, i: (b * nq + i, 3))],
        out_specs=pl.BlockSpec((DSA_TQ, w), qrow),
        out_shape=jax.ShapeDtypeStruct((m, w), BF16),
        scratch_shapes=[pltpu.VMEM((nkb, DSA_TQ, DSA_KB), I32),
                        pltpu.VMEM((nkb, DSA_TQ, DSA_KB), F32)],
        compiler_params=_cparams(("parallel", "arbitrary")),
    )(qr, kr, vb, iqr, ikr, iws, pb)


def _head_sum(x, bd):
    parts = [_dot(x[:, j * LANES:(j + 1) * LANES], bd, HIGHEST) for j in range(x.shape[1] // LANES)]
    return jnp.concatenate(parts, axis=1)


def _rwkv_prep_kernel(c_ref, cs_ref, pc_ref, pcs_ref, mu_ref, mus_ref, w0_ref, ww2_ref, a0_ref, wa2_ref,
                      kkw_ref, ka_ref, rk_ref, bd_ref,
                      r_o, wl_o, k_o, v_o, kk_o, a_o, sg_o, bon_o, *, blocks_per_seq):
    tb = c_ref.shape[0]
    first = (pl.program_id(0) % blocks_per_seq) == 0
    rowi = lax.broadcasted_iota(I32, (tb, 1), 0)

    def shift_lerp(c, prev8, mu):
        prev = jnp.where(first, 0.0, prev8[7:8, :])
        sh = jnp.where(rowi == 0, prev, pltpu.roll(c, 1, 0))
        return c + (sh - c) * mu

    w = BRANCH_W
    c = shift_lerp(c_ref[...], pc_ref[...], mu_ref[...])
    cs = shift_lerp(cs_ref[...], pcs_ref[...], mus_ref[...])
    r, k, v, gate = c[:, :w], c[:, w:2 * w], c[:, 2 * w:3 * w], c[:, 3 * w:]
    w_log = -RWKV_DECAY_SCALE * jax.nn.sigmoid(w0_ref[...] + _dot(jnp.tanh(cs[:, :LANES]), ww2_ref[...], HIGHEST))
    a = jax.nn.sigmoid(a0_ref[...] + _dot(cs[:, LANES:], wa2_ref[...], HIGHEST))
    kk = k * kkw_ref[...]
    k2 = k * (1.0 + (a - 1.0) * ka_ref[...])
    bd = bd_ref[...]
    kk = kk * lax.rsqrt(_head_sum(kk * kk, bd) + RWKV_KK_EPS)
    r_o[...] = r
    wl_o[...] = w_log
    k_o[...] = k2
    v_o[...] = v
    kk_o[...] = kk
    a_o[...] = a
    sg_o[...] = _silu(gate)
    bon_o[...] = _head_sum(r * k2 * rk_ref[...], bd) * v


def _rwkv_prep(pc, pcs, mu_main, mu_s, w0, ww2p, a0, wa2p, kkw, ka, rk, bd, seq):
    m = pc.shape[0]
    tb = RWKV_PREP_TB
    w = BRANCH_W
    rows = lambda i: (i, 0)
    prev = lambda i: (jnp.maximum(i * (tb // 8) - 1, 0), 0)
    const = lambda i: (0, 0)
    vec = pl.BlockSpec((1, w), const)
    out = pl.BlockSpec((tb, w), rows)
    return pl.pallas_call(
        functools.partial(_rwkv_prep_kernel, blocks_per_seq=seq // tb),
        grid=(m // tb,),
        in_specs=[pl.BlockSpec((tb, 4 * w), rows), pl.BlockSpec((tb, 2 * LANES), rows),
                  pl.BlockSpec((8, 4 * w), prev), pl.BlockSpec((8, 2 * LANES), prev),
                  pl.BlockSpec((1, 4 * w), const), pl.BlockSpec((1, 2 * LANES), const),
                  vec, pl.BlockSpec((LANES, w), const), vec, pl.BlockSpec((LANES, w), const),
                  vec, vec, vec, pl.BlockSpec((LANES, LANES), const)],
        out_specs=[out] * 8,
        out_shape=[jax.ShapeDtypeStruct((m, w), F32)] * 8,
        compiler_params=_cparams(("parallel",)),
    )(pc, pcs, pc, pcs, mu_main, mu_s, w0.reshape(1, w), ww2p, a0.reshape(1, w), wa2p,
      kkw.reshape(1, w), ka.reshape(1, w), rk.reshape(1, w), bd)


def _rwkv_scan_kernel(r_ref, wl_ref, k_ref, v_ref, kk_ref, a_ref, y_ref, lin_scr, off_scr, *, prec):
    n = RWKV_L
    hd = RWKV_HD
    nc = r_ref.shape[0] // n
    ii = lax.broadcasted_iota(I32, (n, n), 0)
    jj = lax.broadcasted_iota(I32, (n, n), 1)
    strict = ii > jj
    incl = ii >= jj
    same_blk = (ii // 16) == (jj // 16)
    tril = incl.astype(F32)
    eye = (ii == jj).astype(F32)
    mm = functools.partial(_dot, precision=prec)

    def tri_inv(a_low):
        a_d = jnp.where(same_blk, a_low, 0.0)
        a_o = a_low - a_d
        a2 = mm(a_d, a_d)
        a4 = mm(a2, a2)
        a8 = mm(a4, a4)
        t_d = mm(mm(eye + a_d, eye + a2), mm(eye + a4, eye + a8))
        nn = mm(t_d, a_o)
        n2 = mm(nn, nn)
        return mm(mm(eye + nn, eye + n2), t_d)

    for hh in range(LANES // hd):
        lanes = slice(hh * hd, (hh + 1) * hd)

        def build(c, carry, lanes=lanes):
            rows = pl.ds(pl.multiple_of(c * n, n), n)
            wl = wl_ref[rows, lanes]
            r = r_ref[rows, lanes]
            k = k_ref[rows, lanes]
            v = v_ref[rows, lanes]
            kk = kk_ref[rows, lanes]
            beta = kk * a_ref[rows, lanes]
            cum = _dot(tril, wl, HIGHEST)
            cum_end = cum[n - 1:n, :]
            g_inv = jnp.exp(-cum)
            g_end = jnp.exp(cum_end - cum)
            a_t = -kk * jnp.exp(cum - wl)
            r_t = r * jnp.exp(cum)
            b_t = beta * g_inv
            k_t = k * g_inv
            a_ab = jnp.where(strict, _dot_nt(a_t, b_t, prec), 0.0)
            a_ak = jnp.where(strict, _dot_nt(a_t, k_t, prec), 0.0)
            a_rb = jnp.where(incl, _dot_nt(r_t, b_t, prec), 0.0)
            a_rk = jnp.where(incl, _dot_nt(r_t, k_t, prec), 0.0)
            t_inv = tri_inv(a_ab)
            p = mm(t_inv, a_t)
            q = mm(t_inv, mm(a_ak, v))
            b_h = beta * g_end
            k_h = k * g_end
            lin_scr[c, :n, :] = r_t + mm(a_rb, p)
            lin_scr[c, n:, :] = eye * jnp.exp(cum_end) + _dot_tn(b_h, p, prec)
            off_scr[c, :n, :] = mm(a_rb, q) + mm(a_rk, v)
            off_scr[c, n:, :] = _dot_tn(b_h, q, prec) + _dot_tn(k_h, v, prec)
            return carry

        lax.fori_loop(0, nc, build, 0)

        def sweep(c, st, lanes=lanes):
            rows = pl.ds(pl.multiple_of(c * n, n), n)
            both = mm(lin_scr[c], st)
            y_ref[rows, lanes] = both[:n] + off_scr[c, :n, :]
            return both[n:] + off_scr[c, n:, :]

        lax.fori_loop(0, nc, sweep, jnp.zeros((hd, hd), F32))


def _rwkv_scan(r, wl, k2, v, kk, a, bsz, seq, prec):
    m = r.shape[0]
    nc = seq // RWKV_L
    npair = BRANCH_W // LANES
    blk = pl.BlockSpec((seq, LANES), lambda b, p: (b, p))
    scr = pltpu.VMEM((nc, 2 * RWKV_L, RWKV_HD), F32)
    return pl.pallas_call(
        functools.partial(_rwkv_scan_kernel, prec=prec),
        grid=(bsz, npair),
        in_specs=[blk] * 6,
        out_specs=blk,
        out_shape=jax.ShapeDtypeStruct((m, BRANCH_W), F32),
        scratch_shapes=[scr, scr],
        compiler_params=_cparams(("parallel", "parallel")),
    )(r, wl, k2, v, kk, a)


def _rwkv_post_kernel(y_ref, bon_ref, sg_ref, g_ref, b_ref, bd_ref, o_ref):
    y = y_ref[...]
    bd = bd_ref[...]
    mu = _head_sum(y, bd) * (1.0 / RWKV_HD)
    yc = y - mu
    var = _head_sum(yc * yc, bd) * (1.0 / RWKV_HD)
    yn = yc * lax.rsqrt(var + RWKV_GN_EPS) * g_ref[...] + b_ref[...]
    o_ref[...] = ((yn + bon_ref[...]) * sg_ref[...]).astype(o_ref.dtype)


def _rwkv_post(y, bon, sg, g, b, bd):
    m, w = y.shape
    tb = RWKV_POST_TB
    rows = pl.BlockSpec((tb, w), lambda i: (i, 0))
    vec = pl.BlockSpec((1, w), lambda i: (0, 0))
    return pl.pallas_call(
        _rwkv_post_kernel,
        grid=(m // tb,),
        in_specs=[rows, rows, rows, vec, vec, pl.BlockSpec((LANES, LANES), lambda i: (0, 0))],
        out_specs=rows,
        out_shape=jax.ShapeDtypeStruct((m, w), BF16),
        compiler_params=_cparams(("parallel",)),
    )(y, bon, sg, g.reshape(1, w), b.reshape(1, w), bd)


def _sgu_kernel(u_ref, v_ref, gate_ref, lng_ref, lnb_ref, ws_ref, bs_ref, o_ref):
    inv_sqrt2 = 0.7071067811865476

    def gelu(z):
        return 0.5 * z * (1.0 + lax.erf(z * inv_sqrt2))

    v = gelu(v_ref[...])
    mu = jnp.mean(v, axis=-1, keepdims=True)
    vc = v - mu
    var = jnp.mean(vc * vc, axis=-1, keepdims=True)
    vn = vc * lax.rsqrt(var + SGU_LN_EPS) * lng_ref[...] + lnb_ref[...]
    t = SGU_CHUNK
    ii = lax.broadcasted_iota(I32, (t, t), 0) // CHUNK
    jj = lax.broadcasted_iota(I32, (t, t), 1) // CHUNK
    mask = jj <= ii
    for g in range(SGU_GROUPS):
        ws = jnp.where(mask, ws_ref[g], 0.0)
        sl = slice(g * SGU_GW, (g + 1) * SGU_GW)
        bias = bs_ref[:, g:g + 1]
        for c in range(u_ref.shape[0] // t):
            rows = slice(c * t, (c + 1) * t)
            sv = _dot(ws, vn[rows, sl]) + bias
            o_ref[rows, sl] = (gelu(u_ref[rows, sl]) * sv * _silu(gate_ref[rows, sl])).astype(o_ref.dtype)


def _sgu(pd, lng, lnb, ws, bs_t):
    m = pd.shape[0]
    w = BRANCH_W
    tb = SGU_TB
    return pl.pallas_call(
        _sgu_kernel,
        grid=(m // tb,),
        in_specs=[pl.BlockSpec((tb, w), lambda i: (i, 0)),
                  pl.BlockSpec((tb, w), lambda i: (i, 1)),
                  pl.BlockSpec((tb, w), lambda i: (i, 2)),
                  pl.BlockSpec((1, w), lambda i: (0, 0)),
                  pl.BlockSpec((1, w), lambda i: (0, 0)),
                  pl.BlockSpec((SGU_GROUPS, SGU_CHUNK, SGU_CHUNK), lambda i: (0, 0, 0)),
                  pl.BlockSpec((SGU_CHUNK, SGU_GROUPS), lambda i: (0, 0))],
        out_specs=pl.BlockSpec((tb, w), lambda i: (i, 0)),
        out_shape=jax.ShapeDtypeStruct((m, w), BF16),
        compiler_params=_cparams(("parallel",)),
    )(pd, pd, pd, lng.reshape(1, w), lnb.reshape(1, w), ws, bs_t)


def _merge_kernel(ya_ref, yb_ref, yc_ref, yd_ref, ga_ref, gb_ref, gc_ref, gd_ref, wp_ref, o_ref):
    acc = None
    for g, (y_ref, g_ref) in enumerate(((ya_ref, ga_ref), (yb_ref, gb_ref), (yc_ref, gc_ref), (yd_ref, gd_ref))):
        term = jax.nn.sigmoid(g_ref[...]) * _dot(y_ref[...], wp_ref[g])
        acc = term if acc is None else acc + term
    o_ref[...] = acc.astype(o_ref.dtype)


def _merge(ya, yb, yc, yd, pm, wproj):
    m = ya.shape[0]
    w = BRANCH_W
    tm, tn = MERGE_TM, MERGE_TN
    nj = D_MODEL // tn
    ysp = pl.BlockSpec((tm, w), lambda j, i: (i, 0))
    gsp = [pl.BlockSpec((tm, tn), lambda j, i, g=g: (i, g * nj + j)) for g in range(N_BRANCH)]
    return pl.pallas_call(
        _merge_kernel,
        grid=(nj, m // tm),
        in_specs=[ysp, ysp, ysp, ysp, *gsp,
                  pl.BlockSpec((N_BRANCH, w, tn), lambda j, i: (0, 0, j))],
        out_specs=pl.BlockSpec((tm, tn), lambda j, i: (i, j)),
        out_shape=jax.ShapeDtypeStruct((m, D_MODEL), BF16),
        compiler_params=_cparams(("parallel", "parallel")),
    )(ya, yb, yc, yd, pm, pm, pm, pm, wproj)


def _out_kernel(mg_ref, wo_ref, g_ref, x_ref, o_ref):
    y = _dot(mg_ref[...], wo_ref[...])
    y = y * lax.rsqrt(jnp.mean(y * y, axis=-1, keepdims=True) + NORM_EPS) * g_ref[...]
    o_ref[...] = x_ref[...] + y


def _out_proj(merged, wout, g, x2):
    m, d = x2.shape
    tm = OUT_TM
    return pl.pallas_call(
        _out_kernel,
        grid=(m // tm,),
        in_specs=[pl.BlockSpec((tm, d), lambda i: (i, 0)),
                  pl.BlockSpec((d, d), lambda i: (0, 0)),
                  pl.BlockSpec((1, d), lambda i: (0, 0)),
                  pl.BlockSpec((tm, d), lambda i: (i, 0))],
        out_specs=pl.BlockSpec((tm, d), lambda i: (i, 0)),
        out_shape=jax.ShapeDtypeStruct((m, d), F32),
        compiler_params=_cparams(("parallel",)),
    )(merged, wout, g.reshape(1, d), x2)


def _split_cols(w, sizes):
    offs = np.cumsum((0,) + tuple(sizes))
    return [w[:, int(offs[i]):int(offs[i + 1])] for i in range(len(sizes))]


def _pad_cols(w, n):
    return jnp.pad(w, ((0, 0), (0, n - w.shape[1])))


def _pad_rows(w, n):
    return jnp.pad(w, ((0, n - w.shape[0]), (0, 0)))


def _rope_tables(positions, rot_dim, width):
    inv = ROPE_THETA ** (-jnp.arange(0, rot_dim, 2, dtype=F32) / rot_dim)
    ang = positions.astype(F32).reshape(-1, 1) * inv
    cos, sin = jnp.cos(ang), jnp.sin(ang)
    m = cos.shape[0]
    rest = width - rot_dim
    cos_h = jnp.concatenate([cos, cos, jnp.ones((m, rest), F32)], axis=1)
    sin_h = jnp.concatenate([-sin, sin, jnp.zeros((m, rest), F32)], axis=1)
    reps = LANES // width
    return jnp.tile(cos_h, (1, reps)), jnp.tile(sin_h, (1, reps))


def _mm_tn(n):
    for tn in (1024, 768, 640, 512, 256, 128):
        if n % tn == 0:
            return tn
    raise ValueError(n)


def kernel(x, positions, norm_pre, norm_post, w_in, gla_w_g2, gla_b_g, gla_head_g, rwkv_mu, rwkv_w0,
           rwkv_w_w2, rwkv_a0, rwkv_w_a2, rwkv_k_k, rwkv_k_a, rwkv_r_k, rwkv_lnx_g, rwkv_lnx_b,
           sgu_ln_g, sgu_ln_b, sgu_w_s, sgu_b_s, w_proj, w_out):
    bsz, seq, d = x.shape
    depth = w_in.shape[0]
    m = bsz * seq
    x2 = x.reshape(m, d)
    ca, sa = _rope_tables(positions, DSA_ROT, DSA_HD)
    ci, si = _rope_tables(positions, IDX_ROT, IDX_HD)
    lane_grp = np.arange(LANES) // RWKV_HD
    bd = jnp.asarray((lane_grp[:, None] == lane_grp[None, :]).astype(np.float32))

    for l in range(depth):
        wa, wb, wc, wd, wm = _split_cols(w_in[l], GROUP_COLS)
        a_q, a_k, a_v, a_glr, a_gate = _split_cols(wa, A_COLS)
        b_q, b_k, b_v, b_iq, b_ik, b_iw, b_gate = _split_cols(wb, B_COLS)
        c_r, c_k, c_v, c_wlr, c_alr, c_gate = _split_cols(wc, C_COLS)
        w_a = jnp.concatenate([a_q, a_k, a_v, a_gate, _pad_cols(a_glr, LANES)], axis=1).astype(BF16)
        w_b = jnp.concatenate([b_q, b_k, b_v, b_gate], axis=1).astype(BF16)
        w_bi = jnp.concatenate([b_iq, _pad_cols(b_ik, LANES), _pad_cols(b_iw, LANES)], axis=1).astype(BF16)
        w_c = jnp.concatenate([c_r, c_k, c_v, c_gate], axis=1).astype(BF16)
        w_cs = jnp.concatenate([_pad_cols(c_wlr, LANES), _pad_cols(c_alr, LANES)], axis=1).astype(BF16)
        w_d = wd.astype(BF16)
        w_m = wm.astype(BF16)

        h = _rmsnorm_cast(x2, norm_pre[l])
        pa, pb, pbi, pc, pcs, pd, pm = (_matmul(h, w, _mm_tn(w.shape[1]))
                                        for w in (w_a, w_b, w_bi, w_c, w_cs, w_d, w_m))

        ya = _gla(pa, _pad_rows(gla_w_g2[l], LANES), gla_b_g[l], gla_head_g[l], bsz, seq)

        qr, kr, vb, iqr, ikr, iws = _dsa_prep(pb, pbi, ca, sa, ci, si)
        yb = _dsa(qr, kr, vb, iqr, ikr, iws, pb, bsz, seq)

        mu_r, mu_k, mu_v, mu_wlr, mu_alr, mu_gate = _split_cols(rwkv_mu[l][None, :], C_COLS)
        mu_main = jnp.concatenate([mu_r, mu_k, mu_v, mu_gate], axis=1)
        mu_s = jnp.concatenate([_pad_cols(mu_wlr, LANES), _pad_cols(mu_alr, LANES)], axis=1)
        r, wl, k2, v, kk, a, sg, bon = _rwkv_prep(
            pc, pcs, mu_main, mu_s, rwkv_w0[l], _pad_rows(rwkv_w_w2[l], LANES), rwkv_a0[l],
            _pad_rows(rwkv_w_a2[l], LANES), rwkv_k_k[l], rwkv_k_a[l], rwkv_r_k[l].reshape(-1), bd, seq)
        y = _rwkv_scan(r, wl, k2, v, kk, a, bsz, seq, HIGHEST)
        yc = _rwkv_post(y, bon, sg, rwkv_lnx_g[l], rwkv_lnx_b[l], bd)

        yd = _sgu(pd, sgu_ln_g[l], sgu_ln_b[l], sgu_w_s[l], sgu_b_s[l].T)

        merged = _merge(ya, yb, yc, yd, pm, w_proj[l].astype(BF16))
        x2 = _out_proj(merged, w_out[l].astype(BF16), norm_post[l], x2)
    return x2.reshape(bsz, seq, d)
```

```python
import functools

import numpy as np
import jax
import jax.numpy as jnp
from jax import lax
from jax.experimental import pallas as pl
from jax.experimental.pallas import tpu as pltpu

F32 = jnp.float32
BF16 = jnp.bfloat16
I32 = jnp.int32
HIGHEST = lax.Precision.HIGHEST

D_MODEL = 2048
CHUNK = 64
NORM_EPS = 1e-6
ROPE_THETA = 500000.0
N_BRANCH = 4
BRANCH_W = D_MODEL // 2

GLA_HEADS = 4
GLA_DK = BRANCH_W // 2 // GLA_HEADS
GLA_DV = BRANCH_W // GLA_HEADS
GLA_RANK = 16
GLA_TAU = 16.0

DSA_HEADS = 8
DSA_HD = BRANCH_W // DSA_HEADS
DSA_ROT = DSA_HD // 4
IDX_HEADS = 8
IDX_HD = 64
IDX_ROT = IDX_HD // 4
TOPK_MAX = 256

RWKV_HD = 64
RWKV_HEADS = BRANCH_W // RWKV_HD
RWKV_DECAY_RANK = 96
RWKV_A_RANK = 96
RWKV_DECAY_SCALE = 0.606531
RWKV_GN_EPS = 64e-5
RWKV_KK_EPS = 1e-12

SGU_CHUNK = 128
SGU_GROUPS = 8
SGU_GW = BRANCH_W // SGU_GROUPS
SGU_LN_EPS = 1e-5

A_COLS = (GLA_HEADS * GLA_DK, GLA_HEADS * GLA_DK, BRANCH_W, GLA_RANK, BRANCH_W)
B_COLS = (BRANCH_W, BRANCH_W, BRANCH_W, IDX_HEADS * IDX_HD, IDX_HD, IDX_HEADS, BRANCH_W)
C_COLS = (BRANCH_W, BRANCH_W, BRANCH_W, RWKV_DECAY_RANK, RWKV_A_RANK, BRANCH_W)
D_COLS = (2 * BRANCH_W, BRANCH_W)
GROUP_COLS = (sum(A_COLS), sum(B_COLS), sum(C_COLS), sum(D_COLS), N_BRANCH * D_MODEL)

LANES = 128
VMEM_LIMIT = 56 * 1024 * 1024
MXU_DEPTH = 256
IDX_K = MXU_DEPTH
NEG_BIG = -1e30
LOG2E = 1.4426950408889634
INT_MIN = -(2 ** 31)

NORM_TM = 512
MM_TM = 1024
GLA_TB = 512
DSA_PREP_TB = 512
DSA_TQ = 128
DSA_KB = 512
RWKV_PREP_TB = 256
RWKV_L = 64
RWKV_UNROLL = 8
RWKV_POST_TB = 512
SGU_TB = 512
MERGE_TM = 512
MERGE_TN = 1024
OUT_TM = 512


def _cparams(sem):
    return pltpu.CompilerParams(dimension_semantics=sem, vmem_limit_bytes=VMEM_LIMIT)


def _dot(a, b, precision=None):
    return jnp.dot(a, b, preferred_element_type=F32, precision=precision)


def _dot_nt(a, b, precision=None):
    return lax.dot_general(a, b, (((1,), (1,)), ((), ())), preferred_element_type=F32,
                           precision=precision)


def _dot_tn(a, b, precision=None):
    return lax.dot_general(a, b, (((0,), (0,)), ((), ())), preferred_element_type=F32,
                           precision=precision)


def _silu(x):
    return x * jax.nn.sigmoid(x)


def _rmsnorm_cast_kernel(x_ref, g_ref, o_ref):
    x = x_ref[...]
    ms = jnp.mean(x * x, axis=-1, keepdims=True)
    o_ref[...] = (x * lax.rsqrt(ms + NORM_EPS) * g_ref[...]).astype(o_ref.dtype)


def _rmsnorm_cast(x2, g):
    m, d = x2.shape
    return pl.pallas_call(
        _rmsnorm_cast_kernel,
        name="prenorm",
        grid=(m // NORM_TM,),
        in_specs=[pl.BlockSpec((NORM_TM, d), lambda i: (i, 0)),
                  pl.BlockSpec((1, d), lambda i: (0, 0))],
        out_specs=pl.BlockSpec((NORM_TM, d), lambda i: (i, 0)),
        out_shape=jax.ShapeDtypeStruct((m, d), BF16),
        compiler_params=_cparams(("parallel",)),
    )(x2, g.reshape(1, d))


def _matmul_kernel(a_ref, b_ref, o_ref):
    o_ref[...] = _dot(a_ref[...], b_ref[...]).astype(o_ref.dtype)


def _matmul(a, b, tn, out_dtype=F32):
    m, k = a.shape
    n = b.shape[1]
    tm = min(MM_TM, m)
    return pl.pallas_call(
        _matmul_kernel,
        name="proj",
        grid=(m // tm, n // tn),
        in_specs=[pl.BlockSpec((tm, k), lambda i, j: (i, 0)),
                  pl.BlockSpec((k, tn), lambda i, j: (0, j))],
        out_specs=pl.BlockSpec((tm, tn), lambda i, j: (i, j)),
        out_shape=jax.ShapeDtypeStruct((m, n), out_dtype),
        compiler_params=_cparams(("parallel", "parallel")),
    )(a, b)


def _gla_kernel(q_ref, k_ref, v_ref, gate_ref, glr_ref, wg2_ref, bg_ref, hg_ref, o_ref, state_ref):
    @pl.when(pl.program_id(2) == 0)
    def _():
        state_ref[...] = jnp.zeros_like(state_ref)

    ii = lax.broadcasted_iota(I32, (CHUNK, CHUNK), 0)
    jj = lax.broadcasted_iota(I32, (CHUNK, CHUNK), 1)
    tril = (ii >= jj).astype(F32)
    for c in range(GLA_TB // CHUNK):
        sl = pl.ds(c * CHUNK, CHUNK)
        q = q_ref[sl, :] * (GLA_DK ** -0.5)
        k = k_ref[sl, :]
        v = v_ref[sl, :]
        z = _dot(glr_ref[sl, :], wg2_ref[...], HIGHEST) + bg_ref[...]
        log_a = (jnp.minimum(z, 0.0) - jnp.log1p(jnp.exp(-jnp.abs(z)))) / GLA_TAU
        cum = _dot(tril, log_a, HIGHEST)
        total = cum[CHUNK - 1:CHUNK, :]
        k_dec = k * jnp.exp(total - cum)
        st = state_ref[...] * jnp.exp(total) + _dot_tn(v.astype(BF16), k_dec.astype(BF16))
        state_ref[...] = st
        o = _dot_nt(q.astype(BF16), st.astype(BF16))
        o = o * lax.rsqrt(jnp.mean(o * o, axis=-1, keepdims=True) + NORM_EPS) * hg_ref[...]
        o_ref[sl, :] = (o * _silu(gate_ref[sl, :])).astype(o_ref.dtype)


def _gla(pa, wg2p, bg, hg, bsz, seq):
    m = pa.shape[0]
    nt = seq // GLA_TB
    row = lambda b, h, t: b * nt + t
    kb = GLA_HEADS
    vb = (2 * GLA_HEADS * GLA_DK) // GLA_DV
    gb = vb + GLA_HEADS
    lb = (2 * GLA_HEADS * GLA_DK + 2 * BRANCH_W) // LANES
    return pl.pallas_call(
        _gla_kernel,
        name="gla",
        grid=(bsz, GLA_HEADS, nt),
        in_specs=[
            pl.BlockSpec((GLA_TB, GLA_DK), lambda b, h, t: (row(b, h, t), h)),
            pl.BlockSpec((GLA_TB, GLA_DK), lambda b, h, t: (row(b, h, t), kb + h)),
            pl.BlockSpec((GLA_TB, GLA_DV), lambda b, h, t: (row(b, h, t), vb + h)),
            pl.BlockSpec((GLA_TB, GLA_DV), lambda b, h, t: (row(b, h, t), gb + h)),
            pl.BlockSpec((GLA_TB, LANES), lambda b, h, t: (row(b, h, t), lb)),
            pl.BlockSpec((LANES, GLA_DK), lambda b, h, t: (0, h)),
            pl.BlockSpec((1, GLA_DK), lambda b, h, t: (0, h)),
            pl.BlockSpec((1, GLA_DV), lambda b, h, t: (0, 0)),
        ],
        out_specs=pl.BlockSpec((GLA_TB, GLA_DV), lambda b, h, t: (row(b, h, t), h)),
        out_shape=jax.ShapeDtypeStruct((m, BRANCH_W), BF16),
        scratch_shapes=[pltpu.VMEM((GLA_DV, GLA_DK), F32)],
        compiler_params=_cparams(("parallel", "parallel", "arbitrary")),
    )(pa, pa, pa, pa, pa, wg2p, bg.reshape(1, -1), hg.reshape(1, -1))


def _rope_tile(x, cos_t, sin_t, half, width):
    lane = lax.broadcasted_iota(I32, x.shape, 1) % width
    partner = jnp.where(lane < half, pltpu.roll(x, LANES - half, 1), pltpu.roll(x, half, 1))
    return x * cos_t + partner * sin_t


def _dsa_prep_kernel(q_ref, k_ref, v_ref, idx_ref, ca_ref, sa_ref, ci_ref, si_ref,
                     qo_ref, ko_ref, vo_ref, iqo_ref, iko_ref, iwo_ref):
    ca, sa = ca_ref[...], sa_ref[...]
    ci, si = ci_ref[...], si_ref[...]
    for h in range(DSA_HEADS):
        sl = slice(h * DSA_HD, (h + 1) * DSA_HD)
        q = _rope_tile(q_ref[:, sl], ca, sa, DSA_ROT // 2, DSA_HD) * (DSA_HD ** -0.5 * LOG2E)
        qo_ref[:, sl] = q.astype(qo_ref.dtype)
        ko_ref[:, sl] = _rope_tile(k_ref[:, sl], ca, sa, DSA_ROT // 2, DSA_HD).astype(ko_ref.dtype)
    vo_ref[...] = v_ref[...].astype(vo_ref.dtype)
    niq = IDX_HEADS * IDX_HD
    low = lax.broadcasted_iota(I32, (q_ref.shape[0], LANES), 1) < IDX_HD

    def hi_lo(x):
        hi = x.astype(BF16).astype(F32)
        return hi, x - hi

    for j in range(niq // LANES):
        x = _rope_tile(idx_ref[:, j * LANES:(j + 1) * LANES], ci, si, IDX_ROT // 2, IDX_HD)
        hi, lo = hi_lo(x)
        hi_sw = pltpu.roll(hi, IDX_HD, 1)
        lo_sw = pltpu.roll(lo, IDX_HD, 1)
        base = 2 * j * IDX_K
        iqo_ref[:, base:base + LANES] = jnp.where(low, hi, hi_sw).astype(BF16)
        iqo_ref[:, base + LANES:base + IDX_K] = jnp.where(low, lo, 0.0).astype(BF16)
        iqo_ref[:, base + IDX_K:base + IDX_K + LANES] = jnp.where(low, hi_sw, hi).astype(BF16)
        iqo_ref[:, base + IDX_K + LANES:base + 2 * IDX_K] = jnp.where(low, lo_sw, 0.0).astype(BF16)
    hi, lo = hi_lo(_rope_tile(idx_ref[:, niq:niq + LANES], ci, si, IDX_ROT // 2, IDX_HD))
    iko_ref[:, :LANES] = jnp.where(low, hi, pltpu.roll(lo, IDX_HD, 1)).astype(BF16)
    iko_ref[:, LANES:] = jnp.where(low, hi, 0.0).astype(BF16)
    iwo_ref[...] = idx_ref[:, niq + LANES:niq + 2 * LANES] * (niq ** -0.5)


def _dsa_prep(pb, pbi, ca, sa, ci, si):
    m = pb.shape[0]
    tb = DSA_PREP_TB
    w = BRANCH_W
    niq = IDX_HEADS * IDX_HD
    rows = lambda i: (i, 0)
    return pl.pallas_call(
        _dsa_prep_kernel,
        name="dsa_prep",
        grid=(m // tb,),
        in_specs=[pl.BlockSpec((tb, w), lambda i: (i, 0)),
                  pl.BlockSpec((tb, w), lambda i: (i, 1)),
                  pl.BlockSpec((tb, w), lambda i: (i, 2)),
                  pl.BlockSpec((tb, niq + 2 * LANES), rows),
                  pl.BlockSpec((tb, LANES), rows), pl.BlockSpec((tb, LANES), rows),
                  pl.BlockSpec((tb, LANES), rows), pl.BlockSpec((tb, LANES), rows)],
        out_specs=[pl.BlockSpec((tb, w), rows), pl.BlockSpec((tb, w), rows), pl.BlockSpec((tb, w), rows),
                   pl.BlockSpec((tb, IDX_HEADS * IDX_K), rows), pl.BlockSpec((tb, IDX_K), rows),
                   pl.BlockSpec((tb, LANES), rows)],
        out_shape=[jax.ShapeDtypeStruct((m, w), BF16), jax.ShapeDtypeStruct((m, w), BF16),
                   jax.ShapeDtypeStruct((m, w), BF16), jax.ShapeDtypeStruct((m, IDX_HEADS * IDX_K), BF16),
                   jax.ShapeDtypeStruct((m, IDX_K), BF16), jax.ShapeDtypeStruct((m, LANES), F32)],
        compiler_params=_cparams(("parallel",)),
    )(pb, pb, pb, pbi, ca, sa, ci, si)


def _dsa_kernel(q_ref, k_ref, v_ref, iq_ref, ik_ref, iw_ref, gate_ref, o_ref, key_scr, bias_scr, tie_scr,
                *, topk):
    tq, kb_w = DSA_TQ, DSA_KB
    q0 = pl.program_id(1) * tq
    nkb = (q0 + tq + kb_w - 1) // kb_w
    row = lax.broadcasted_iota(I32, (tq, 1), 0)
    qlim = ((q0 + row) // CHUNK + 1) * CHUNK
    lane_k = lax.broadcasted_iota(I32, (1, kb_w), 1)
    iw = iw_ref[...]

    def score_body(kb, carry):
        ik = ik_ref[pl.ds(pl.multiple_of(kb * kb_w, kb_w), kb_w), :]
        dots = [_dot_nt(iq_ref[:, h * IDX_K:(h + 1) * IDX_K], ik) for h in range(IDX_HEADS)]
        acc = jnp.zeros((tq, kb_w), F32)
        for h, d in enumerate(dots):
            acc = acc + iw[:, h:h + 1] * jnp.maximum(d, 0.0)
        acc = acc + 0.0
        bits = lax.bitcast_convert_type(acc, I32)
        key = bits ^ ((bits >> 31) & 0x7FFFFFFF)
        sidx = kb * kb_w + lane_k
        key_scr[kb] = jnp.where(sidx < qlim, key, INT_MIN)
        return carry

    lax.fori_loop(0, nkb, score_body, 0)

    def count(pred):
        def body(kb, c):
            m = pred(key_scr[kb], kb * kb_w + lane_k).astype(I32)
            for j in range(kb_w // LANES):
                c = c + m[:, j * LANES:(j + 1) * LANES]
            return c
        c = lax.fori_loop(0, nkb, body, jnp.zeros((tq, LANES), I32))
        return jnp.sum(c.astype(F32), axis=1, keepdims=True).astype(I32)

    def bit_body(it, carry):
        res, cnt_res = carry
        trial = res | lax.shift_left(jnp.int32(1), 31 - it)
        cand = trial ^ INT_MIN
        cnt = count(lambda kt, s: kt >= cand)
        ok = cnt >= topk
        return jnp.where(ok, trial, res), jnp.where(ok, cnt, cnt_res)

    res, cnt_ge = lax.fori_loop(0, 32, bit_body,
                                (jnp.zeros((tq, 1), I32), jnp.full((tq, 1), nkb * kb_w, I32)))
    thr = res ^ INT_MIN
    seq_len = key_scr.shape[0] * kb_w
    tie_scr[...] = jnp.full(tie_scr.shape, seq_len, I32)
    has_tie = (cnt_ge > topk) & (thr != INT_MIN)

    @pl.when(jnp.max(jnp.where(has_tie, 1.0, 0.0)) > 0.5)
    def _():
        need = topk - count(lambda kt, s: kt > thr)
        nbits = max(1, int(np.ceil(np.log2(seq_len))))

        def tie_body(it, resj):
            trial = resj | lax.shift_left(jnp.int32(1), nbits - 1 - it)
            c = count(lambda kt, s: (kt == thr) & (s < trial))
            return jnp.where(c < need, trial, resj)

        tie_scr[...] = lax.fori_loop(0, nbits, tie_body, jnp.zeros((tq, 1), I32))

    last_tie = tie_scr[...]

    def bias_body(kb, carry):
        kt = key_scr[kb]
        sidx = kb * kb_w + lane_k
        sel = ((kt > thr) | ((kt == thr) & (sidx <= last_tie))) & (kt != INT_MIN)
        bias_scr[kb] = jnp.where(sel, 0.0, NEG_BIG)
        return carry

    lax.fori_loop(0, nkb, bias_body, 0)

    heads = [slice(h * DSA_HD, (h + 1) * DSA_HD) for h in range(DSA_HEADS)]

    def att_body(kb, carry):
        rows = pl.ds(pl.multiple_of(kb * kb_w, kb_w), kb_w)
        bias = bias_scr[kb]
        s = [_dot_nt(q_ref[:, sl], k_ref[rows, sl]) + bias for sl in heads]
        m_n = [jnp.maximum(c[0], jnp.max(x, axis=1, keepdims=True)) for c, x in zip(carry, s)]
        p = [jnp.exp2(x - m) for x, m in zip(s, m_n)]
        pv = [_dot(x.astype(BF16), v_ref[rows, sl]) for x, sl in zip(p, heads)]
        alpha = [jnp.exp2(c[0] - m) for c, m in zip(carry, m_n)]
        l_n = [a * c[1] + jnp.sum(x, axis=1, keepdims=True) for a, c, x in zip(alpha, carry, p)]
        return tuple((m, l, a * c[2] + y) for m, l, a, c, y in zip(m_n, l_n, alpha, carry, pv))

    init = tuple((jnp.full((tq, 1), NEG_BIG, F32), jnp.zeros((tq, 1), F32), jnp.zeros((tq, DSA_HD), F32))
                 for _ in heads)
    final = lax.fori_loop(0, nkb, att_body, init)
    for sl, (_, l_f, acc_f) in zip(heads, final):
        o_ref[:, sl] = (acc_f / l_f * _silu(gate_ref[:, sl])).astype(o_ref.dtype)


def _dsa(qr, kr, vb, iqr, ikr, iws, pb, bsz, seq):
    m = qr.shape[0]
    w = BRANCH_W
    nq = seq // DSA_TQ
    nkb = seq // DSA_KB
    topk = min(TOPK_MAX, seq // 4)
    qrow = lambda b, i: (b * nq + i, 0)
    return pl.pallas_call(
        functools.partial(_dsa_kernel, topk=topk),
        name="dsa",
        grid=(bsz, nq),
        in_specs=[pl.BlockSpec((DSA_TQ, w), qrow),
                  pl.BlockSpec((seq, w), lambda b, i: (b, 0)),
                  pl.BlockSpec((seq, w), lambda b, i: (b, 0)),
                  pl.BlockSpec((DSA_TQ, IDX_HEADS * IDX_K), qrow),
                  pl.BlockSpec((seq, IDX_K), lambda b, i: (b, 0)),
                  pl.BlockSpec((DSA_TQ, LANES), qrow),
                  pl.BlockSpec((DSA_TQ, w), lambda b, i: (b * nq + i, 3))],
        out_specs=pl.BlockSpec((DSA_TQ, w), qrow),
        out_shape=jax.ShapeDtypeStruct((m, w), BF16),
        scratch_shapes=[pltpu.VMEM((nkb, DSA_TQ, DSA_KB), I32),
                        pltpu.VMEM((nkb, DSA_TQ, DSA_KB), F32),
                        pltpu.VMEM((DSA_TQ, 1), I32)],
        compiler_params=_cparams(("parallel", "arbitrary")),
    )(qr, kr, vb, iqr, ikr, iws, pb)


def _head_sum(x, bd):
    parts = [_dot(x[:, j * LANES:(j + 1) * LANES], bd, HIGHEST) for j in range(x.shape[1] // LANES)]
    return jnp.concatenate(parts, axis=1)


def _rwkv_prep_kernel(c_ref, cs_ref, pc_ref, pcs_ref, mu_ref, mus_ref, w0_ref, ww2_ref, a0_ref, wa2_ref,
                      kkw_ref, ka_ref, rk_ref, bd_ref,
                      r_o, wl_o, k_o, v_o, kk_o, a_o, sg_o, bon_o, *, blocks_per_seq):
    tb = c_ref.shape[0]
    first = (pl.program_id(0) % blocks_per_seq) == 0
    rowi = lax.broadcasted_iota(I32, (tb, 1), 0)

    def shift_lerp(c, prev8, mu):
        prev = jnp.where(first, 0.0, prev8[7:8, :])
        sh = jnp.where(rowi == 0, prev, pltpu.roll(c, 1, 0))
        return c + (sh - c) * mu

    w = BRANCH_W
    c = shift_lerp(c_ref[...], pc_ref[...], mu_ref[...])
    cs = shift_lerp(cs_ref[...], pcs_ref[...], mus_ref[...])
    r, k, v, gate = c[:, :w], c[:, w:2 * w], c[:, 2 * w:3 * w], c[:, 3 * w:]
    w_log = -RWKV_DECAY_SCALE * jax.nn.sigmoid(w0_ref[...] + _dot(jnp.tanh(cs[:, :LANES]), ww2_ref[...], HIGHEST))
    a = jax.nn.sigmoid(a0_ref[...] + _dot(cs[:, LANES:], wa2_ref[...], HIGHEST))
    kk = k * kkw_ref[...]
    k2 = k * (1.0 + (a - 1.0) * ka_ref[...])
    bd = bd_ref[...]
    kk = kk * lax.rsqrt(_head_sum(kk * kk, bd) + RWKV_KK_EPS)
    r_o[...] = r
    wl_o[...] = w_log
    k_o[...] = k2
    v_o[...] = v
    kk_o[...] = kk
    a_o[...] = a
    sg_o[...] = _silu(gate)
    bon_o[...] = _head_sum(r * k2 * rk_ref[...], bd) * v


def _rwkv_prep(pc, pcs, mu_main, mu_s, w0, ww2p, a0, wa2p, kkw, ka, rk, bd, seq):
    m = pc.shape[0]
    tb = RWKV_PREP_TB
    w = BRANCH_W
    rows = lambda i: (i, 0)
    prev = lambda i: (jnp.maximum(i * (tb // 8) - 1, 0), 0)
    const = lambda i: (0, 0)
    vec = pl.BlockSpec((1, w), const)
    out = pl.BlockSpec((tb, w), rows)
    return pl.pallas_call(
        functools.partial(_rwkv_prep_kernel, blocks_per_seq=seq // tb),
        name="rwkv_prep",
        grid=(m // tb,),
        in_specs=[pl.BlockSpec((tb, 4 * w), rows), pl.BlockSpec((tb, 2 * LANES), rows),
                  pl.BlockSpec((8, 4 * w), prev), pl.BlockSpec((8, 2 * LANES), prev),
                  pl.BlockSpec((1, 4 * w), const), pl.BlockSpec((1, 2 * LANES), const),
                  vec, pl.BlockSpec((LANES, w), const), vec, pl.BlockSpec((LANES, w), const),
                  vec, vec, vec, pl.BlockSpec((LANES, LANES), const)],
        out_specs=[out] * 8,
        out_shape=[jax.ShapeDtypeStruct((m, w), F32)] * 8,
        compiler_params=_cparams(("parallel",)),
    )(pc, pcs, pc, pcs, mu_main, mu_s, w0.reshape(1, w), ww2p, a0.reshape(1, w), wa2p,
      kkw.reshape(1, w), ka.reshape(1, w), rk.reshape(1, w), bd)


def _rwkv_scan_kernel(r_ref, wl_ref, k_ref, v_ref, kk_ref, a_ref, y_ref, lin_scr, yq_scr, h_scr, *, unroll):
    n = RWKV_L
    two = 2 * n
    nc = r_ref.shape[0] // n
    i2 = lax.broadcasted_iota(I32, (two, two), 0)
    j2 = lax.broadcasted_iota(I32, (two, two), 1)
    strict = i2 > j2
    incl = i2 >= j2
    same16 = (i2 // 16) == (j2 // 16)
    same_head = (i2 // RWKV_HD) == (j2 // RWKV_HD)
    eye = (i2 == j2).astype(F32)
    ii = lax.broadcasted_iota(I32, (n, n), 0)
    jj = lax.broadcasted_iota(I32, (n, n), 1)
    tril = (ii >= jj).astype(F32)
    head0 = lax.broadcasted_iota(I32, (n, LANES), 1) < RWKV_HD

    def mm(a, b):
        return _dot(a.astype(BF16), b.astype(BF16))

    def mm_nt(a, b):
        return _dot_nt(a.astype(BF16), b.astype(BF16))

    def mm_tn(a, b):
        return _dot_tn(a.astype(BF16), b.astype(BF16))

    def stack(x):
        return jnp.concatenate([jnp.where(head0, x, 0.0), jnp.where(head0, 0.0, x)], axis=0)

    def fold(x):
        return x[:n] + x[n:]

    def each(f, *lists):
        return [f(*args) for args in zip(*lists)]

    def tri_inv(a_low):
        a_d = each(lambda a: jnp.where(same16, a, 0.0), a_low)
        a_o = each(lambda a, d: a - d, a_low, a_d)
        a2 = each(mm, a_d, a_d)
        tick()
        a4 = each(mm, a2, a2)
        t_lo = each(lambda d, s: mm(eye + d, eye + s), a_d, a2)
        tick()
        a8 = each(mm, a4, a4)
        tick()
        t_hi = each(lambda x, y: mm(eye + x, eye + y), a4, a8)
        tick()
        t_d = each(mm, t_lo, t_hi)
        tick()
        nn = each(mm, t_d, a_o)
        tick()
        n2 = each(mm, nn, nn)
        tick()
        t_o = each(lambda x, y: mm(eye + x, eye + y), nn, n2)
        tick()
        return each(mm, t_o, t_d)

    pending = []

    def tick():
        if pending:
            pending.pop(0)()

    def build_group(g):
        rows = [pl.ds(pl.multiple_of((g * unroll + u) * n, n), n) for u in range(unroll)]
        wl = [wl_ref[r, :] for r in rows]
        k = [k_ref[r, :] for r in rows]
        v = [v_ref[r, :] for r in rows]
        kk = [kk_ref[r, :] for r in rows]
        beta = each(lambda x, r: x * a_ref[r, :], kk, rows)
        cum = each(lambda x: _dot(tril, x, HIGHEST), wl)
        cum_end = each(lambda x: x[n - 1:n, :], cum)
        g_inv = each(lambda x: jnp.exp(-x), cum)
        g_end = each(lambda e, x: jnp.exp(e - x), cum_end, cum)
        a_st = each(lambda x, c, w: stack(-x * jnp.exp(c - w)), kk, cum, wl)
        r_st = each(lambda r, c: stack(r_ref[r, :] * jnp.exp(c)), rows, cum)
        b_st = each(lambda x, y: stack(x * y), beta, g_inv)
        k_st = each(lambda x, y: stack(x * y), k, g_inv)
        v_st = each(stack, v)
        ar = each(lambda x, y: jnp.concatenate([x, y], axis=0), a_st, r_st)
        gb = each(mm_nt, ar, b_st)
        gk = each(mm_nt, ar, k_st)
        a_ab = each(lambda x: jnp.where(strict, x[:two], 0.0), gb)
        a_rb = each(lambda x: jnp.where(incl, x[two:], 0.0), gb)
        a_ak = each(lambda x: jnp.where(strict, x[:two], 0.0), gk)
        a_rk = each(lambda x: jnp.where(incl, x[two:], 0.0), gk)
        av2 = each(lambda x, y, z: mm(jnp.concatenate([x, y], axis=0), z), a_ak, a_rk, v_st)
        t_inv = tri_inv(a_ab)
        pq = each(lambda t, x, y: mm(t, jnp.concatenate([x, y[:two]], axis=1)), t_inv, a_st, av2)
        yr = each(mm, a_rb, pq)
        kv = each(lambda x, y, z: mm_tn(x * y, z), k, g_end, v)
        pq_d = each(lambda x: jnp.concatenate([fold(x[:, :LANES]), fold(x[:, LANES:])], axis=1), pq)
        gh = each(lambda x, y, z: mm_tn(x * y, z), beta, g_end, pq_d)
        while pending:
            tick()
        for u in range(unroll):
            ry = fold(r_st[u] + yr[u][:, :LANES])
            g_bd = jnp.where(same_head, gh[u][:, :LANES], 0.0) + eye * jnp.exp(cum_end[u])
            lin_scr[u] = jnp.concatenate([ry, g_bd], axis=0)
            yq_scr[u] = fold(yr[u][:, LANES:] + av2[u][two:])
            h_scr[u] = jnp.where(same_head, gh[u][:, LANES:] + kv[u], 0.0)

    def sweep_steps(g, state):
        def step(u):
            rows = pl.ds(pl.multiple_of((g * unroll + u) * n, n), n)
            both = mm(lin_scr[u], state[0])
            y_ref[rows, :] = both[:n] + yq_scr[u]
            state[0] = both[n:] + h_scr[u]
        return [functools.partial(step, u) for u in range(unroll)]

    ngroups = nc // unroll
    build_group(0)

    def body(g, st):
        state = [st]
        pending.extend(sweep_steps(g - 1, state))
        build_group(g)
        return state[0]

    st = lax.fori_loop(1, ngroups, body, jnp.zeros((two, two), F32))
    state = [st]
    for step in sweep_steps(ngroups - 1, state):
        step()


def _rwkv_scan(r, wl, k2, v, kk, a, bsz, seq):
    m = r.shape[0]
    npair = BRANCH_W // LANES
    blk = pl.BlockSpec((seq, LANES), lambda b, p: (b, p))
    u = RWKV_UNROLL
    return pl.pallas_call(
        functools.partial(_rwkv_scan_kernel, unroll=u),
        name="rwkv_scan",
        grid=(bsz, npair),
        in_specs=[blk] * 6,
        out_specs=blk,
        out_shape=jax.ShapeDtypeStruct((m, BRANCH_W), F32),
        scratch_shapes=[pltpu.VMEM((u, 3 * RWKV_L, LANES), F32), pltpu.VMEM((u, RWKV_L, LANES), F32),
                        pltpu.VMEM((u, LANES, LANES), F32)],
        compiler_params=_cparams(("parallel", "parallel")),
    )(r, wl, k2, v, kk, a)


def _rwkv_post_kernel(y_ref, bon_ref, sg_ref, g_ref, b_ref, bd_ref, o_ref):
    y = y_ref[...]
    bd = bd_ref[...]
    mu = _head_sum(y, bd) * (1.0 / RWKV_HD)
    yc = y - mu
    var = _head_sum(yc * yc, bd) * (1.0 / RWKV_HD)
    yn = yc * lax.rsqrt(var + RWKV_GN_EPS) * g_ref[...] + b_ref[...]
    o_ref[...] = ((yn + bon_ref[...]) * sg_ref[...]).astype(o_ref.dtype)


def _rwkv_post(y, bon, sg, g, b, bd):
    m, w = y.shape
    tb = RWKV_POST_TB
    rows = pl.BlockSpec((tb, w), lambda i: (i, 0))
    vec = pl.BlockSpec((1, w), lambda i: (0, 0))
    return pl.pallas_call(
        _rwkv_post_kernel,
        name="rwkv_post",
        grid=(m // tb,),
        in_specs=[rows, rows, rows, vec, vec, pl.BlockSpec((LANES, LANES), lambda i: (0, 0))],
        out_specs=rows,
        out_shape=jax.ShapeDtypeStruct((m, w), BF16),
        compiler_params=_cparams(("parallel",)),
    )(y, bon, sg, g.reshape(1, w), b.reshape(1, w), bd)


def _sgu_kernel(u_ref, v_ref, gate_ref, lng_ref, lnb_ref, ws_ref, bs_ref, o_ref):
    inv_sqrt2 = 0.7071067811865476

    def gelu(z):
        return 0.5 * z * (1.0 + lax.erf(z * inv_sqrt2))

    v = gelu(v_ref[...])
    mu = jnp.mean(v, axis=-1, keepdims=True)
    vc = v - mu
    var = jnp.mean(vc * vc, axis=-1, keepdims=True)
    vn = vc * lax.rsqrt(var + SGU_LN_EPS) * lng_ref[...] + lnb_ref[...]
    t = SGU_CHUNK
    ii = lax.broadcasted_iota(I32, (t, t), 0) // CHUNK
    jj = lax.broadcasted_iota(I32, (t, t), 1) // CHUNK
    mask = jj <= ii
    for g in range(SGU_GROUPS):
        ws = jnp.where(mask, ws_ref[g], 0.0)
        sl = slice(g * SGU_GW, (g + 1) * SGU_GW)
        bias = bs_ref[:, g:g + 1]
        for c in range(u_ref.shape[0] // t):
            rows = slice(c * t, (c + 1) * t)
            sv = _dot(ws, vn[rows, sl]) + bias
            o_ref[rows, sl] = (gelu(u_ref[rows, sl]) * sv * _silu(gate_ref[rows, sl])).astype(o_ref.dtype)


def _sgu(pd, lng, lnb, ws, bs_t):
    m = pd.shape[0]
    w = BRANCH_W
    tb = SGU_TB
    return pl.pallas_call(
        _sgu_kernel,
        name="sgu",
        grid=(m // tb,),
        in_specs=[pl.BlockSpec((tb, w), lambda i: (i, 0)),
                  pl.BlockSpec((tb, w), lambda i: (i, 1)),
                  pl.BlockSpec((tb, w), lambda i: (i, 2)),
                  pl.BlockSpec((1, w), lambda i: (0, 0)),
                  pl.BlockSpec((1, w), lambda i: (0, 0)),
                  pl.BlockSpec((SGU_GROUPS, SGU_CHUNK, SGU_CHUNK), lambda i: (0, 0, 0)),
                  pl.BlockSpec((SGU_CHUNK, SGU_GROUPS), lambda i: (0, 0))],
        out_specs=pl.BlockSpec((tb, w), lambda i: (i, 0)),
        out_shape=jax.ShapeDtypeStruct((m, w), BF16),
        compiler_params=_cparams(("parallel",)),
    )(pd, pd, pd, lng.reshape(1, w), lnb.reshape(1, w), ws, bs_t)


def _merge_kernel(ya_ref, yb_ref, yc_ref, yd_ref, ga_ref, gb_ref, gc_ref, gd_ref, wp_ref, o_ref):
    acc = None
    for g, (y_ref, g_ref) in enumerate(((ya_ref, ga_ref), (yb_ref, gb_ref), (yc_ref, gc_ref), (yd_ref, gd_ref))):
        term = jax.nn.sigmoid(g_ref[...]) * _dot(y_ref[...], wp_ref[g])
        acc = term if acc is None else acc + term
    o_ref[...] = acc.astype(o_ref.dtype)


def _merge(ya, yb, yc, yd, pm, wproj):
    m = ya.shape[0]
    w = BRANCH_W
    tm, tn = MERGE_TM, MERGE_TN
    nj = D_MODEL // tn
    ysp = pl.BlockSpec((tm, w), lambda j, i: (i, 0))
    gsp = [pl.BlockSpec((tm, tn), lambda j, i, g=g: (i, g * nj + j)) for g in range(N_BRANCH)]
    return pl.pallas_call(
        _merge_kernel,
        name="merge",
        grid=(nj, m // tm),
        in_specs=[ysp, ysp, ysp, ysp, *gsp,
                  pl.BlockSpec((N_BRANCH, w, tn), lambda j, i: (0, 0, j))],
        out_specs=pl.BlockSpec((tm, tn), lambda j, i: (i, j)),
        out_shape=jax.ShapeDtypeStruct((m, D_MODEL), BF16),
        compiler_params=_cparams(("parallel", "parallel")),
    )(ya, yb, yc, yd, pm, pm, pm, pm, wproj)


def _out_kernel(mg_ref, wo_ref, g_ref, x_ref, o_ref):
    y = _dot(mg_ref[...], wo_ref[...])
    y = y * lax.rsqrt(jnp.mean(y * y, axis=-1, keepdims=True) + NORM_EPS) * g_ref[...]
    o_ref[...] = x_ref[...] + y


def _out_proj(merged, wout, g, x2):
    m, d = x2.shape
    tm = OUT_TM
    return pl.pallas_call(
        _out_kernel,
        name="out_proj",
        grid=(m // tm,),
        in_specs=[pl.BlockSpec((tm, d), lambda i: (i, 0)),
                  pl.BlockSpec((d, d), lambda i: (0, 0)),
                  pl.BlockSpec((1, d), lambda i: (0, 0)),
                  pl.BlockSpec((tm, d), lambda i: (i, 0))],
        out_specs=pl.BlockSpec((tm, d), lambda i: (i, 0)),
        out_shape=jax.ShapeDtypeStruct((m, d), F32),
        compiler_params=_cparams(("parallel",)),
    )(merged, wout, g.reshape(1, d), x2)


def _split_cols(w, sizes):
    offs = np.cumsum((0,) + tuple(sizes))
    return [w[:, int(offs[i]):int(offs[i + 1])] for i in range(len(sizes))]


def _pad_cols(w, n):
    return jnp.pad(w, ((0, 0), (0, n - w.shape[1])))


def _pad_rows(w, n):
    return jnp.pad(w, ((0, n - w.shape[0]), (0, 0)))


def _rope_tables(positions, rot_dim, width):
    inv = ROPE_THETA ** (-jnp.arange(0, rot_dim, 2, dtype=F32) / rot_dim)
    ang = positions.astype(F32).reshape(-1, 1) * inv
    cos, sin = jnp.cos(ang), jnp.sin(ang)
    m = cos.shape[0]
    rest = width - rot_dim
    cos_h = jnp.concatenate([cos, cos, jnp.ones((m, rest), F32)], axis=1)
    sin_h = jnp.concatenate([-sin, sin, jnp.zeros((m, rest), F32)], axis=1)
    reps = LANES // width
    return jnp.tile(cos_h, (1, reps)), jnp.tile(sin_h, (1, reps))


def _mm_tn(n):
    for tn in (1024, 768, 640, 512, 256, 128):
        if n % tn == 0:
            return tn
    raise ValueError(n)


def kernel(x, positions, norm_pre, norm_post, w_in, gla_w_g2, gla_b_g, gla_head_g, rwkv_mu, rwkv_w0,
           rwkv_w_w2, rwkv_a0, rwkv_w_a2, rwkv_k_k, rwkv_k_a, rwkv_r_k, rwkv_lnx_g, rwkv_lnx_b,
           sgu_ln_g, sgu_ln_b, sgu_w_s, sgu_b_s, w_proj, w_out):
    bsz, seq, d = x.shape
    depth = w_in.shape[0]
    m = bsz * seq
    x2 = x.reshape(m, d)
    ca, sa = _rope_tables(positions, DSA_ROT, DSA_HD)
    ci, si = _rope_tables(positions, IDX_ROT, IDX_HD)
    lane_grp = np.arange(LANES) // RWKV_HD
    bd = jnp.asarray((lane_grp[:, None] == lane_grp[None, :]).astype(np.float32))

    for l in range(depth):
        wa, wb, wc, wd, wm = _split_cols(w_in[l], GROUP_COLS)
        a_q, a_k, a_v, a_glr, a_gate = _split_cols(wa, A_COLS)
        b_q, b_k, b_v, b_iq, b_ik, b_iw, b_gate = _split_cols(wb, B_COLS)
        c_r, c_k, c_v, c_wlr, c_alr, c_gate = _split_cols(wc, C_COLS)
        w_a = jnp.concatenate([a_q, a_k, a_v, a_gate, _pad_cols(a_glr, LANES)], axis=1).astype(BF16)
        w_b = jnp.concatenate([b_q, b_k, b_v, b_gate], axis=1).astype(BF16)
        w_bi = jnp.concatenate([b_iq, _pad_cols(b_ik, LANES), _pad_cols(b_iw, LANES)], axis=1).astype(BF16)
        w_c = jnp.concatenate([c_r, c_k, c_v, c_gate], axis=1).astype(BF16)
        w_cs = jnp.concatenate([_pad_cols(c_wlr, LANES), _pad_cols(c_alr, LANES)], axis=1).astype(BF16)
        w_d = wd.astype(BF16)
        w_m = wm.astype(BF16)

        h = _rmsnorm_cast(x2, norm_pre[l])
        pa, pb, pbi, pc, pcs, pd, pm = (_matmul(h, w, _mm_tn(w.shape[1]))
                                        for w in (w_a, w_b, w_bi, w_c, w_cs, w_d, w_m))

        ya = _gla(pa, _pad_rows(gla_w_g2[l], LANES), gla_b_g[l], gla_head_g[l], bsz, seq)

        qr, kr, vb, iqr, ikr, iws = _dsa_prep(pb, pbi, ca, sa, ci, si)
        yb = _dsa(qr, kr, vb, iqr, ikr, iws, pb, bsz, seq)

        mu_r, mu_k, mu_v, mu_wlr, mu_alr, mu_gate = _split_cols(rwkv_mu[l][None, :], C_COLS)
        mu_main = jnp.concatenate([mu_r, mu_k, mu_v, mu_gate], axis=1)
        mu_s = jnp.concatenate([_pad_cols(mu_wlr, LANES), _pad_cols(mu_alr, LANES)], axis=1)
        r, wl, k2, v, kk, a, sg, bon = _rwkv_prep(
            pc, pcs, mu_main, mu_s, rwkv_w0[l], _pad_rows(rwkv_w_w2[l], LANES), rwkv_a0[l],
            _pad_rows(rwkv_w_a2[l], LANES), rwkv_k_k[l], rwkv_k_a[l], rwkv_r_k[l].reshape(-1), bd, seq)
        y = _rwkv_scan(r, wl, k2, v, kk, a, bsz, seq)
        yc = _rwkv_post(y, bon, sg, rwkv_lnx_g[l], rwkv_lnx_b[l], bd)

        yd = _sgu(pd, sgu_ln_g[l], sgu_ln_b[l], sgu_w_s[l], sgu_b_s[l].T)

        merged = _merge(ya, yb, yc, yd, pm, w_proj[l].astype(BF16))
        x2 = _out_proj(merged, w_out[l].astype(BF16), norm_post[l], x2)
    return x2.reshape(bsz, seq, d)
```

```python
import functools

import numpy as np
import jax
import jax.numpy as jnp
from jax import lax
from jax.experimental import pallas as pl
from jax.experimental.pallas import tpu as pltpu

F32 = jnp.float32
BF16 = jnp.bfloat16
I32 = jnp.int32
HIGHEST = lax.Precision.HIGHEST

D_MODEL = 2048
CHUNK = 64
NORM_EPS = 1e-6
ROPE_THETA = 500000.0
N_BRANCH = 4
BRANCH_W = D_MODEL // 2

GLA_HEADS = 4
GLA_DK = BRANCH_W // 2 // GLA_HEADS
GLA_DV = BRANCH_W // GLA_HEADS
GLA_RANK = 16
GLA_TAU = 16.0

DSA_HEADS = 8
DSA_HD = BRANCH_W // DSA_HEADS
DSA_ROT = DSA_HD // 4
IDX_HEADS = 8
IDX_HD = 64
IDX_ROT = IDX_HD // 4
TOPK_MAX = 256

RWKV_HD = 64
RWKV_HEADS = BRANCH_W // RWKV_HD
RWKV_DECAY_RANK = 96
RWKV_A_RANK = 96
RWKV_DECAY_SCALE = 0.606531
RWKV_GN_EPS = 64e-5
RWKV_KK_EPS = 1e-12

SGU_CHUNK = 128
SGU_GROUPS = 8
SGU_GW = BRANCH_W // SGU_GROUPS
SGU_LN_EPS = 1e-5

A_COLS = (GLA_HEADS * GLA_DK, GLA_HEADS * GLA_DK, BRANCH_W, GLA_RANK, BRANCH_W)
B_COLS = (BRANCH_W, BRANCH_W, BRANCH_W, IDX_HEADS * IDX_HD, IDX_HD, IDX_HEADS, BRANCH_W)
C_COLS = (BRANCH_W, BRANCH_W, BRANCH_W, RWKV_DECAY_RANK, RWKV_A_RANK, BRANCH_W)
D_COLS = (2 * BRANCH_W, BRANCH_W)
GROUP_COLS = (sum(A_COLS), sum(B_COLS), sum(C_COLS), sum(D_COLS), N_BRANCH * D_MODEL)

LANES = 128
VMEM_LIMIT = 56 * 1024 * 1024
MXU_DEPTH = 256
IDX_K = MXU_DEPTH
NEG_BIG = -1e30
LOG2E = 1.4426950408889634
INT_MIN = -(2 ** 31)

NORM_TM = 512
MM_TM = 1024
GLA_TB = 512
DSA_PREP_TB = 512
DSA_TQ = 256
DSA_TA = 128
DSA_KB = 512
RWKV_PREP_TB = 256
RWKV_L = 64
RWKV_UNROLL = 8
RWKV_POST_TB = 512
SGU_TB = 512
MERGE_TM = 512
MERGE_TN = 1024
OUT_TM = 512


def _cparams(sem):
    return pltpu.CompilerParams(dimension_semantics=sem, vmem_limit_bytes=VMEM_LIMIT)


def _dot(a, b, precision=None):
    return jnp.dot(a, b, preferred_element_type=F32, precision=precision)


def _dot_nt(a, b, precision=None):
    return lax.dot_general(a, b, (((1,), (1,)), ((), ())), preferred_element_type=F32,
                           precision=precision)


def _dot_tn(a, b, precision=None):
    return lax.dot_general(a, b, (((0,), (0,)), ((), ())), preferred_element_type=F32,
                           precision=precision)


def _silu(x):
    return x * jax.nn.sigmoid(x)


def _rmsnorm_cast_kernel(x_ref, g_ref, o_ref):
    x = x_ref[...]
    ms = jnp.mean(x * x, axis=-1, keepdims=True)
    o_ref[...] = (x * lax.rsqrt(ms + NORM_EPS) * g_ref[...]).astype(o_ref.dtype)


def _rmsnorm_cast(x2, g):
    m, d = x2.shape
    return pl.pallas_call(
        _rmsnorm_cast_kernel,
        name="prenorm",
        grid=(m // NORM_TM,),
        in_specs=[pl.BlockSpec((NORM_TM, d), lambda i: (i, 0)),
                  pl.BlockSpec((1, d), lambda i: (0, 0))],
        out_specs=pl.BlockSpec((NORM_TM, d), lambda i: (i, 0)),
        out_shape=jax.ShapeDtypeStruct((m, d), BF16),
        compiler_params=_cparams(("parallel",)),
    )(x2, g.reshape(1, d))


def _matmul_kernel(a_ref, b_ref, o_ref):
    o_ref[...] = _dot(a_ref[...], b_ref[...]).astype(o_ref.dtype)


def _matmul(a, b, tn, out_dtype=F32):
    m, k = a.shape
    n = b.shape[1]
    tm = min(MM_TM, m)
    return pl.pallas_call(
        _matmul_kernel,
        name="proj",
        grid=(m // tm, n // tn),
        in_specs=[pl.BlockSpec((tm, k), lambda i, j: (i, 0)),
                  pl.BlockSpec((k, tn), lambda i, j: (0, j))],
        out_specs=pl.BlockSpec((tm, tn), lambda i, j: (i, j)),
        out_shape=jax.ShapeDtypeStruct((m, n), out_dtype),
        compiler_params=_cparams(("parallel", "parallel")),
    )(a, b)


def _gla_kernel(q_ref, k_ref, v_ref, gate_ref, glr_ref, wg2_ref, bg_ref, hg_ref, o_ref, state_ref):
    @pl.when(pl.program_id(2) == 0)
    def _():
        state_ref[...] = jnp.zeros_like(state_ref)

    ii = lax.broadcasted_iota(I32, (CHUNK, CHUNK), 0)
    jj = lax.broadcasted_iota(I32, (CHUNK, CHUNK), 1)
    tril = (ii >= jj).astype(F32)
    chunks = [pl.ds(c * CHUNK, CHUNK) for c in range(GLA_TB // CHUNK)]
    z = _dot(glr_ref[...], wg2_ref[...], HIGHEST) + bg_ref[...]
    log_a = (jnp.minimum(z, 0.0) - jnp.log1p(jnp.exp(-jnp.abs(z)))) / GLA_TAU
    cum = [_dot(tril, log_a[c * CHUNK:(c + 1) * CHUNK], HIGHEST) for c in range(len(chunks))]
    total = [x[CHUNK - 1:CHUNK, :] for x in cum]
    k_dec = [k_ref[sl, :] * jnp.exp(t - x) for sl, t, x in zip(chunks, total, cum)]
    kv = [_dot_tn(v_ref[sl, :].astype(BF16), kd.astype(BF16)) for sl, kd in zip(chunks, k_dec)]
    states = []
    st = state_ref[...]
    for t, x in zip(total, kv):
        st = st * jnp.exp(t) + x
        states.append(st)
    state_ref[...] = st
    outs = [_dot_nt((q_ref[sl, :] * (GLA_DK ** -0.5)).astype(BF16), s.astype(BF16))
            for sl, s in zip(chunks, states)]
    for sl, o in zip(chunks, outs):
        o = o * lax.rsqrt(jnp.mean(o * o, axis=-1, keepdims=True) + NORM_EPS) * hg_ref[...]
        o_ref[sl, :] = (o * _silu(gate_ref[sl, :])).astype(o_ref.dtype)


def _gla(pa, wg2p, bg, hg, bsz, seq):
    m = pa.shape[0]
    nt = seq // GLA_TB
    row = lambda b, h, t: b * nt + t
    kb = GLA_HEADS
    vb = (2 * GLA_HEADS * GLA_DK) // GLA_DV
    gb = vb + GLA_HEADS
    lb = (2 * GLA_HEADS * GLA_DK + 2 * BRANCH_W) // LANES
    return pl.pallas_call(
        _gla_kernel,
        name="gla",
        grid=(bsz, GLA_HEADS, nt),
        in_specs=[
            pl.BlockSpec((GLA_TB, GLA_DK), lambda b, h, t: (row(b, h, t), h)),
            pl.BlockSpec((GLA_TB, GLA_DK), lambda b, h, t: (row(b, h, t), kb + h)),
            pl.BlockSpec((GLA_TB, GLA_DV), lambda b, h, t: (row(b, h, t), vb + h)),
            pl.BlockSpec((GLA_TB, GLA_DV), lambda b, h, t: (row(b, h, t), gb + h)),
            pl.BlockSpec((GLA_TB, LANES), lambda b, h, t: (row(b, h, t), lb)),
            pl.BlockSpec((LANES, GLA_DK), lambda b, h, t: (0, h)),
            pl.BlockSpec((1, GLA_DK), lambda b, h, t: (0, h)),
            pl.BlockSpec((1, GLA_DV), lambda b, h, t: (0, 0)),
        ],
        out_specs=pl.BlockSpec((GLA_TB, GLA_DV), lambda b, h, t: (row(b, h, t), h)),
        out_shape=jax.ShapeDtypeStruct((m, BRANCH_W), BF16),
        scratch_shapes=[pltpu.VMEM((GLA_DV, GLA_DK), F32)],
        compiler_params=_cparams(("parallel", "parallel", "arbitrary")),
    )(pa, pa, pa, pa, pa, wg2p, bg.reshape(1, -1), hg.reshape(1, -1))


def _rope_tile(x, cos_t, sin_t, half, width):
    lane = lax.broadcasted_iota(I32, x.shape, 1) % width
    partner = jnp.where(lane < half, pltpu.roll(x, LANES - half, 1), pltpu.roll(x, half, 1))
    return x * cos_t + partner * sin_t


def _dsa_prep_kernel(q_ref, k_ref, v_ref, idx_ref, ca_ref, sa_ref, ci_ref, si_ref,
                     qo_ref, ko_ref, vo_ref, iqo_ref, iko_ref, iwo_ref):
    ca, sa = ca_ref[...], sa_ref[...]
    ci, si = ci_ref[...], si_ref[...]
    for h in range(DSA_HEADS):
        sl = slice(h * DSA_HD, (h + 1) * DSA_HD)
        q = _rope_tile(q_ref[:, sl], ca, sa, DSA_ROT // 2, DSA_HD) * (DSA_HD ** -0.5 * LOG2E)
        qo_ref[:, sl] = q.astype(qo_ref.dtype)
        ko_ref[:, sl] = _rope_tile(k_ref[:, sl], ca, sa, DSA_ROT // 2, DSA_HD).astype(ko_ref.dtype)
    vo_ref[...] = v_ref[...].astype(vo_ref.dtype)
    niq = IDX_HEADS * IDX_HD
    low = lax.broadcasted_iota(I32, (q_ref.shape[0], LANES), 1) < IDX_HD

    def hi_lo(x):
        hi = x.astype(BF16).astype(F32)
        return hi, x - hi

    for j in range(niq // LANES):
        x = _rope_tile(idx_ref[:, j * LANES:(j + 1) * LANES], ci, si, IDX_ROT // 2, IDX_HD)
        hi, lo = hi_lo(x)
        hi_sw = pltpu.roll(hi, IDX_HD, 1)
        lo_sw = pltpu.roll(lo, IDX_HD, 1)
        base = 2 * j * IDX_K
        iqo_ref[:, base:base + LANES] = jnp.where(low, hi, hi_sw).astype(BF16)
        iqo_ref[:, base + LANES:base + IDX_K] = jnp.where(low, lo, 0.0).astype(BF16)
        iqo_ref[:, base + IDX_K:base + IDX_K + LANES] = jnp.where(low, hi_sw, hi).astype(BF16)
        iqo_ref[:, base + IDX_K + LANES:base + 2 * IDX_K] = jnp.where(low, lo_sw, 0.0).astype(BF16)
    hi, lo = hi_lo(_rope_tile(idx_ref[:, niq:niq + LANES], ci, si, IDX_ROT // 2, IDX_HD))
    iko_ref[:, :LANES] = jnp.where(low, hi, pltpu.roll(lo, IDX_HD, 1)).astype(BF16)
    iko_ref[:, LANES:] = jnp.where(low, hi, 0.0).astype(BF16)
    iw = idx_ref[:, niq + LANES:niq + 2 * LANES] * (niq ** -0.5)
    for h in range(IDX_HEADS):
        iwo_ref[:, h * LANES:(h + 1) * LANES] = jnp.broadcast_to(iw[:, h:h + 1], (iw.shape[0], LANES))


def _dsa_prep(pb, pbi, ca, sa, ci, si):
    m = pb.shape[0]
    tb = DSA_PREP_TB
    w = BRANCH_W
    niq = IDX_HEADS * IDX_HD
    rows = lambda i: (i, 0)
    return pl.pallas_call(
        _dsa_prep_kernel,
        name="dsa_prep",
        grid=(m // tb,),
        in_specs=[pl.BlockSpec((tb, w), lambda i: (i, 0)),
                  pl.BlockSpec((tb, w), lambda i: (i, 1)),
                  pl.BlockSpec((tb, w), lambda i: (i, 2)),
                  pl.BlockSpec((tb, niq + 2 * LANES), rows),
                  pl.BlockSpec((tb, LANES), rows), pl.BlockSpec((tb, LANES), rows),
                  pl.BlockSpec((tb, LANES), rows), pl.BlockSpec((tb, LANES), rows)],
        out_specs=[pl.BlockSpec((tb, w), rows), pl.BlockSpec((tb, w), rows), pl.BlockSpec((tb, w), rows),
                   pl.BlockSpec((tb, IDX_HEADS * IDX_K), rows), pl.BlockSpec((tb, IDX_K), rows),
                   pl.BlockSpec((tb, IDX_HEADS * LANES), rows)],
        out_shape=[jax.ShapeDtypeStruct((m, w), BF16), jax.ShapeDtypeStruct((m, w), BF16),
                   jax.ShapeDtypeStruct((m, w), BF16), jax.ShapeDtypeStruct((m, IDX_HEADS * IDX_K), BF16),
                   jax.ShapeDtypeStruct((m, IDX_K), BF16), jax.ShapeDtypeStruct((m, IDX_HEADS * LANES), F32)],
        compiler_params=_cparams(("parallel",)),
    )(pb, pb, pb, pbi, ca, sa, ci, si)


def _dsa_kernel(q_ref, k_ref, v_ref, iq_ref, ik_ref, iw_ref, gate_ref, o_ref, key_scr, tie_scr, *, topk):
    tq, kb_w = DSA_TQ, DSA_KB
    q0 = pl.program_id(1) * tq
    nkb = (q0 + tq + kb_w - 1) // kb_w
    row = lax.broadcasted_iota(I32, (tq, 1), 0)
    qlim = ((q0 + row) // CHUNK + 1) * CHUNK
    lane_k = lax.broadcasted_iota(I32, (1, kb_w), 1)

    def score_body(kb, carry):
        ik = ik_ref[pl.ds(pl.multiple_of(kb * kb_w, kb_w), kb_w), :]
        dots = [_dot_nt(iq_ref[:, h * IDX_K:(h + 1) * IDX_K], ik) for h in range(IDX_HEADS)]
        acc = jnp.zeros((tq, kb_w), F32)
        for h, d in enumerate(dots):
            w_h = jnp.tile(iw_ref[:, h * LANES:(h + 1) * LANES], (1, kb_w // LANES))
            acc = acc + w_h * jnp.maximum(d, 0.0)
        acc = acc + 0.0
        bits = lax.bitcast_convert_type(acc, I32)
        key = bits ^ ((bits >> 31) & 0x7FFFFFFF)
        sidx = kb * kb_w + lane_k
        key_scr[kb] = jnp.where(sidx < qlim, key, INT_MIN)
        return carry

    lax.fori_loop(0, nkb, score_body, 0)

    def count(pred):
        def body(kb, c):
            m = pred(key_scr[kb], kb * kb_w + lane_k).astype(I32)
            for j in range(kb_w // LANES):
                c = c + m[:, j * LANES:(j + 1) * LANES]
            return c
        c = lax.fori_loop(0, nkb, body, jnp.zeros((tq, LANES), I32))
        return jnp.sum(c.astype(F32), axis=1, keepdims=True).astype(I32)

    def bit_body(it, carry):
        res, cnt_res = carry
        trial = res | lax.shift_left(jnp.int32(1), 31 - it)
        cand = trial ^ INT_MIN
        cnt = count(lambda kt, s: kt >= cand)
        ok = cnt >= topk
        return jnp.where(ok, trial, res), jnp.where(ok, cnt, cnt_res)

    res, cnt_ge = lax.fori_loop(0, 32, bit_body,
                                (jnp.zeros((tq, 1), I32), jnp.full((tq, 1), nkb * kb_w, I32)))
    thr = res ^ INT_MIN
    seq_len = key_scr.shape[0] * kb_w
    tie_scr[...] = jnp.full(tie_scr.shape, seq_len, I32)
    has_tie = (cnt_ge > topk) & (thr != INT_MIN)

    @pl.when(jnp.max(jnp.where(has_tie, 1.0, 0.0)) > 0.5)
    def _():
        need = topk - count(lambda kt, s: kt > thr)
        nbits = max(1, int(np.ceil(np.log2(seq_len))))

        def tie_body(it, resj):
            trial = resj | lax.shift_left(jnp.int32(1), nbits - 1 - it)
            c = count(lambda kt, s: (kt == thr) & (s < trial))
            return jnp.where(c < need, trial, resj)

        tie_scr[...] = lax.fori_loop(0, nbits, tie_body, jnp.zeros((tq, 1), I32))

    last_tie = tie_scr[...]

    def bias_body(kb, carry):
        kt = key_scr[kb]
        sidx = kb * kb_w + lane_k
        sel = ((kt > thr) | ((kt == thr) & (sidx <= last_tie))) & (kt != INT_MIN)
        key_scr[kb] = lax.bitcast_convert_type(jnp.where(sel, 0.0, NEG_BIG), I32)
        return carry

    lax.fori_loop(0, nkb, bias_body, 0)

    heads = [slice(h * DSA_HD, (h + 1) * DSA_HD) for h in range(DSA_HEADS)]
    ta = DSA_TA
    for part in range(tq // ta):
        qrows = slice(part * ta, (part + 1) * ta)

        def att_body(kb, carry, qrows=qrows):
            rows = pl.ds(pl.multiple_of(kb * kb_w, kb_w), kb_w)
            bias = lax.bitcast_convert_type(key_scr[kb, qrows, :], F32)
            s = [_dot_nt(q_ref[qrows, sl], k_ref[rows, sl]) + bias for sl in heads]
            m_n = [jnp.maximum(c[0], jnp.max(x, axis=1, keepdims=True)) for c, x in zip(carry, s)]
            p = [jnp.exp2(x - m) for x, m in zip(s, m_n)]
            pv = [_dot(x.astype(BF16), v_ref[rows, sl]) for x, sl in zip(p, heads)]
            alpha = [jnp.exp2(c[0] - m) for c, m in zip(carry, m_n)]
            l_n = [a * c[1] + jnp.sum(x, axis=1, keepdims=True) for a, c, x in zip(alpha, carry, p)]
            return tuple((m, l, a * c[2] + y) for m, l, a, c, y in zip(m_n, l_n, alpha, carry, pv))

        init = tuple((jnp.full((ta, 1), NEG_BIG, F32), jnp.zeros((ta, 1), F32), jnp.zeros((ta, DSA_HD), F32))
                     for _ in heads)
        nkb_part = (q0 + (part + 1) * ta + kb_w - 1) // kb_w
        final = lax.fori_loop(0, nkb_part, att_body, init)
        for sl, (_, l_f, acc_f) in zip(heads, final):
            o_ref[qrows, sl] = (acc_f / l_f * _silu(gate_ref[qrows, sl])).astype(o_ref.dtype)


def _dsa(qr, kr, vb, iqr, ikr, iws, pb, bsz, seq):
    m = qr.shape[0]
    w = BRANCH_W
    nq = seq // DSA_TQ
    nkb = seq // DSA_KB
    topk = min(TOPK_MAX, seq // 4)
    qrow = lambda b, i: (b * nq + i, 0)
    return pl.pallas_call(
        functools.partial(_dsa_kernel, topk=topk),
        name="dsa",
        grid=(bsz, nq),
        in_specs=[pl.BlockSpec((DSA_TQ, w), qrow),
                  pl.BlockSpec((seq, w), lambda b, i: (b, 0)),
                  pl.BlockSpec((seq, w), lambda b, i: (b, 0)),
                  pl.BlockSpec((DSA_TQ, IDX_HEADS * IDX_K), qrow),
                  pl.BlockSpec((seq, IDX_K), lambda b, i: (b, 0)),
                  pl.BlockSpec((DSA_TQ, IDX_HEADS * LANES), qrow),
                  pl.BlockSpec((DSA_TQ, w), lambda b, i: (b * nq + i, 3))],
        out_specs=pl.BlockSpec((DSA_TQ, w), qrow),
        out_shape=jax.ShapeDtypeStruct((m, w), BF16),
        scratch_shapes=[pltpu.VMEM((nkb, DSA_TQ, DSA_KB), I32),
                        pltpu.VMEM((DSA_TQ, 1), I32)],
        compiler_params=_cparams(("parallel", "arbitrary")),
    )(qr, kr, vb, iqr, ikr, iws, pb)


def _head_sum(x, bd):
    parts = [_dot(x[:, j * LANES:(j + 1) * LANES], bd, HIGHEST) for j in range(x.shape[1] // LANES)]
    return jnp.concatenate(parts, axis=1)


def _rwkv_prep_kernel(c_ref, cs_ref, pc_ref, pcs_ref, mu_ref, mus_ref, w0_ref, ww2_ref, a0_ref, wa2_ref,
                      kkw_ref, ka_ref, rk_ref, bd_ref,
                      r_o, wl_o, k_o, v_o, kk_o, a_o, sg_o, bon_o, *, blocks_per_seq):
    tb = c_ref.shape[0]
    first = (pl.program_id(0) % blocks_per_seq) == 0
    rowi = lax.broadcasted_iota(I32, (tb, 1), 0)

    def shift_lerp(c, prev8, mu):
        prev = jnp.where(first, 0.0, prev8[7:8, :])
        sh = jnp.where(rowi == 0, prev, pltpu.roll(c, 1, 0))
        return c + (sh - c) * mu

    w = BRANCH_W
    c = shift_lerp(c_ref[...], pc_ref[...], mu_ref[...])
    cs = shift_lerp(cs_ref[...], pcs_ref[...], mus_ref[...])
    r, k, v, gate = c[:, :w], c[:, w:2 * w], c[:, 2 * w:3 * w], c[:, 3 * w:]
    w_log = -RWKV_DECAY_SCALE * jax.nn.sigmoid(w0_ref[...] + _dot(jnp.tanh(cs[:, :LANES]), ww2_ref[...], HIGHEST))
    a = jax.nn.sigmoid(a0_ref[...] + _dot(cs[:, LANES:], wa2_ref[...], HIGHEST))
    kk = k * kkw_ref[...]
    k2 = k * (1.0 + (a - 1.0) * ka_ref[...])
    bd = bd_ref[...]
    kk = kk * lax.rsqrt(_head_sum(kk * kk, bd) + RWKV_KK_EPS)
    r_o[...] = r
    wl_o[...] = w_log
    k_o[...] = k2
    v_o[...] = v
    kk_o[...] = kk
    a_o[...] = a
    sg_o[...] = _silu(gate)
    bon_o[...] = _head_sum(r * k2 * rk_ref[...], bd) * v


def _rwkv_prep(pc, pcs, mu_main, mu_s, w0, ww2p, a0, wa2p, kkw, ka, rk, bd, seq):
    m = pc.shape[0]
    tb = RWKV_PREP_TB
    w = BRANCH_W
    rows = lambda i: (i, 0)
    prev = lambda i: (jnp.maximum(i * (tb // 8) - 1, 0), 0)
    const = lambda i: (0, 0)
    vec = pl.BlockSpec((1, w), const)
    out = pl.BlockSpec((tb, w), rows)
    return pl.pallas_call(
        functools.partial(_rwkv_prep_kernel, blocks_per_seq=seq // tb),
        name="rwkv_prep",
        grid=(m // tb,),
        in_specs=[pl.BlockSpec((tb, 4 * w), rows), pl.BlockSpec((tb, 2 * LANES), rows),
                  pl.BlockSpec((8, 4 * w), prev), pl.BlockSpec((8, 2 * LANES), prev),
                  pl.BlockSpec((1, 4 * w), const), pl.BlockSpec((1, 2 * LANES), const),
                  vec, pl.BlockSpec((LANES, w), const), vec, pl.BlockSpec((LANES, w), const),
                  vec, vec, vec, pl.BlockSpec((LANES, LANES), const)],
        out_specs=[out] * 8,
        out_shape=[jax.ShapeDtypeStruct((m, w), F32)] * 8,
        compiler_params=_cparams(("parallel",)),
    )(pc, pcs, pc, pcs, mu_main, mu_s, w0.reshape(1, w), ww2p, a0.reshape(1, w), wa2p,
      kkw.reshape(1, w), ka.reshape(1, w), rk.reshape(1, w), bd)


def _rwkv_scan_kernel(r_ref, wl_ref, k_ref, v_ref, kk_ref, a_ref, y_ref, lin_scr, yq_scr, h_scr, *, unroll):
    n = RWKV_L
    two = 2 * n
    nc = r_ref.shape[0] // n
    i2 = lax.broadcasted_iota(I32, (two, two), 0)
    j2 = lax.broadcasted_iota(I32, (two, two), 1)
    strict = i2 > j2
    incl = i2 >= j2
    same16 = (i2 // 16) == (j2 // 16)
    same_head = (i2 // RWKV_HD) == (j2 // RWKV_HD)
    eye = (i2 == j2).astype(F32)
    ii = lax.broadcasted_iota(I32, (n, n), 0)
    jj = lax.broadcasted_iota(I32, (n, n), 1)
    tril = (ii >= jj).astype(F32)
    head0 = lax.broadcasted_iota(I32, (n, LANES), 1) < RWKV_HD

    def mm(a, b):
        return _dot(a.astype(BF16), b.astype(BF16))

    def mm_nt(a, b):
        return _dot_nt(a.astype(BF16), b.astype(BF16))

    def mm_tn(a, b):
        return _dot_tn(a.astype(BF16), b.astype(BF16))

    def stack(x):
        return jnp.concatenate([jnp.where(head0, x, 0.0), jnp.where(head0, 0.0, x)], axis=0)

    def fold(x):
        return x[:n] + x[n:]

    def each(f, *lists):
        return [f(*args) for args in zip(*lists)]

    def tri_inv(a_low):
        a_d = each(lambda a: jnp.where(same16, a, 0.0), a_low)
        a_o = each(lambda a, d: a - d, a_low, a_d)
        a2 = each(mm, a_d, a_d)
        tick()
        a4 = each(mm, a2, a2)
        t_lo = each(lambda d, s: mm(eye + d, eye + s), a_d, a2)
        tick()
        a8 = each(mm, a4, a4)
        tick()
        t_hi = each(lambda x, y: mm(eye + x, eye + y), a4, a8)
        tick()
        t_d = each(mm, t_lo, t_hi)
        tick()
        nn = each(mm, t_d, a_o)
        tick()
        n2 = each(mm, nn, nn)
        tick()
        t_o = each(lambda x, y: mm(eye + x, eye + y), nn, n2)
        tick()
        return each(mm, t_o, t_d)

    pending = []

    def tick():
        if pending:
            pending.pop(0)()

    def build_group(g):
        rows = [pl.ds(pl.multiple_of((g * unroll + u) * n, n), n) for u in range(unroll)]
        wl = [wl_ref[r, :] for r in rows]
        k = [k_ref[r, :] for r in rows]
        v = [v_ref[r, :] for r in rows]
        kk = [kk_ref[r, :] for r in rows]
        beta = each(lambda x, r: x * a_ref[r, :], kk, rows)
        cum = each(lambda x: _dot(tril, x, HIGHEST), wl)
        cum_end = each(lambda x: x[n - 1:n, :], cum)
        g_inv = each(lambda x: jnp.exp(-x), cum)
        g_end = each(lambda e, x: jnp.exp(e - x), cum_end, cum)
        a_st = each(lambda x, c, w: stack(-x * jnp.exp(c - w)), kk, cum, wl)
        r_st = each(lambda r, c: stack(r_ref[r, :] * jnp.exp(c)), rows, cum)
        b_st = each(lambda x, y: stack(x * y), beta, g_inv)
        k_st = each(lambda x, y: stack(x * y), k, g_inv)
        v_st = each(stack, v)
        ar = each(lambda x, y: jnp.concatenate([x, y], axis=0), a_st, r_st)
        gb = each(mm_nt, ar, b_st)
        gk = each(mm_nt, ar, k_st)
        a_ab = each(lambda x: jnp.where(strict, x[:two], 0.0), gb)
        a_rb = each(lambda x: jnp.where(incl, x[two:], 0.0), gb)
        a_ak = each(lambda x: jnp.where(strict, x[:two], 0.0), gk)
        a_rk = each(lambda x: jnp.where(incl, x[two:], 0.0), gk)
        av2 = each(lambda x, y, z: mm(jnp.concatenate([x, y], axis=0), z), a_ak, a_rk, v_st)
        t_inv = tri_inv(a_ab)
        pq = each(lambda t, x, y: mm(t, jnp.concatenate([x, y[:two]], axis=1)), t_inv, a_st, av2)
        yr = each(mm, a_rb, pq)
        kv = each(lambda x, y, z: mm_tn(x * y, z), k, g_end, v)
        pq_d = each(lambda x: jnp.concatenate([fold(x[:, :LANES]), fold(x[:, LANES:])], axis=1), pq)
        gh = each(lambda x, y, z: mm_tn(x * y, z), beta, g_end, pq_d)
        while pending:
            tick()
        for u in range(unroll):
            ry = fold(r_st[u] + yr[u][:, :LANES])
            g_bd = jnp.where(same_head, gh[u][:, :LANES], 0.0) + eye * jnp.exp(cum_end[u])
            lin_scr[u] = jnp.concatenate([ry, g_bd], axis=0)
            yq_scr[u] = fold(yr[u][:, LANES:] + av2[u][two:])
            h_scr[u] = jnp.where(same_head, gh[u][:, LANES:] + kv[u], 0.0)

    def sweep_steps(g, state):
        def step(u):
            rows = pl.ds(pl.multiple_of((g * unroll + u) * n, n), n)
            both = mm(lin_scr[u], state[0])
            y_ref[rows, :] = both[:n] + yq_scr[u]
            state[0] = both[n:] + h_scr[u]
        return [functools.partial(step, u) for u in range(unroll)]

    ngroups = nc // unroll
    build_group(0)

    def body(g, st):
        state = [st]
        pending.extend(sweep_steps(g - 1, state))
        build_group(g)
        return state[0]

    st = lax.fori_loop(1, ngroups, body, jnp.zeros((two, two), F32))
    state = [st]
    for step in sweep_steps(ngroups - 1, state):
        step()


def _rwkv_scan(r, wl, k2, v, kk, a, bsz, seq):
    m = r.shape[0]
    npair = BRANCH_W // LANES
    blk = pl.BlockSpec((seq, LANES), lambda b, p: (b, p))
    u = RWKV_UNROLL
    return pl.pallas_call(
        functools.partial(_rwkv_scan_kernel, unroll=u),
        name="rwkv_scan",
        grid=(bsz, npair),
        in_specs=[blk] * 6,
        out_specs=blk,
        out_shape=jax.ShapeDtypeStruct((m, BRANCH_W), F32),
        scratch_shapes=[pltpu.VMEM((u, 3 * RWKV_L, LANES), F32), pltpu.VMEM((u, RWKV_L, LANES), F32),
                        pltpu.VMEM((u, LANES, LANES), F32)],
        compiler_params=_cparams(("parallel", "parallel")),
    )(r, wl, k2, v, kk, a)


def _rwkv_post_kernel(y_ref, bon_ref, sg_ref, g_ref, b_ref, bd_ref, o_ref):
    y = y_ref[...]
    bd = bd_ref[...]
    mu = _head_sum(y, bd) * (1.0 / RWKV_HD)
    yc = y - mu
    var = _head_sum(yc * yc, bd) * (1.0 / RWKV_HD)
    yn = yc * lax.rsqrt(var + RWKV_GN_EPS) * g_ref[...] + b_ref[...]
    o_ref[...] = ((yn + bon_ref[...]) * sg_ref[...]).astype(o_ref.dtype)


def _rwkv_post(y, bon, sg, g, b, bd):
    m, w = y.shape
    tb = RWKV_POST_TB
    rows = pl.BlockSpec((tb, w), lambda i: (i, 0))
    vec = pl.BlockSpec((1, w), lambda i: (0, 0))
    return pl.pallas_call(
        _rwkv_post_kernel,
        name="rwkv_post",
        grid=(m // tb,),
        in_specs=[rows, rows, rows, vec, vec, pl.BlockSpec((LANES, LANES), lambda i: (0, 0))],
        out_specs=rows,
        out_shape=jax.ShapeDtypeStruct((m, w), BF16),
        compiler_params=_cparams(("parallel",)),
    )(y, bon, sg, g.reshape(1, w), b.reshape(1, w), bd)


def _sgu_kernel(u_ref, v_ref, gate_ref, lng_ref, lnb_ref, ws_ref, bs_ref, o_ref):
    inv_sqrt2 = 0.7071067811865476

    def gelu(z):
        return 0.5 * z * (1.0 + lax.erf(z * inv_sqrt2))

    v = gelu(v_ref[...])
    mu = jnp.mean(v, axis=-1, keepdims=True)
    vc = v - mu
    var = jnp.mean(vc * vc, axis=-1, keepdims=True)
    vn = vc * lax.rsqrt(var + SGU_LN_EPS) * lng_ref[...] + lnb_ref[...]
    t = SGU_CHUNK
    ii = lax.broadcasted_iota(I32, (t, t), 0) // CHUNK
    jj = lax.broadcasted_iota(I32, (t, t), 1) // CHUNK
    mask = jj <= ii
    for g in range(SGU_GROUPS):
        ws = jnp.where(mask, ws_ref[g], 0.0)
        sl = slice(g * SGU_GW, (g + 1) * SGU_GW)
        bias = bs_ref[:, g:g + 1]
        for c in range(u_ref.shape[0] // t):
            rows = slice(c * t, (c + 1) * t)
            sv = _dot(ws, vn[rows, sl]) + bias
            o_ref[rows, sl] = (gelu(u_ref[rows, sl]) * sv * _silu(gate_ref[rows, sl])).astype(o_ref.dtype)


def _sgu(pd, lng, lnb, ws, bs_t):
    m = pd.shape[0]
    w = BRANCH_W
    tb = SGU_TB
    return pl.pallas_call(
        _sgu_kernel,
        name="sgu",
        grid=(m // tb,),
        in_specs=[pl.BlockSpec((tb, w), lambda i: (i, 0)),
                  pl.BlockSpec((tb, w), lambda i: (i, 1)),
                  pl.BlockSpec((tb, w), lambda i: (i, 2)),
                  pl.BlockSpec((1, w), lambda i: (0, 0)),
                  pl.BlockSpec((1, w), lambda i: (0, 0)),
                  pl.BlockSpec((SGU_GROUPS, SGU_CHUNK, SGU_CHUNK), lambda i: (0, 0, 0)),
                  pl.BlockSpec((SGU_CHUNK, SGU_GROUPS), lambda i: (0, 0))],
        out_specs=pl.BlockSpec((tb, w), lambda i: (i, 0)),
        out_shape=jax.ShapeDtypeStruct((m, w), BF16),
        compiler_params=_cparams(("parallel",)),
    )(pd, pd, pd, lng.reshape(1, w), lnb.reshape(1, w), ws, bs_t)


def _merge_kernel(ya_ref, yb_ref, yc_ref, yd_ref, ga_ref, gb_ref, gc_ref, gd_ref, wp_ref, o_ref):
    acc = None
    for g, (y_ref, g_ref) in enumerate(((ya_ref, ga_ref), (yb_ref, gb_ref), (yc_ref, gc_ref), (yd_ref, gd_ref))):
        term = jax.nn.sigmoid(g_ref[...]) * _dot(y_ref[...], wp_ref[g])
        acc = term if acc is None else acc + term
    o_ref[...] = acc.astype(o_ref.dtype)


def _merge(ya, yb, yc, yd, pm, wproj):
    m = ya.shape[0]
    w = BRANCH_W
    tm, tn = MERGE_TM, MERGE_TN
    nj = D_MODEL // tn
    ysp = pl.BlockSpec((tm, w), lambda j, i: (i, 0))
    gsp = [pl.BlockSpec((tm, tn), lambda j, i, g=g: (i, g * nj + j)) for g in range(N_BRANCH)]
    return pl.pallas_call(
        _merge_kernel,
        name="merge",
        grid=(nj, m // tm),
        in_specs=[ysp, ysp, ysp, ysp, *gsp,
                  pl.BlockSpec((N_BRANCH, w, tn), lambda j, i: (0, 0, j))],
        out_specs=pl.BlockSpec((tm, tn), lambda j, i: (i, j)),
        out_shape=jax.ShapeDtypeStruct((m, D_MODEL), BF16),
        compiler_params=_cparams(("parallel", "parallel")),
    )(ya, yb, yc, yd, pm, pm, pm, pm, wproj)


def _out_kernel(mg_ref, wo_ref, g_ref, x_ref, o_ref):
    y = _dot(mg_ref[...], wo_ref[...])
    y = y * lax.rsqrt(jnp.mean(y * y, axis=-1, keepdims=True) + NORM_EPS) * g_ref[...]
    o_ref[...] = x_ref[...] + y


def _out_proj(merged, wout, g, x2):
    m, d = x2.shape
    tm = OUT_TM
    return pl.pallas_call(
        _out_kernel,
        name="out_proj",
        grid=(m // tm,),
        in_specs=[pl.BlockSpec((tm, d), lambda i: (i, 0)),
                  pl.BlockSpec((d, d), lambda i: (0, 0)),
                  pl.BlockSpec((1, d), lambda i: (0, 0)),
                  pl.BlockSpec((tm, d), lambda i: (i, 0))],
        out_specs=pl.BlockSpec((tm, d), lambda i: (i, 0)),
        out_shape=jax.ShapeDtypeStruct((m, d), F32),
        compiler_params=_cparams(("parallel",)),
    )(merged, wout, g.reshape(1, d), x2)


def _split_cols(w, sizes):
    offs = np.cumsum((0,) + tuple(sizes))
    return [w[:, int(offs[i]):int(offs[i + 1])] for i in range(len(sizes))]


def _pad_cols(w, n):
    return jnp.pad(w, ((0, 0), (0, n - w.shape[1])))


def _pad_rows(w, n):
    return jnp.pad(w, ((0, n - w.shape[0]), (0, 0)))


def _rope_tables(positions, rot_dim, width):
    inv = ROPE_THETA ** (-jnp.arange(0, rot_dim, 2, dtype=F32) / rot_dim)
    ang = positions.astype(F32).reshape(-1, 1) * inv
    cos, sin = jnp.cos(ang), jnp.sin(ang)
    m = cos.shape[0]
    rest = width - rot_dim
    cos_h = jnp.concatenate([cos, cos, jnp.ones((m, rest), F32)], axis=1)
    sin_h = jnp.concatenate([-sin, sin, jnp.zeros((m, rest), F32)], axis=1)
    reps = LANES // width
    return jnp.tile(cos_h, (1, reps)), jnp.tile(sin_h, (1, reps))


def _mm_tn(n):
    for tn in (1024, 768, 640, 512, 256, 128):
        if n % tn == 0:
            return tn
    raise ValueError(n)


def kernel(x, positions, norm_pre, norm_post, w_in, gla_w_g2, gla_b_g, gla_head_g, rwkv_mu, rwkv_w0,
           rwkv_w_w2, rwkv_a0, rwkv_w_a2, rwkv_k_k, rwkv_k_a, rwkv_r_k, rwkv_lnx_g, rwkv_lnx_b,
           sgu_ln_g, sgu_ln_b, sgu_w_s, sgu_b_s, w_proj, w_out):
    bsz, seq, d = x.shape
    depth = w_in.shape[0]
    m = bsz * seq
    x2 = x.reshape(m, d)
    ca, sa = _rope_tables(positions, DSA_ROT, DSA_HD)
    ci, si = _rope_tables(positions, IDX_ROT, IDX_HD)
    lane_grp = np.arange(LANES) // RWKV_HD
    bd = jnp.asarray((lane_grp[:, None] == lane_grp[None, :]).astype(np.float32))

    for l in range(depth):
        wa, wb, wc, wd, wm = _split_cols(w_in[l], GROUP_COLS)
        a_q, a_k, a_v, a_glr, a_gate = _split_cols(wa, A_COLS)
        b_q, b_k, b_v, b_iq, b_ik, b_iw, b_gate = _split_cols(wb, B_COLS)
        c_r, c_k, c_v, c_wlr, c_alr, c_gate = _split_cols(wc, C_COLS)
        w_a = jnp.concatenate([a_q, a_k, a_v, a_gate, _pad_cols(a_glr, LANES)], axis=1).astype(BF16)
        w_b = jnp.concatenate([b_q, b_k, b_v, b_gate], axis=1).astype(BF16)
        w_bi = jnp.concatenate([b_iq, _pad_cols(b_ik, LANES), _pad_cols(b_iw, LANES)], axis=1).astype(BF16)
        w_c = jnp.concatenate([c_r, c_k, c_v, c_gate], axis=1).astype(BF16)
        w_cs = jnp.concatenate([_pad_cols(c_wlr, LANES), _pad_cols(c_alr, LANES)], axis=1).astype(BF16)
        w_d = wd.astype(BF16)
        w_m = wm.astype(BF16)

        h = _rmsnorm_cast(x2, norm_pre[l])
        pa, pb, pbi, pc, pcs, pd, pm = (_matmul(h, w, _mm_tn(w.shape[1]))
                                        for w in (w_a, w_b, w_bi, w_c, w_cs, w_d, w_m))

        ya = _gla(pa, _pad_rows(gla_w_g2[l], LANES), gla_b_g[l], gla_head_g[l], bsz, seq)

        qr, kr, vb, iqr, ikr, iws = _dsa_prep(pb, pbi, ca, sa, ci, si)
        yb = _dsa(qr, kr, vb, iqr, ikr, iws, pb, bsz, seq)

        mu_r, mu_k, mu_v, mu_wlr, mu_alr, mu_gate = _split_cols(rwkv_mu[l][None, :], C_COLS)
        mu_main = jnp.concatenate([mu_r, mu_k, mu_v, mu_gate], axis=1)
        mu_s = jnp.concatenate([_pad_cols(mu_wlr, LANES), _pad_cols(mu_alr, LANES)], axis=1)
        r, wl, k2, v, kk, a, sg, bon = _rwkv_prep(
            pc, pcs, mu_main, mu_s, rwkv_w0[l], _pad_rows(rwkv_w_w2[l], LANES), rwkv_a0[l],
            _pad_rows(rwkv_w_a2[l], LANES), rwkv_k_k[l], rwkv_k_a[l], rwkv_r_k[l].reshape(-1), bd, seq)
        y = _rwkv_scan(r, wl, k2, v, kk, a, bsz, seq)
        yc = _rwkv_post(y, bon, sg, rwkv_lnx_g[l], rwkv_lnx_b[l], bd)

        yd = _sgu(pd, sgu_ln_g[l], sgu_ln_b[l], sgu_w_s[l], sgu_b_s[l].T)

        merged = _merge(ya, yb, yc, yd, pm, w_proj[l].astype(BF16))
        x2 = _out_proj(merged, w_out[l].astype(BF16), norm_post[l], x2)
    return x2.reshape(bsz, seq, d)
```

```python
import functools

import numpy as np
import jax
import jax.numpy as jnp
from jax import lax
from jax.experimental import pallas as pl
from jax.experimental.pallas import tpu as pltpu

F32 = jnp.float32
BF16 = jnp.bfloat16
I32 = jnp.int32
HIGHEST = lax.Precision.HIGHEST

D_MODEL = 2048
CHUNK = 64
NORM_EPS = 1e-6
ROPE_THETA = 500000.0
N_BRANCH = 4
BRANCH_W = D_MODEL // 2

GLA_HEADS = 4
GLA_DK = BRANCH_W // 2 // GLA_HEADS
GLA_DV = BRANCH_W // GLA_HEADS
GLA_RANK = 16
GLA_TAU = 16.0

DSA_HEADS = 8
DSA_HD = BRANCH_W // DSA_HEADS
DSA_ROT = DSA_HD // 4
IDX_HEADS = 8
IDX_HD = 64
IDX_ROT = IDX_HD // 4
TOPK_MAX = 256

RWKV_HD = 64
RWKV_HEADS = BRANCH_W // RWKV_HD
RWKV_DECAY_RANK = 96
RWKV_A_RANK = 96
RWKV_DECAY_SCALE = 0.606531
RWKV_GN_EPS = 64e-5
RWKV_KK_EPS = 1e-12

SGU_CHUNK = 128
SGU_GROUPS = 8
SGU_GW = BRANCH_W // SGU_GROUPS
SGU_LN_EPS = 1e-5

A_COLS = (GLA_HEADS * GLA_DK, GLA_HEADS * GLA_DK, BRANCH_W, GLA_RANK, BRANCH_W)
B_COLS = (BRANCH_W, BRANCH_W, BRANCH_W, IDX_HEADS * IDX_HD, IDX_HD, IDX_HEADS, BRANCH_W)
C_COLS = (BRANCH_W, BRANCH_W, BRANCH_W, RWKV_DECAY_RANK, RWKV_A_RANK, BRANCH_W)
D_COLS = (2 * BRANCH_W, BRANCH_W)
GROUP_COLS = (sum(A_COLS), sum(B_COLS), sum(C_COLS), sum(D_COLS), N_BRANCH * D_MODEL)

LANES = 128
VMEM_LIMIT = 56 * 1024 * 1024
MXU_DEPTH = 256
IDX_K = MXU_DEPTH
NEG_BIG = -1e30
LOG2E = 1.4426950408889634
INT_MIN = -(2 ** 31)

NORM_TM = 512
MM_TM = 1024
GLA_TB = 512
DSA_PREP_TB = 512
DSA_TQ = 256
DSA_TA = 128
DSA_KB = 512
RWKV_PREP_TB = 256
RWKV_L = 64
RWKV_UNROLL = 8
RWKV_POST_TB = 512
SGU_TB = 512
MERGE_TM = 512
MERGE_TN = 1024
OUT_TM = 512


def _cparams(sem):
    return pltpu.CompilerParams(dimension_semantics=sem, vmem_limit_bytes=VMEM_LIMIT)


def _dot(a, b, precision=None):
    return jnp.dot(a, b, preferred_element_type=F32, precision=precision)


def _dot_nt(a, b, precision=None):
    return lax.dot_general(a, b, (((1,), (1,)), ((), ())), preferred_element_type=F32,
                           precision=precision)


def _dot_tn(a, b, precision=None):
    return lax.dot_general(a, b, (((0,), (0,)), ((), ())), preferred_element_type=F32,
                           precision=precision)


def _silu(x):
    return x * jax.nn.sigmoid(x)


def _rmsnorm_cast_kernel(x_ref, g_ref, o_ref):
    x = x_ref[...]
    ms = jnp.mean(x * x, axis=-1, keepdims=True)
    o_ref[...] = (x * lax.rsqrt(ms + NORM_EPS) * g_ref[...]).astype(o_ref.dtype)


def _rmsnorm_cast(x2, g):
    m, d = x2.shape
    return pl.pallas_call(
        _rmsnorm_cast_kernel,
        name="prenorm",
        grid=(m // NORM_TM,),
        in_specs=[pl.BlockSpec((NORM_TM, d), lambda i: (i, 0)),
                  pl.BlockSpec((1, d), lambda i: (0, 0))],
        out_specs=pl.BlockSpec((NORM_TM, d), lambda i: (i, 0)),
        out_shape=jax.ShapeDtypeStruct((m, d), BF16),
        compiler_params=_cparams(("parallel",)),
    )(x2, g.reshape(1, d))


def _matmul_kernel(a_ref, b_ref, o_ref):
    o_ref[...] = _dot(a_ref[...], b_ref[...]).astype(o_ref.dtype)


def _matmul(a, b, tn, out_dtype=F32):
    m, k = a.shape
    n = b.shape[1]
    tm = min(MM_TM, m)
    return pl.pallas_call(
        _matmul_kernel,
        name="proj",
        grid=(m // tm, n // tn),
        in_specs=[pl.BlockSpec((tm, k), lambda i, j: (i, 0)),
                  pl.BlockSpec((k, tn), lambda i, j: (0, j))],
        out_specs=pl.BlockSpec((tm, tn), lambda i, j: (i, j)),
        out_shape=jax.ShapeDtypeStruct((m, n), out_dtype),
        compiler_params=_cparams(("parallel", "parallel")),
    )(a, b)


def _gla_kernel(q_ref, k_ref, v_ref, gate_ref, glr_ref, wg2_ref, bg_ref, hg_ref, o_ref, state_ref):
    @pl.when(pl.program_id(2) == 0)
    def _():
        state_ref[...] = jnp.zeros_like(state_ref)

    ii = lax.broadcasted_iota(I32, (CHUNK, CHUNK), 0)
    jj = lax.broadcasted_iota(I32, (CHUNK, CHUNK), 1)
    tril = (ii >= jj).astype(F32)
    chunks = [pl.ds(c * CHUNK, CHUNK) for c in range(GLA_TB // CHUNK)]
    z = _dot(glr_ref[...], wg2_ref[...], HIGHEST) + bg_ref[...]
    log_a = (jnp.minimum(z, 0.0) - jnp.log1p(jnp.exp(-jnp.abs(z)))) / GLA_TAU
    cum = [_dot(tril, log_a[c * CHUNK:(c + 1) * CHUNK], HIGHEST) for c in range(len(chunks))]
    total = [x[CHUNK - 1:CHUNK, :] for x in cum]
    k_dec = [k_ref[sl, :] * jnp.exp(t - x) for sl, t, x in zip(chunks, total, cum)]
    kv = [_dot_tn(v_ref[sl, :].astype(BF16), kd.astype(BF16)) for sl, kd in zip(chunks, k_dec)]
    states = []
    st = state_ref[...]
    for t, x in zip(total, kv):
        st = st * jnp.exp(t) + x
        states.append(st)
    state_ref[...] = st
    outs = [_dot_nt((q_ref[sl, :] * (GLA_DK ** -0.5)).astype(BF16), s.astype(BF16))
            for sl, s in zip(chunks, states)]
    for sl, o in zip(chunks, outs):
        o = o * lax.rsqrt(jnp.mean(o * o, axis=-1, keepdims=True) + NORM_EPS) * hg_ref[...]
        o_ref[sl, :] = (o * _silu(gate_ref[sl, :])).astype(o_ref.dtype)


def _gla(pa, wg2p, bg, hg, bsz, seq):
    m = pa.shape[0]
    nt = seq // GLA_TB
    row = lambda b, h, t: b * nt + t
    kb = GLA_HEADS
    vb = (2 * GLA_HEADS * GLA_DK) // GLA_DV
    gb = vb + GLA_HEADS
    lb = (2 * GLA_HEADS * GLA_DK + 2 * BRANCH_W) // LANES
    return pl.pallas_call(
        _gla_kernel,
        name="gla",
        grid=(bsz, GLA_HEADS, nt),
        in_specs=[
            pl.BlockSpec((GLA_TB, GLA_DK), lambda b, h, t: (row(b, h, t), h)),
            pl.BlockSpec((GLA_TB, GLA_DK), lambda b, h, t: (row(b, h, t), kb + h)),
            pl.BlockSpec((GLA_TB, GLA_DV), lambda b, h, t: (row(b, h, t), vb + h)),
            pl.BlockSpec((GLA_TB, GLA_DV), lambda b, h, t: (row(b, h, t), gb + h)),
            pl.BlockSpec((GLA_TB, LANES), lambda b, h, t: (row(b, h, t), lb)),
            pl.BlockSpec((LANES, GLA_DK), lambda b, h, t: (0, h)),
            pl.BlockSpec((1, GLA_DK), lambda b, h, t: (0, h)),
            pl.BlockSpec((1, GLA_DV), lambda b, h, t: (0, 0)),
        ],
        out_specs=pl.BlockSpec((GLA_TB, GLA_DV), lambda b, h, t: (row(b, h, t), h)),
        out_shape=jax.ShapeDtypeStruct((m, BRANCH_W), BF16),
        scratch_shapes=[pltpu.VMEM((GLA_DV, GLA_DK), F32)],
        compiler_params=_cparams(("parallel", "parallel", "arbitrary")),
    )(pa, pa, pa, pa, pa, wg2p, bg.reshape(1, -1), hg.reshape(1, -1))


def _rope_tile(x, cos_t, sin_t, half, width):
    lane = lax.broadcasted_iota(I32, x.shape, 1) % width
    partner = jnp.where(lane < half, pltpu.roll(x, LANES - half, 1), pltpu.roll(x, half, 1))
    return x * cos_t + partner * sin_t


def _dsa_prep_kernel(q_ref, k_ref, v_ref, idx_ref, ca_ref, sa_ref, ci_ref, si_ref,
                     qo_ref, ko_ref, vo_ref, iqo_ref, iko_ref, iwo_ref):
    ca, sa = ca_ref[...], sa_ref[...]
    ci, si = ci_ref[...], si_ref[...]
    for h in range(DSA_HEADS):
        sl = slice(h * DSA_HD, (h + 1) * DSA_HD)
        q = _rope_tile(q_ref[:, sl], ca, sa, DSA_ROT // 2, DSA_HD) * (DSA_HD ** -0.5 * LOG2E)
        qo_ref[:, sl] = q.astype(qo_ref.dtype)
        ko_ref[:, sl] = _rope_tile(k_ref[:, sl], ca, sa, DSA_ROT // 2, DSA_HD).astype(ko_ref.dtype)
        vo_ref[0, sl, :] = v_ref[:, sl].T.astype(vo_ref.dtype)
    niq = IDX_HEADS * IDX_HD
    low = lax.broadcasted_iota(I32, (q_ref.shape[0], LANES), 1) < IDX_HD

    def hi_lo(x):
        hi = x.astype(BF16).astype(F32)
        return hi, x - hi

    for j in range(niq // LANES):
        x = _rope_tile(idx_ref[:, j * LANES:(j + 1) * LANES], ci, si, IDX_ROT // 2, IDX_HD)
        hi, lo = hi_lo(x)
        hi_sw = pltpu.roll(hi, IDX_HD, 1)
        lo_sw = pltpu.roll(lo, IDX_HD, 1)
        base = 2 * j * IDX_K
        iqo_ref[:, base:base + LANES] = jnp.where(low, hi, hi_sw).astype(BF16)
        iqo_ref[:, base + LANES:base + IDX_K] = jnp.where(low, lo, 0.0).astype(BF16)
        iqo_ref[:, base + IDX_K:base + IDX_K + LANES] = jnp.where(low, hi_sw, hi).astype(BF16)
        iqo_ref[:, base + IDX_K + LANES:base + 2 * IDX_K] = jnp.where(low, lo_sw, 0.0).astype(BF16)
    hi, lo = hi_lo(_rope_tile(idx_ref[:, niq:niq + LANES], ci, si, IDX_ROT // 2, IDX_HD))
    iko_ref[:, :LANES] = jnp.where(low, hi, pltpu.roll(lo, IDX_HD, 1)).astype(BF16)
    iko_ref[:, LANES:] = jnp.where(low, hi, 0.0).astype(BF16)
    iwo_ref[...] = idx_ref[:, niq + LANES:niq + 2 * LANES] * (niq ** -0.5)


def _dsa_prep(pb, pbi, ca, sa, ci, si):
    m = pb.shape[0]
    tb = DSA_KB
    w = BRANCH_W
    niq = IDX_HEADS * IDX_HD
    rows = lambda i: (i, 0)
    return pl.pallas_call(
        _dsa_prep_kernel,
        name="dsa_prep",
        grid=(m // tb,),
        in_specs=[pl.BlockSpec((tb, w), lambda i: (i, 0)),
                  pl.BlockSpec((tb, w), lambda i: (i, 1)),
                  pl.BlockSpec((tb, w), lambda i: (i, 2)),
                  pl.BlockSpec((tb, niq + 2 * LANES), rows),
                  pl.BlockSpec((tb, LANES), rows), pl.BlockSpec((tb, LANES), rows),
                  pl.BlockSpec((tb, LANES), rows), pl.BlockSpec((tb, LANES), rows)],
        out_specs=[pl.BlockSpec((tb, w), rows), pl.BlockSpec((tb, w), rows),
                   pl.BlockSpec((1, w, tb), lambda i: (i, 0, 0)),
                   pl.BlockSpec((tb, IDX_HEADS * IDX_K), rows), pl.BlockSpec((tb, IDX_K), rows),
                   pl.BlockSpec((tb, LANES), rows)],
        out_shape=[jax.ShapeDtypeStruct((m, w), BF16), jax.ShapeDtypeStruct((m, w), BF16),
                   jax.ShapeDtypeStruct((m // tb, w, tb), BF16), jax.ShapeDtypeStruct((m, IDX_HEADS * IDX_K), BF16),
                   jax.ShapeDtypeStruct((m, IDX_K), BF16), jax.ShapeDtypeStruct((m, LANES), F32)],
        compiler_params=_cparams(("parallel",)),
    )(pb, pb, pb, pbi, ca, sa, ci, si)


def _dsa_kernel(q_ref, k_ref, vt_ref, iq_ref, ik_ref, iw_ref, gate_ref, o_ref, key_scr, tie_scr, *, topk):
    tq, kb_w = DSA_TQ, DSA_KB
    q0 = pl.program_id(1) * tq
    nkb = (q0 + tq + kb_w - 1) // kb_w
    qcol = lax.broadcasted_iota(I32, (1, tq), 1)
    qlim = ((q0 + qcol) // CHUNK + 1) * CHUNK
    sub_k = lax.broadcasted_iota(I32, (kb_w, 1), 0)
    sub = 8

    def score_body(kb, carry):
        ik = ik_ref[pl.ds(pl.multiple_of(kb * kb_w, kb_w), kb_w), :]
        dots = [_dot_nt(ik, iq_ref[:, h * IDX_K:(h + 1) * IDX_K]) for h in range(IDX_HEADS)]
        acc = jnp.zeros((kb_w, tq), F32)
        for h, d in enumerate(dots):
            acc = acc + iw_ref[h:h + 1, :] * jnp.maximum(d, 0.0)
        acc = acc + 0.0
        bits = lax.bitcast_convert_type(acc, I32)
        key = bits ^ ((bits >> 31) & 0x7FFFFFFF)
        sidx = kb * kb_w + sub_k
        key_scr[kb] = jnp.where(sidx < qlim, key, INT_MIN)
        return carry

    lax.fori_loop(0, nkb, score_body, 0)

    def count(pred):
        def body(kb, c):
            m = pred(key_scr[kb], kb * kb_w + sub_k).astype(I32)
            nparts = 8
            rows_part = kb_w // nparts
            parts = [jnp.sum(m[j * rows_part:(j + 1) * rows_part].reshape(rows_part // sub, sub, tq), axis=0)
                     for j in range(nparts)]
            while len(parts) > 1:
                parts = [a + b for a, b in zip(parts[::2], parts[1::2])]
            return c + parts[0]
        c = lax.fori_loop(0, nkb, body, jnp.zeros((sub, tq), I32))
        return jnp.sum(c, axis=0, keepdims=True)

    def bit_body(it, carry):
        res, cnt_res = carry
        trial = res | lax.shift_left(jnp.int32(1), 31 - it)
        cand = trial ^ INT_MIN
        cnt = count(lambda kt, s: kt >= cand)
        ok = cnt >= topk
        return jnp.where(ok, trial, res), jnp.where(ok, cnt, cnt_res)

    res, cnt_ge = lax.fori_loop(0, 32, bit_body,
                                (jnp.zeros((1, tq), I32), jnp.full((1, tq), nkb * kb_w, I32)))
    thr = res ^ INT_MIN
    seq_len = key_scr.shape[0] * kb_w
    tie_scr[...] = jnp.full(tie_scr.shape, seq_len, I32)
    has_tie = (cnt_ge > topk) & (thr != INT_MIN)

    @pl.when(jnp.max(jnp.where(has_tie, 1.0, 0.0)) > 0.5)
    def _():
        need = topk - count(lambda kt, s: kt > thr)
        nbits = max(1, int(np.ceil(np.log2(seq_len))))

        def tie_body(it, resj):
            trial = resj | lax.shift_left(jnp.int32(1), nbits - 1 - it)
            c = count(lambda kt, s: (kt == thr) & (s < trial))
            return jnp.where(c < need, trial, resj)

        tie_scr[...] = lax.fori_loop(0, nbits, tie_body, jnp.zeros((1, tq), I32))

    last_tie = tie_scr[...]

    def bias_body(kb, carry):
        kt = key_scr[kb]
        sidx = kb * kb_w + sub_k
        sel = ((kt > thr) | ((kt == thr) & (sidx <= last_tie))) & (kt != INT_MIN)
        key_scr[kb] = lax.bitcast_convert_type(jnp.where(sel, 0.0, NEG_BIG), I32)
        return carry

    lax.fori_loop(0, nkb, bias_body, 0)

    heads = [slice(h * DSA_HD, (h + 1) * DSA_HD) for h in range(DSA_HEADS)]

    def att_body(kb, carry):
        rows = pl.ds(pl.multiple_of(kb * kb_w, kb_w), kb_w)
        bias = lax.bitcast_convert_type(key_scr[kb], F32)
        s = [_dot_nt(k_ref[rows, sl], q_ref[:, sl]) + bias for sl in heads]
        m_n = [jnp.maximum(c[0], jnp.max(x, axis=0, keepdims=True)) for c, x in zip(carry, s)]
        p = [jnp.exp2(x - m) for x, m in zip(s, m_n)]
        pv = [_dot(vt_ref[kb, sl, :], x.astype(BF16)) for x, sl in zip(p, heads)]
        alpha = [jnp.exp2(c[0] - m) for c, m in zip(carry, m_n)]
        l_n = [a * c[1] + jnp.sum(x, axis=0, keepdims=True) for a, c, x in zip(alpha, carry, p)]
        return tuple((m, l, a * c[2] + y) for m, l, a, c, y in zip(m_n, l_n, alpha, carry, pv))

    init = tuple((jnp.full((1, tq), NEG_BIG, F32), jnp.zeros((1, tq), F32), jnp.zeros((DSA_HD, tq), F32))
                 for _ in heads)
    final = lax.fori_loop(0, nkb, att_body, init)
    for sl, (_, l_f, acc_f) in zip(heads, final):
        o_ref[:, sl] = ((acc_f / l_f).T * _silu(gate_ref[:, sl])).astype(o_ref.dtype)


def _dsa(qr, kr, vt, iqr, ikr, iws, pb, bsz, seq):
    m = qr.shape[0]
    w = BRANCH_W
    nq = seq // DSA_TQ
    nkb = seq // DSA_KB
    topk = min(TOPK_MAX, seq // 4)
    qrow = lambda b, i: (b * nq + i, 0)
    return pl.pallas_call(
        functools.partial(_dsa_kernel, topk=topk),
        name="dsa",
        grid=(bsz, nq),
        in_specs=[pl.BlockSpec((DSA_TQ, w), qrow),
                  pl.BlockSpec((seq, w), lambda b, i: (b, 0)),
                  pl.BlockSpec((nkb, w, DSA_KB), lambda b, i: (b, 0, 0)),
                  pl.BlockSpec((DSA_TQ, IDX_HEADS * IDX_K), qrow),
                  pl.BlockSpec((seq, IDX_K), lambda b, i: (b, 0)),
                  pl.BlockSpec((IDX_HEADS, DSA_TQ), lambda b, i: (0, b * nq + i)),
                  pl.BlockSpec((DSA_TQ, w), lambda b, i: (b * nq + i, 3))],
        out_specs=pl.BlockSpec((DSA_TQ, w), qrow),
        out_shape=jax.ShapeDtypeStruct((m, w), BF16),
        scratch_shapes=[pltpu.VMEM((nkb, DSA_KB, DSA_TQ), I32),
                        pltpu.VMEM((1, DSA_TQ), I32)],
        compiler_params=_cparams(("parallel", "arbitrary")),
    )(qr, kr, vt, iqr, ikr, iws[:, :IDX_HEADS].T, pb)


def _head_sum(x, bd):
    parts = [_dot(x[:, j * LANES:(j + 1) * LANES], bd, HIGHEST) for j in range(x.shape[1] // LANES)]
    return jnp.concatenate(parts, axis=1)


def _rwkv_prep_kernel(c_ref, cs_ref, pc_ref, pcs_ref, mu_ref, mus_ref, w0_ref, ww2_ref, a0_ref, wa2_ref,
                      kkw_ref, ka_ref, rk_ref, bd_ref,
                      r_o, wl_o, k_o, v_o, kk_o, a_o, sg_o, bon_o, *, blocks_per_seq):
    tb = c_ref.shape[0]
    first = (pl.program_id(0) % blocks_per_seq) == 0
    rowi = lax.broadcasted_iota(I32, (tb, 1), 0)

    def shift_lerp(c, prev8, mu):
        prev = jnp.where(first, 0.0, prev8[7:8, :])
        sh = jnp.where(rowi == 0, prev, pltpu.roll(c, 1, 0))
        return c + (sh - c) * mu

    w = BRANCH_W
    c = shift_lerp(c_ref[...], pc_ref[...], mu_ref[...])
    cs = shift_lerp(cs_ref[...], pcs_ref[...], mus_ref[...])
    r, k, v, gate = c[:, :w], c[:, w:2 * w], c[:, 2 * w:3 * w], c[:, 3 * w:]
    w_log = -RWKV_DECAY_SCALE * jax.nn.sigmoid(w0_ref[...] + _dot(jnp.tanh(cs[:, :LANES]), ww2_ref[...], HIGHEST))
    a = jax.nn.sigmoid(a0_ref[...] + _dot(cs[:, LANES:], wa2_ref[...], HIGHEST))
    kk = k * kkw_ref[...]
    k2 = k * (1.0 + (a - 1.0) * ka_ref[...])
    bd = bd_ref[...]
    kk = kk * lax.rsqrt(_head_sum(kk * kk, bd) + RWKV_KK_EPS)
    r_o[...] = r
    wl_o[...] = w_log
    k_o[...] = k2
    v_o[...] = v
    kk_o[...] = kk
    a_o[...] = a
    sg_o[...] = _silu(gate)
    bon_o[...] = _head_sum(r * k2 * rk_ref[...], bd) * v


def _rwkv_prep(pc, pcs, mu_main, mu_s, w0, ww2p, a0, wa2p, kkw, ka, rk, bd, seq):
    m = pc.shape[0]
    tb = RWKV_PREP_TB
    w = BRANCH_W
    rows = lambda i: (i, 0)
    prev = lambda i: (jnp.maximum(i * (tb // 8) - 1, 0), 0)
    const = lambda i: (0, 0)
    vec = pl.BlockSpec((1, w), const)
    out = pl.BlockSpec((tb, w), rows)
    return pl.pallas_call(
        functools.partial(_rwkv_prep_kernel, blocks_per_seq=seq // tb),
        name="rwkv_prep",
        grid=(m // tb,),
        in_specs=[pl.BlockSpec((tb, 4 * w), rows), pl.BlockSpec((tb, 2 * LANES), rows),
                  pl.BlockSpec((8, 4 * w), prev), pl.BlockSpec((8, 2 * LANES), prev),
                  pl.BlockSpec((1, 4 * w), const), pl.BlockSpec((1, 2 * LANES), const),
                  vec, pl.BlockSpec((LANES, w), const), vec, pl.BlockSpec((LANES, w), const),
                  vec, vec, vec, pl.BlockSpec((LANES, LANES), const)],
        out_specs=[out] * 8,
        out_shape=[jax.ShapeDtypeStruct((m, w), F32)] * 8,
        compiler_params=_cparams(("parallel",)),
    )(pc, pcs, pc, pcs, mu_main, mu_s, w0.reshape(1, w), ww2p, a0.reshape(1, w), wa2p,
      kkw.reshape(1, w), ka.reshape(1, w), rk.reshape(1, w), bd)


def _rwkv_scan_kernel(r_ref, wl_ref, k_ref, v_ref, kk_ref, a_ref, y_ref, lin_scr, yq_scr, h_scr, *, unroll):
    n = RWKV_L
    two = 2 * n
    nc = r_ref.shape[0] // n
    i2 = lax.broadcasted_iota(I32, (two, two), 0)
    j2 = lax.broadcasted_iota(I32, (two, two), 1)
    strict = i2 > j2
    incl = i2 >= j2
    same16 = (i2 // 16) == (j2 // 16)
    same_head = (i2 // RWKV_HD) == (j2 // RWKV_HD)
    eye = (i2 == j2).astype(F32)
    ii = lax.broadcasted_iota(I32, (n, n), 0)
    jj = lax.broadcasted_iota(I32, (n, n), 1)
    tril = (ii >= jj).astype(F32)
    head0 = lax.broadcasted_iota(I32, (n, LANES), 1) < RWKV_HD

    def mm(a, b):
        return _dot(a.astype(BF16), b.astype(BF16))

    def mm_nt(a, b):
        return _dot_nt(a.astype(BF16), b.astype(BF16))

    def mm_tn(a, b):
        return _dot_tn(a.astype(BF16), b.astype(BF16))

    def stack(x):
        return jnp.concatenate([jnp.where(head0, x, 0.0), jnp.where(head0, 0.0, x)], axis=0)

    def fold(x):
        return x[:n] + x[n:]

    def each(f, *lists):
        return [f(*args) for args in zip(*lists)]

    def tri_inv(a_low):
        a_d = each(lambda a: jnp.where(same16, a, 0.0), a_low)
        a_o = each(lambda a, d: a - d, a_low, a_d)
        a2 = each(mm, a_d, a_d)
        tick()
        a4 = each(mm, a2, a2)
        t_lo = each(lambda d, s: mm(eye + d, eye + s), a_d, a2)
        tick()
        a8 = each(mm, a4, a4)
        tick()
        t_hi = each(lambda x, y: mm(eye + x, eye + y), a4, a8)
        tick()
        t_d = each(mm, t_lo, t_hi)
        tick()
        nn = each(mm, t_d, a_o)
        tick()
        n2 = each(mm, nn, nn)
        tick()
        t_o = each(lambda x, y: mm(eye + x, eye + y), nn, n2)
        tick()
        return each(mm, t_o, t_d)

    pending = []

    def tick():
        if pending:
            pending.pop(0)()

    def build_group(g):
        rows = [pl.ds(pl.multiple_of((g * unroll + u) * n, n), n) for u in range(unroll)]
        wl = [wl_ref[r, :] for r in rows]
        k = [k_ref[r, :] for r in rows]
        v = [v_ref[r, :] for r in rows]
        kk = [kk_ref[r, :] for r in rows]
        beta = each(lambda x, r: x * a_ref[r, :], kk, rows)
        cum = each(lambda x: _dot(tril, x, HIGHEST), wl)
        cum_end = each(lambda x: x[n - 1:n, :], cum)
        g_inv = each(lambda x: jnp.exp(-x), cum)
        g_end = each(lambda e, x: jnp.exp(e - x), cum_end, cum)
        a_st = each(lambda x, c, w: stack(-x * jnp.exp(c - w)), kk, cum, wl)
        r_st = each(lambda r, c: stack(r_ref[r, :] * jnp.exp(c)), rows, cum)
        b_st = each(lambda x, y: stack(x * y), beta, g_inv)
        k_st = each(lambda x, y: stack(x * y), k, g_inv)
        v_st = each(stack, v)
        ar = each(lambda x, y: jnp.concatenate([x, y], axis=0), a_st, r_st)
        gb = each(mm_nt, ar, b_st)
        gk = each(mm_nt, ar, k_st)
        a_ab = each(lambda x: jnp.where(strict, x[:two], 0.0), gb)
        a_rb = each(lambda x: jnp.where(incl, x[two:], 0.0), gb)
        a_ak = each(lambda x: jnp.where(strict, x[:two], 0.0), gk)
        a_rk = each(lambda x: jnp.where(incl, x[two:], 0.0), gk)
        av2 = each(lambda x, y, z: mm(jnp.concatenate([x, y], axis=0), z), a_ak, a_rk, v_st)
        t_inv = tri_inv(a_ab)
        pq = each(lambda t, x, y: mm(t, jnp.concatenate([x, y[:two]], axis=1)), t_inv, a_st, av2)
        yr = each(mm, a_rb, pq)
        kv = each(lambda x, y, z: mm_tn(x * y, z), k, g_end, v)
        pq_d = each(lambda x: jnp.concatenate([fold(x[:, :LANES]), fold(x[:, LANES:])], axis=1), pq)
        gh = each(lambda x, y, z: mm_tn(x * y, z), beta, g_end, pq_d)
        while pending:
            tick()
        for u in range(unroll):
            ry = fold(r_st[u] + yr[u][:, :LANES])
            g_bd = jnp.where(same_head, gh[u][:, :LANES], 0.0) + eye * jnp.exp(cum_end[u])
            lin_scr[u] = jnp.concatenate([ry, g_bd], axis=0)
            yq_scr[u] = fold(yr[u][:, LANES:] + av2[u][two:])
            h_scr[u] = jnp.where(same_head, gh[u][:, LANES:] + kv[u], 0.0)

    def sweep_steps(g, state):
        def step(u):
            rows = pl.ds(pl.multiple_of((g * unroll + u) * n, n), n)
            both = mm(lin_scr[u], state[0])
            y_ref[rows, :] = both[:n] + yq_scr[u]
            state[0] = both[n:] + h_scr[u]
        return [functools.partial(step, u) for u in range(unroll)]

    ngroups = nc // unroll
    build_group(0)

    def body(g, st):
        state = [st]
        pending.extend(sweep_steps(g - 1, state))
        build_group(g)
        return state[0]

    st = lax.fori_loop(1, ngroups, body, jnp.zeros((two, two), F32))
    state = [st]
    for step in sweep_steps(ngroups - 1, state):
        step()


def _rwkv_scan(r, wl, k2, v, kk, a, bsz, seq):
    m = r.shape[0]
    npair = BRANCH_W // LANES
    blk = pl.BlockSpec((seq, LANES), lambda b, p: (b, p))
    u = RWKV_UNROLL
    return pl.pallas_call(
        functools.partial(_rwkv_scan_kernel, unroll=u),
        name="rwkv_scan",
        grid=(bsz, npair),
        in_specs=[blk] * 6,
        out_specs=blk,
        out_shape=jax.ShapeDtypeStruct((m, BRANCH_W), F32),
        scratch_shapes=[pltpu.VMEM((u, 3 * RWKV_L, LANES), F32), pltpu.VMEM((u, RWKV_L, LANES), F32),
                        pltpu.VMEM((u, LANES, LANES), F32)],
        compiler_params=_cparams(("parallel", "parallel")),
    )(r, wl, k2, v, kk, a)


def _rwkv_post_kernel(y_ref, bon_ref, sg_ref, g_ref, b_ref, bd_ref, o_ref):
    y = y_ref[...]
    bd = bd_ref[...]
    mu = _head_sum(y, bd) * (1.0 / RWKV_HD)
    yc = y - mu
    var = _head_sum(yc * yc, bd) * (1.0 / RWKV_HD)
    yn = yc * lax.rsqrt(var + RWKV_GN_EPS) * g_ref[...] + b_ref[...]
    o_ref[...] = ((yn + bon_ref[...]) * sg_ref[...]).astype(o_ref.dtype)


def _rwkv_post(y, bon, sg, g, b, bd):
    m, w = y.shape
    tb = RWKV_POST_TB
    rows = pl.BlockSpec((tb, w), lambda i: (i, 0))
    vec = pl.BlockSpec((1, w), lambda i: (0, 0))
    return pl.pallas_call(
        _rwkv_post_kernel,
        name="rwkv_post",
        grid=(m // tb,),
        in_specs=[rows, rows, rows, vec, vec, pl.BlockSpec((LANES, LANES), lambda i: (0, 0))],
        out_specs=rows,
        out_shape=jax.ShapeDtypeStruct((m, w), BF16),
        compiler_params=_cparams(("parallel",)),
    )(y, bon, sg, g.reshape(1, w), b.reshape(1, w), bd)


def _sgu_kernel(u_ref, v_ref, gate_ref, lng_ref, lnb_ref, ws_ref, bs_ref, o_ref):
    inv_sqrt2 = 0.7071067811865476

    def gelu(z):
        return 0.5 * z * (1.0 + lax.erf(z * inv_sqrt2))

    v = gelu(v_ref[...])
    mu = jnp.mean(v, axis=-1, keepdims=True)
    vc = v - mu
    var = jnp.mean(vc * vc, axis=-1, keepdims=True)
    vn = vc * lax.rsqrt(var + SGU_LN_EPS) * lng_ref[...] + lnb_ref[...]
    t = SGU_CHUNK
    ii = lax.broadcasted_iota(I32, (t, t), 0) // CHUNK
    jj = lax.broadcasted_iota(I32, (t, t), 1) // CHUNK
    mask = jj <= ii
    for g in range(SGU_GROUPS):
        ws = jnp.where(mask, ws_ref[g], 0.0)
        sl = slice(g * SGU_GW, (g + 1) * SGU_GW)
        bias = bs_ref[:, g:g + 1]
        for c in range(u_ref.shape[0] // t):
            rows = slice(c * t, (c + 1) * t)
            sv = _dot(ws, vn[rows, sl]) + bias
            o_ref[rows, sl] = (gelu(u_ref[rows, sl]) * sv * _silu(gate_ref[rows, sl])).astype(o_ref.dtype)


def _sgu(pd, lng, lnb, ws, bs_t):
    m = pd.shape[0]
    w = BRANCH_W
    tb = SGU_TB
    return pl.pallas_call(
        _sgu_kernel,
        name="sgu",
        grid=(m // tb,),
        in_specs=[pl.BlockSpec((tb, w), lambda i: (i, 0)),
                  pl.BlockSpec((tb, w), lambda i: (i, 1)),
                  pl.BlockSpec((tb, w), lambda i: (i, 2)),
                  pl.BlockSpec((1, w), lambda i: (0, 0)),
                  pl.BlockSpec((1, w), lambda i: (0, 0)),
                  pl.BlockSpec((SGU_GROUPS, SGU_CHUNK, SGU_CHUNK), lambda i: (0, 0, 0)),
                  pl.BlockSpec((SGU_CHUNK, SGU_GROUPS), lambda i: (0, 0))],
        out_specs=pl.BlockSpec((tb, w), lambda i: (i, 0)),
        out_shape=jax.ShapeDtypeStruct((m, w), BF16),
        compiler_params=_cparams(("parallel",)),
    )(pd, pd, pd, lng.reshape(1, w), lnb.reshape(1, w), ws, bs_t)


def _merge_kernel(ya_ref, yb_ref, yc_ref, yd_ref, ga_ref, gb_ref, gc_ref, gd_ref, wp_ref, o_ref):
    acc = None
    for g, (y_ref, g_ref) in enumerate(((ya_ref, ga_ref), (yb_ref, gb_ref), (yc_ref, gc_ref), (yd_ref, gd_ref))):
        term = jax.nn.sigmoid(g_ref[...]) * _dot(y_ref[...], wp_ref[g])
        acc = term if acc is None else acc + term
    o_ref[...] = acc.astype(o_ref.dtype)


def _merge(ya, yb, yc, yd, pm, wproj):
    m = ya.shape[0]
    w = BRANCH_W
    tm, tn = MERGE_TM, MERGE_TN
    nj = D_MODEL // tn
    ysp = pl.BlockSpec((tm, w), lambda j, i: (i, 0))
    gsp = [pl.BlockSpec((tm, tn), lambda j, i, g=g: (i, g * nj + j)) for g in range(N_BRANCH)]
    return pl.pallas_call(
        _merge_kernel,
        name="merge",
        grid=(nj, m // tm),
        in_specs=[ysp, ysp, ysp, ysp, *gsp,
                  pl.BlockSpec((N_BRANCH, w, tn), lambda j, i: (0, 0, j))],
        out_specs=pl.BlockSpec((tm, tn), lambda j, i: (i, j)),
        out_shape=jax.ShapeDtypeStruct((m, D_MODEL), BF16),
        compiler_params=_cparams(("parallel", "parallel")),
    )(ya, yb, yc, yd, pm, pm, pm, pm, wproj)


def _out_kernel(mg_ref, wo_ref, g_ref, x_ref, o_ref):
    y = _dot(mg_ref[...], wo_ref[...])
    y = y * lax.rsqrt(jnp.mean(y * y, axis=-1, keepdims=True) + NORM_EPS) * g_ref[...]
    o_ref[...] = x_ref[...] + y


def _out_proj(merged, wout, g, x2):
    m, d = x2.shape
    tm = OUT_TM
    return pl.pallas_call(
        _out_kernel,
        name="out_proj",
        grid=(m // tm,),
        in_specs=[pl.BlockSpec((tm, d), lambda i: (i, 0)),
                  pl.BlockSpec((d, d), lambda i: (0, 0)),
                  pl.BlockSpec((1, d), lambda i: (0, 0)),
                  pl.BlockSpec((tm, d), lambda i: (i, 0))],
        out_specs=pl.BlockSpec((tm, d), lambda i: (i, 0)),
        out_shape=jax.ShapeDtypeStruct((m, d), F32),
        compiler_params=_cparams(("parallel",)),
    )(merged, wout, g.reshape(1, d), x2)


def _split_cols(w, sizes):
    offs = np.cumsum((0,) + tuple(sizes))
    return [w[:, int(offs[i]):int(offs[i + 1])] for i in range(len(sizes))]


def _pad_cols(w, n):
    return jnp.pad(w, ((0, 0), (0, n - w.shape[1])))


def _pad_rows(w, n):
    return jnp.pad(w, ((0, n - w.shape[0]), (0, 0)))


def _rope_tables(positions, rot_dim, width):
    inv = ROPE_THETA ** (-jnp.arange(0, rot_dim, 2, dtype=F32) / rot_dim)
    ang = positions.astype(F32).reshape(-1, 1) * inv
    cos, sin = jnp.cos(ang), jnp.sin(ang)
    m = cos.shape[0]
    rest = width - rot_dim
    cos_h = jnp.concatenate([cos, cos, jnp.ones((m, rest), F32)], axis=1)
    sin_h = jnp.concatenate([-sin, sin, jnp.zeros((m, rest), F32)], axis=1)
    reps = LANES // width
    return jnp.tile(cos_h, (1, reps)), jnp.tile(sin_h, (1, reps))


def _mm_tn(n):
    for tn in (1024, 768, 640, 512, 256, 128):
        if n % tn == 0:
            return tn
    raise ValueError(n)


def kernel(x, positions, norm_pre, norm_post, w_in, gla_w_g2, gla_b_g, gla_head_g, rwkv_mu, rwkv_w0,
           rwkv_w_w2, rwkv_a0, rwkv_w_a2, rwkv_k_k, rwkv_k_a, rwkv_r_k, rwkv_lnx_g, rwkv_lnx_b,
           sgu_ln_g, sgu_ln_b, sgu_w_s, sgu_b_s, w_proj, w_out):
    bsz, seq, d = x.shape
    depth = w_in.shape[0]
    m = bsz * seq
    x2 = x.reshape(m, d)
    ca, sa = _rope_tables(positions, DSA_ROT, DSA_HD)
    ci, si = _rope_tables(positions, IDX_ROT, IDX_HD)
    lane_grp = np.arange(LANES) // RWKV_HD
    bd = jnp.asarray((lane_grp[:, None] == lane_grp[None, :]).astype(np.float32))

    for l in range(depth):
        wa, wb, wc, wd, wm = _split_cols(w_in[l], GROUP_COLS)
        a_q, a_k, a_v, a_glr, a_gate = _split_cols(wa, A_COLS)
        b_q, b_k, b_v, b_iq, b_ik, b_iw, b_gate = _split_cols(wb, B_COLS)
        c_r, c_k, c_v, c_wlr, c_alr, c_gate = _split_cols(wc, C_COLS)
        w_a = jnp.concatenate([a_q, a_k, a_v, a_gate, _pad_cols(a_glr, LANES)], axis=1).astype(BF16)
        w_b = jnp.concatenate([b_q, b_k, b_v, b_gate], axis=1).astype(BF16)
        w_bi = jnp.concatenate([b_iq, _pad_cols(b_ik, LANES), _pad_cols(b_iw, LANES)], axis=1).astype(BF16)
        w_c = jnp.concatenate([c_r, c_k, c_v, c_gate], axis=1).astype(BF16)
        w_cs = jnp.concatenate([_pad_cols(c_wlr, LANES), _pad_cols(c_alr, LANES)], axis=1).astype(BF16)
        w_d = wd.astype(BF16)
        w_m = wm.astype(BF16)

        h = _rmsnorm_cast(x2, norm_pre[l])
        pa, pb, pbi, pc, pcs, pd, pm = (_matmul(h, w, _mm_tn(w.shape[1]))
                                        for w in (w_a, w_b, w_bi, w_c, w_cs, w_d, w_m))

        ya = _gla(pa, _pad_rows(gla_w_g2[l], LANES), gla_b_g[l], gla_head_g[l], bsz, seq)

        qr, kr, vb, iqr, ikr, iws = _dsa_prep(pb, pbi, ca, sa, ci, si)
        yb = _dsa(qr, kr, vb, iqr, ikr, iws, pb, bsz, seq)

        mu_r, mu_k, mu_v, mu_wlr, mu_alr, mu_gate = _split_cols(rwkv_mu[l][None, :], C_COLS)
        mu_main = jnp.concatenate([mu_r, mu_k, mu_v, mu_gate], axis=1)
        mu_s = jnp.concatenate([_pad_cols(mu_wlr, LANES), _pad_cols(mu_alr, LANES)], axis=1)
        r, wl, k2, v, kk, a, sg, bon = _rwkv_prep(
            pc, pcs, mu_main, mu_s, rwkv_w0[l], _pad_rows(rwkv_w_w2[l], LANES), rwkv_a0[l],
            _pad_rows(rwkv_w_a2[l], LANES), rwkv_k_k[l], rwkv_k_a[l], rwkv_r_k[l].reshape(-1), bd, seq)
        y = _rwkv_scan(r, wl, k2, v, kk, a, bsz, seq)
        yc = _rwkv_post(y, bon, sg, rwkv_lnx_g[l], rwkv_lnx_b[l], bd)

        yd = _sgu(pd, sgu_ln_g[l], sgu_ln_b[l], sgu_w_s[l], sgu_b_s[l].T)

        merged = _merge(ya, yb, yc, yd, pm, w_proj[l].astype(BF16))
        x2 = _out_proj(merged, w_out[l].astype(BF16), norm_post[l], x2)
    return x2.reshape(bsz, seq, d)
```

```python
import functools

import numpy as np
import jax
import jax.numpy as jnp
from jax import lax
from jax.experimental import pallas as pl
from jax.experimental.pallas import tpu as pltpu

F32 = jnp.float32
BF16 = jnp.bfloat16
I32 = jnp.int32
HIGHEST = lax.Precision.HIGHEST

D_MODEL = 2048
CHUNK = 64
NORM_EPS = 1e-6
ROPE_THETA = 500000.0
N_BRANCH = 4
BRANCH_W = D_MODEL // 2

GLA_HEADS = 4
GLA_DK = BRANCH_W // 2 // GLA_HEADS
GLA_DV = BRANCH_W // GLA_HEADS
GLA_RANK = 16
GLA_TAU = 16.0

DSA_HEADS = 8
DSA_HD = BRANCH_W // DSA_HEADS
DSA_ROT = DSA_HD // 4
IDX_HEADS = 8
IDX_HD = 64
IDX_ROT = IDX_HD // 4
TOPK_MAX = 256

RWKV_HD = 64
RWKV_HEADS = BRANCH_W // RWKV_HD
RWKV_DECAY_RANK = 96
RWKV_A_RANK = 96
RWKV_DECAY_SCALE = 0.606531
RWKV_GN_EPS = 64e-5
RWKV_KK_EPS = 1e-12

SGU_CHUNK = 128
SGU_GROUPS = 8
SGU_GW = BRANCH_W // SGU_GROUPS
SGU_LN_EPS = 1e-5

A_COLS = (GLA_HEADS * GLA_DK, GLA_HEADS * GLA_DK, BRANCH_W, GLA_RANK, BRANCH_W)
B_COLS = (BRANCH_W, BRANCH_W, BRANCH_W, IDX_HEADS * IDX_HD, IDX_HD, IDX_HEADS, BRANCH_W)
C_COLS = (BRANCH_W, BRANCH_W, BRANCH_W, RWKV_DECAY_RANK, RWKV_A_RANK, BRANCH_W)
D_COLS = (2 * BRANCH_W, BRANCH_W)
GROUP_COLS = (sum(A_COLS), sum(B_COLS), sum(C_COLS), sum(D_COLS), N_BRANCH * D_MODEL)

LANES = 128
VMEM_LIMIT = 56 * 1024 * 1024
MXU_DEPTH = 256
IDX_K = MXU_DEPTH
NEG_BIG = -1e30
LOG2E = 1.4426950408889634
INT_MIN = -(2 ** 31)

P_M0 = 0
P_B0 = P_M0 + N_BRANCH * D_MODEL
P_C0 = P_B0 + 4 * BRANCH_W
P_D0 = P_C0 + 4 * BRANCH_W
P_A0 = P_D0 + 3 * BRANCH_W
P_COLS = P_A0 + 2 * GLA_HEADS * GLA_DK + 2 * BRANCH_W
S_IDX0 = 0
S_IDX_COLS = IDX_HEADS * IDX_HD + 2 * 128
S_RW0 = S_IDX0 + S_IDX_COLS
S_GLR0 = S_RW0 + 2 * 128
S_COLS = S_GLR0 + 128

NORM_TM = 512
MM_TM = 1024
MM_TN = 1024
GLA_TB = 512
DSA_PREP_TB = 512
DSA_TQ = 256
DSA_KB = 512
DSA_VROWS = DSA_HD + 16
RWKV_PREP_TB = 256
RWKV_L = 64
RWKV_UNROLL = 8
RWKV_POST_TB = 512
SGU_TB = 512
MERGE_TM = 512
MERGE_TN = 1024
OUT_TM = 512


def _cparams(sem):
    return pltpu.CompilerParams(dimension_semantics=sem, vmem_limit_bytes=VMEM_LIMIT)


def _dot(a, b, precision=None):
    return jnp.dot(a, b, preferred_element_type=F32, precision=precision)


def _dot_nt(a, b, precision=None):
    return lax.dot_general(a, b, (((1,), (1,)), ((), ())), preferred_element_type=F32,
                           precision=precision)


def _dot_tn(a, b, precision=None):
    return lax.dot_general(a, b, (((0,), (0,)), ((), ())), preferred_element_type=F32,
                           precision=precision)


def _silu(x):
    return x * jax.nn.sigmoid(x)


def _rmsnorm_cast_kernel(x_ref, g_ref, o_ref):
    x = x_ref[...]
    ms = jnp.mean(x * x, axis=-1, keepdims=True)
    o_ref[...] = (x * lax.rsqrt(ms + NORM_EPS) * g_ref[...]).astype(o_ref.dtype)


def _rmsnorm_cast(x2, g):
    m, d = x2.shape
    return pl.pallas_call(
        _rmsnorm_cast_kernel,
        name="prenorm",
        grid=(m // NORM_TM,),
        in_specs=[pl.BlockSpec((NORM_TM, d), lambda i: (i, 0)),
                  pl.BlockSpec((1, d), lambda i: (0, 0))],
        out_specs=pl.BlockSpec((NORM_TM, d), lambda i: (i, 0)),
        out_shape=jax.ShapeDtypeStruct((m, d), BF16),
        compiler_params=_cparams(("parallel",)),
    )(x2, g.reshape(1, d))


def _matmul_kernel(a_ref, b_ref, o_ref):
    o_ref[...] = _dot(a_ref[...], b_ref[...]).astype(o_ref.dtype)


def _matmul(a, b, tn, out_dtype=F32):
    m, k = a.shape
    n = b.shape[1]
    tm = min(MM_TM, m)
    return pl.pallas_call(
        _matmul_kernel,
        name="proj",
        grid=(m // tm, n // tn),
        in_specs=[pl.BlockSpec((tm, k), lambda i, j: (i, 0)),
                  pl.BlockSpec((k, tn), lambda i, j: (0, j))],
        out_specs=pl.BlockSpec((tm, tn), lambda i, j: (i, j)),
        out_shape=jax.ShapeDtypeStruct((m, n), out_dtype),
        compiler_params=_cparams(("parallel", "parallel")),
    )(a, b)


def _gla_kernel(q_ref, k_ref, v_ref, gate_ref, glr_ref, wg2_ref, bg_ref, hg_ref, o_ref, state_ref):
    @pl.when(pl.program_id(2) == 0)
    def _():
        state_ref[...] = jnp.zeros_like(state_ref)

    ii = lax.broadcasted_iota(I32, (CHUNK, CHUNK), 0)
    jj = lax.broadcasted_iota(I32, (CHUNK, CHUNK), 1)
    tril = (ii >= jj).astype(F32)
    chunks = [pl.ds(c * CHUNK, CHUNK) for c in range(GLA_TB // CHUNK)]
    z = _dot(glr_ref[...], wg2_ref[...], HIGHEST) + bg_ref[...]
    log_a = (jnp.minimum(z, 0.0) - jnp.log1p(jnp.exp(-jnp.abs(z)))) / GLA_TAU
    cum = [_dot(tril, log_a[c * CHUNK:(c + 1) * CHUNK], HIGHEST) for c in range(len(chunks))]
    total = [x[CHUNK - 1:CHUNK, :] for x in cum]
    k_dec = [k_ref[sl, :].astype(F32) * jnp.exp(t - x) for sl, t, x in zip(chunks, total, cum)]
    kv = [_dot_tn(v_ref[sl, :], kd.astype(BF16)) for sl, kd in zip(chunks, k_dec)]
    states = []
    st = state_ref[...]
    for t, x in zip(total, kv):
        st = st * jnp.exp(t) + x
        states.append(st)
    state_ref[...] = st
    outs = [_dot_nt((q_ref[sl, :].astype(F32) * (GLA_DK ** -0.5)).astype(BF16), s.astype(BF16))
            for sl, s in zip(chunks, states)]
    for sl, o in zip(chunks, outs):
        o = o * lax.rsqrt(jnp.mean(o * o, axis=-1, keepdims=True) + NORM_EPS) * hg_ref[...]
        o_ref[sl, :] = (o * _silu(gate_ref[sl, :].astype(F32))).astype(o_ref.dtype)


def _gla(p, ps, wg2p, bg, hg, bsz, seq):
    m = p.shape[0]
    nt = seq // GLA_TB
    row = lambda b, h, t: b * nt + t
    qb = P_A0 // GLA_DK
    kb = qb + GLA_HEADS
    vb = (P_A0 + 2 * GLA_HEADS * GLA_DK) // GLA_DV
    gb = vb + GLA_HEADS
    lb = S_GLR0 // LANES
    return pl.pallas_call(
        _gla_kernel,
        name="gla",
        grid=(bsz, GLA_HEADS, nt),
        in_specs=[
            pl.BlockSpec((GLA_TB, GLA_DK), lambda b, h, t: (row(b, h, t), qb + h)),
            pl.BlockSpec((GLA_TB, GLA_DK), lambda b, h, t: (row(b, h, t), kb + h)),
            pl.BlockSpec((GLA_TB, GLA_DV), lambda b, h, t: (row(b, h, t), vb + h)),
            pl.BlockSpec((GLA_TB, GLA_DV), lambda b, h, t: (row(b, h, t), gb + h)),
            pl.BlockSpec((GLA_TB, LANES), lambda b, h, t: (row(b, h, t), lb)),
            pl.BlockSpec((LANES, GLA_DK), lambda b, h, t: (0, h)),
            pl.BlockSpec((1, GLA_DK), lambda b, h, t: (0, h)),
            pl.BlockSpec((1, GLA_DV), lambda b, h, t: (0, 0)),
        ],
        out_specs=pl.BlockSpec((GLA_TB, GLA_DV), lambda b, h, t: (row(b, h, t), h)),
        out_shape=jax.ShapeDtypeStruct((m, BRANCH_W), BF16),
        scratch_shapes=[pltpu.VMEM((GLA_DV, GLA_DK), F32)],
        compiler_params=_cparams(("parallel", "parallel", "arbitrary")),
    )(p, p, p, p, ps, wg2p, bg.reshape(1, -1), hg.reshape(1, -1))


def _rope_tile(x, cos_t, sin_t, half, width):
    lane = lax.broadcasted_iota(I32, x.shape, 1) % width
    partner = jnp.where(lane < half, pltpu.roll(x, LANES - half, 1), pltpu.roll(x, half, 1))
    return x * cos_t + partner * sin_t


def _dsa_prep_kernel(q_ref, k_ref, v_ref, idx_ref, ca_ref, sa_ref, ci_ref, si_ref,
                     qo_ref, ko_ref, vo_ref, iqo_ref, iko_ref, iwo_ref):
    ca, sa = ca_ref[...], sa_ref[...]
    ci, si = ci_ref[...], si_ref[...]
    for h in range(DSA_HEADS):
        sl = slice(h * DSA_HD, (h + 1) * DSA_HD)
        q = _rope_tile(q_ref[:, sl].astype(F32), ca, sa, DSA_ROT // 2, DSA_HD) * (DSA_HD ** -0.5 * LOG2E)
        qo_ref[:, sl] = q.astype(qo_ref.dtype)
        ko_ref[:, sl] = _rope_tile(k_ref[:, sl].astype(F32), ca, sa, DSA_ROT // 2, DSA_HD).astype(ko_ref.dtype)
        vo_ref[0, h * DSA_VROWS:h * DSA_VROWS + DSA_HD, :] = v_ref[:, sl].astype(F32).T.astype(vo_ref.dtype)
        extra = lax.broadcasted_iota(I32, (DSA_VROWS - DSA_HD, q_ref.shape[0]), 0) == 0
        vo_ref[0, h * DSA_VROWS + DSA_HD:(h + 1) * DSA_VROWS, :] = jnp.where(extra, 1.0, 0.0).astype(vo_ref.dtype)
    niq = IDX_HEADS * IDX_HD
    low = lax.broadcasted_iota(I32, (q_ref.shape[0], LANES), 1) < IDX_HD

    def hi_lo(x):
        hi = x.astype(BF16).astype(F32)
        return hi, x - hi

    for j in range(niq // LANES):
        x = _rope_tile(idx_ref[:, j * LANES:(j + 1) * LANES], ci, si, IDX_ROT // 2, IDX_HD)
        hi, lo = hi_lo(x)
        hi_sw = pltpu.roll(hi, IDX_HD, 1)
        lo_sw = pltpu.roll(lo, IDX_HD, 1)
        base = 2 * j * IDX_K
        iqo_ref[:, base:base + LANES] = jnp.where(low, hi, hi_sw).astype(BF16)
        iqo_ref[:, base + LANES:base + IDX_K] = jnp.where(low, lo, 0.0).astype(BF16)
        iqo_ref[:, base + IDX_K:base + IDX_K + LANES] = jnp.where(low, hi_sw, hi).astype(BF16)
        iqo_ref[:, base + IDX_K + LANES:base + 2 * IDX_K] = jnp.where(low, lo_sw, 0.0).astype(BF16)
    hi, lo = hi_lo(_rope_tile(idx_ref[:, niq:niq + LANES], ci, si, IDX_ROT // 2, IDX_HD))
    iko_ref[:, :LANES] = jnp.where(low, hi, pltpu.roll(lo, IDX_HD, 1)).astype(BF16)
    iko_ref[:, LANES:] = jnp.where(low, hi, 0.0).astype(BF16)
    iwo_ref[...] = idx_ref[:, niq + LANES:niq + 2 * LANES] * (niq ** -0.5)


def _dsa_prep(p, ps, ca, sa, ci, si):
    m = p.shape[0]
    tb = DSA_KB
    w = BRANCH_W
    rows = lambda i: (i, 0)
    b0 = P_B0 // w
    return pl.pallas_call(
        _dsa_prep_kernel,
        name="dsa_prep",
        grid=(m // tb,),
        in_specs=[pl.BlockSpec((tb, w), lambda i: (i, b0)),
                  pl.BlockSpec((tb, w), lambda i: (i, b0 + 1)),
                  pl.BlockSpec((tb, w), lambda i: (i, b0 + 2)),
                  pl.BlockSpec((tb, S_IDX_COLS), lambda i: (i, S_IDX0 // S_IDX_COLS)),
                  pl.BlockSpec((tb, LANES), rows), pl.BlockSpec((tb, LANES), rows),
                  pl.BlockSpec((tb, LANES), rows), pl.BlockSpec((tb, LANES), rows)],
        out_specs=[pl.BlockSpec((tb, w), rows), pl.BlockSpec((tb, w), rows),
                   pl.BlockSpec((1, DSA_HEADS * DSA_VROWS, tb), lambda i: (i, 0, 0)),
                   pl.BlockSpec((tb, IDX_HEADS * IDX_K), rows), pl.BlockSpec((tb, IDX_K), rows),
                   pl.BlockSpec((tb, LANES), rows)],
        out_shape=[jax.ShapeDtypeStruct((m, w), BF16), jax.ShapeDtypeStruct((m, w), BF16),
                   jax.ShapeDtypeStruct((m // tb, DSA_HEADS * DSA_VROWS, tb), BF16),
                   jax.ShapeDtypeStruct((m, IDX_HEADS * IDX_K), BF16),
                   jax.ShapeDtypeStruct((m, IDX_K), BF16), jax.ShapeDtypeStruct((m, LANES), F32)],
        compiler_params=_cparams(("parallel",)),
    )(p, p, p, ps, ca, sa, ci, si)


def _dsa_kernel(q_ref, k_ref, vt_ref, iq_ref, ik_ref, iw_ref, gate_ref, o_ref, key_scr, tie_scr, *, topk):
    tq, kb_w = DSA_TQ, DSA_KB
    q0 = pl.program_id(1) * tq
    nkb = (q0 + tq + kb_w - 1) // kb_w
    qcol = lax.broadcasted_iota(I32, (1, tq), 1)
    qlim = ((q0 + qcol) // CHUNK + 1) * CHUNK
    sub_k = lax.broadcasted_iota(I32, (kb_w, 1), 0)
    sub = 8

    def score_body(kb, carry):
        ik = ik_ref[pl.ds(pl.multiple_of(kb * kb_w, kb_w), kb_w), :]
        dots = [_dot_nt(ik, iq_ref[:, h * IDX_K:(h + 1) * IDX_K]) for h in range(IDX_HEADS)]
        acc = jnp.zeros((kb_w, tq), F32)
        for h, d in enumerate(dots):
            acc = acc + iw_ref[h:h + 1, :] * jnp.maximum(d, 0.0)
        acc = acc + 0.0
        bits = lax.bitcast_convert_type(acc, I32)
        key = bits ^ ((bits >> 31) & 0x7FFFFFFF)
        sidx = kb * kb_w + sub_k
        key_scr[kb] = jnp.where(sidx < qlim, key, INT_MIN)
        return carry

    lax.fori_loop(0, nkb, score_body, 0)

    def count(pred):
        def body(kb, c):
            m = pred(key_scr[kb], kb * kb_w + sub_k).astype(I32)
            nparts = 8
            rows_part = kb_w // nparts
            parts = [jnp.sum(m[j * rows_part:(j + 1) * rows_part].reshape(rows_part // sub, sub, tq), axis=0)
                     for j in range(nparts)]
            while len(parts) > 1:
                parts = [a + b for a, b in zip(parts[::2], parts[1::2])]
            return c + parts[0]
        c = lax.fori_loop(0, nkb, body, jnp.zeros((sub, tq), I32))
        return jnp.sum(c, axis=0, keepdims=True)

    def bit_body(it, carry):
        res, cnt_res = carry
        trial = res | lax.shift_left(jnp.int32(1), 31 - it)
        cand = trial ^ INT_MIN
        cnt = count(lambda kt, s: kt >= cand)
        ok = cnt >= topk
        return jnp.where(ok, trial, res), jnp.where(ok, cnt, cnt_res)

    res, cnt_ge = lax.fori_loop(0, 32, bit_body,
                                (jnp.zeros((1, tq), I32), jnp.full((1, tq), nkb * kb_w, I32)))
    thr = res ^ INT_MIN
    seq_len = key_scr.shape[0] * kb_w
    tie_scr[...] = jnp.full(tie_scr.shape, seq_len, I32)
    has_tie = (cnt_ge > topk) & (thr != INT_MIN)

    @pl.when(jnp.max(jnp.where(has_tie, 1.0, 0.0)) > 0.5)
    def _():
        need = topk - count(lambda kt, s: kt > thr)
        nbits = max(1, int(np.ceil(np.log2(seq_len))))

        def tie_body(it, resj):
            trial = resj | lax.shift_left(jnp.int32(1), nbits - 1 - it)
            c = count(lambda kt, s: (kt == thr) & (s < trial))
            return jnp.where(c < need, trial, resj)

        tie_scr[...] = lax.fori_loop(0, nbits, tie_body, jnp.zeros((1, tq), I32))

    last_tie = tie_scr[...]

    def bias_body(kb, carry):
        kt = key_scr[kb]
        sidx = kb * kb_w + sub_k
        sel = ((kt > thr) | ((kt == thr) & (sidx <= last_tie))) & (kt != INT_MIN)
        key_scr[kb] = lax.bitcast_convert_type(jnp.where(sel, 0.0, NEG_BIG), I32)
        return carry

    lax.fori_loop(0, nkb, bias_body, 0)

    heads = [slice(h * DSA_HD, (h + 1) * DSA_HD) for h in range(DSA_HEADS)]

    def att_body(kb, carry):
        rows = pl.ds(pl.multiple_of(kb * kb_w, kb_w), kb_w)
        bias = lax.bitcast_convert_type(key_scr[kb], F32)
        s = [_dot_nt(k_ref[rows, sl], q_ref[:, sl]) + bias for sl in heads]
        m_n = [jnp.maximum(c[0], jnp.max(x, axis=0, keepdims=True)) for c, x in zip(carry, s)]
        p = [jnp.exp2(x - m) for x, m in zip(s, m_n)]
        pv = [_dot(vt_ref[kb, h * DSA_VROWS:(h + 1) * DSA_VROWS, :], x.astype(BF16)) for h, x in enumerate(p)]
        alpha = [jnp.exp2(c[0] - m) for c, m in zip(carry, m_n)]
        return tuple((m, a * c[1] + y) for m, a, c, y in zip(m_n, alpha, carry, pv))

    init = tuple((jnp.full((1, tq), NEG_BIG, F32), jnp.zeros((DSA_VROWS, tq), F32)) for _ in heads)
    final = lax.fori_loop(0, nkb, att_body, init)
    for sl, (_, acc_f) in zip(heads, final):
        o = (acc_f[:DSA_HD] / acc_f[DSA_HD:DSA_HD + 1]).T
        o_ref[:, sl] = (o * _silu(gate_ref[:, sl].astype(F32))).astype(o_ref.dtype)


def _dsa(qr, kr, vt, iqr, ikr, iws, p, bsz, seq):
    m = qr.shape[0]
    w = BRANCH_W
    nq = seq // DSA_TQ
    nkb = seq // DSA_KB
    topk = min(TOPK_MAX, seq // 4)
    qrow = lambda b, i: (b * nq + i, 0)
    return pl.pallas_call(
        functools.partial(_dsa_kernel, topk=topk),
        name="dsa",
        grid=(bsz, nq),
        in_specs=[pl.BlockSpec((DSA_TQ, w), qrow),
                  pl.BlockSpec((seq, w), lambda b, i: (b, 0)),
                  pl.BlockSpec((nkb, DSA_HEADS * DSA_VROWS, DSA_KB), lambda b, i: (b, 0, 0)),
                  pl.BlockSpec((DSA_TQ, IDX_HEADS * IDX_K), qrow),
                  pl.BlockSpec((seq, IDX_K), lambda b, i: (b, 0)),
                  pl.BlockSpec((IDX_HEADS, DSA_TQ), lambda b, i: (0, b * nq + i)),
                  pl.BlockSpec((DSA_TQ, w), lambda b, i: (b * nq + i, P_B0 // w + 3))],
        out_specs=pl.BlockSpec((DSA_TQ, w), qrow),
        out_shape=jax.ShapeDtypeStruct((m, w), BF16),
        scratch_shapes=[pltpu.VMEM((nkb, DSA_KB, DSA_TQ), I32),
                        pltpu.VMEM((1, DSA_TQ), I32)],
        compiler_params=_cparams(("parallel", "arbitrary")),
    )(qr, kr, vt, iqr, ikr, iws[:, :IDX_HEADS].T, p)


def _head_sum(x, bd):
    parts = [_dot(x[:, j * LANES:(j + 1) * LANES], bd, HIGHEST) for j in range(x.shape[1] // LANES)]
    return jnp.concatenate(parts, axis=1)


def _rwkv_prep_kernel(c_ref, cs_ref, pc_ref, pcs_ref, mu_ref, mus_ref, w0_ref, ww2_ref, a0_ref, wa2_ref,
                      kkw_ref, ka_ref, rk_ref, bd_ref,
                      r_o, wl_o, k_o, v_o, kk_o, a_o, sg_o, bon_o, *, blocks_per_seq):
    tb = c_ref.shape[0]
    first = (pl.program_id(0) % blocks_per_seq) == 0
    rowi = lax.broadcasted_iota(I32, (tb, 1), 0)

    def shift_lerp(c, prev_rows, mu):
        last = prev_rows.shape[0] - 1
        prev = jnp.where(first, 0.0, prev_rows[last:last + 1, :])
        sh = jnp.where(rowi == 0, prev, pltpu.roll(c, 1, 0))
        return c + (sh - c) * mu

    w = BRANCH_W
    c = shift_lerp(c_ref[...].astype(F32), pc_ref[...].astype(F32), mu_ref[...])
    cs = shift_lerp(cs_ref[...], pcs_ref[...], mus_ref[...])
    r, k, v, gate = c[:, :w], c[:, w:2 * w], c[:, 2 * w:3 * w], c[:, 3 * w:]
    w_log = -RWKV_DECAY_SCALE * jax.nn.sigmoid(w0_ref[...] + _dot(jnp.tanh(cs[:, :LANES]), ww2_ref[...], HIGHEST))
    a = jax.nn.sigmoid(a0_ref[...] + _dot(cs[:, LANES:], wa2_ref[...], HIGHEST))
    kk = k * kkw_ref[...]
    k2 = k * (1.0 + (a - 1.0) * ka_ref[...])
    bd = bd_ref[...]
    kk = kk * lax.rsqrt(_head_sum(kk * kk, bd) + RWKV_KK_EPS)
    r_o[...] = r.astype(r_o.dtype)
    wl_o[...] = w_log
    k_o[...] = k2.astype(k_o.dtype)
    v_o[...] = v.astype(v_o.dtype)
    kk_o[...] = kk.astype(kk_o.dtype)
    a_o[...] = a.astype(a_o.dtype)
    sg_o[...] = _silu(gate).astype(sg_o.dtype)
    bon_o[...] = (_head_sum(r * k2 * rk_ref[...], bd) * v).astype(bon_o.dtype)


def _rwkv_prep(p, ps, mu_main, mu_s, w0, ww2p, a0, wa2p, kkw, ka, rk, bd, seq):
    m = p.shape[0]
    tb = RWKV_PREP_TB
    w = BRANCH_W
    pr16, pr8 = 16, 8
    cblk, sblk = P_C0 // (4 * w), S_RW0 // (2 * LANES)
    const = lambda i: (0, 0)
    vec = pl.BlockSpec((1, w), const)
    out = pl.BlockSpec((tb, w), lambda i: (i, 0))
    dts = [BF16, F32, BF16, BF16, BF16, BF16, BF16, BF16]
    return pl.pallas_call(
        functools.partial(_rwkv_prep_kernel, blocks_per_seq=seq // tb),
        name="rwkv_prep",
        grid=(m // tb,),
        in_specs=[pl.BlockSpec((tb, 4 * w), lambda i: (i, cblk)),
                  pl.BlockSpec((tb, 2 * LANES), lambda i: (i, sblk)),
                  pl.BlockSpec((pr16, 4 * w), lambda i: (jnp.maximum(i * (tb // pr16) - 1, 0), cblk)),
                  pl.BlockSpec((pr8, 2 * LANES), lambda i: (jnp.maximum(i * (tb // pr8) - 1, 0), sblk)),
                  pl.BlockSpec((1, 4 * w), const), pl.BlockSpec((1, 2 * LANES), const),
                  vec, pl.BlockSpec((LANES, w), const), vec, pl.BlockSpec((LANES, w), const),
                  vec, vec, vec, pl.BlockSpec((LANES, LANES), const)],
        out_specs=[out] * 8,
        out_shape=[jax.ShapeDtypeStruct((m, w), dt) for dt in dts],
        compiler_params=_cparams(("parallel",)),
    )(p, ps, p, ps, mu_main, mu_s, w0.reshape(1, w), ww2p, a0.reshape(1, w), wa2p,
      kkw.reshape(1, w), ka.reshape(1, w), rk.reshape(1, w), bd)


def _rwkv_scan_kernel(r_ref, wl_ref, k_ref, v_ref, kk_ref, a_ref, y_ref, lin_scr, yq_scr, h_scr, *, unroll):
    n = RWKV_L
    two = 2 * n
    nc = r_ref.shape[0] // n
    i2 = lax.broadcasted_iota(I32, (two, two), 0)
    j2 = lax.broadcasted_iota(I32, (two, two), 1)
    strict = i2 > j2
    incl = i2 >= j2
    same16 = (i2 // 16) == (j2 // 16)
    same_head = (i2 // RWKV_HD) == (j2 // RWKV_HD)
    eye = (i2 == j2).astype(F32)
    ii = lax.broadcasted_iota(I32, (n, n), 0)
    jj = lax.broadcasted_iota(I32, (n, n), 1)
    tril = (ii >= jj).astype(F32)
    head0 = lax.broadcasted_iota(I32, (n, LANES), 1) < RWKV_HD

    def mm(a, b):
        return _dot(a.astype(BF16), b.astype(BF16))

    def mm_nt(a, b):
        return _dot_nt(a.astype(BF16), b.astype(BF16))

    def mm_tn(a, b):
        return _dot_tn(a.astype(BF16), b.astype(BF16))

    def stack(x):
        return jnp.concatenate([jnp.where(head0, x, 0.0), jnp.where(head0, 0.0, x)], axis=0)

    def fold(x):
        return x[:n] + x[n:]

    def each(f, *lists):
        return [f(*args) for args in zip(*lists)]

    def tri_inv(a_low):
        a_d = each(lambda a: jnp.where(same16, a, 0.0), a_low)
        a_o = each(lambda a, d: a - d, a_low, a_d)
        a2 = each(mm, a_d, a_d)
        tick()
        a4 = each(mm, a2, a2)
        t_lo = each(lambda d, s: mm(eye + d, eye + s), a_d, a2)
        tick()
        a8 = each(mm, a4, a4)
        tick()
        t_hi = each(lambda x, y: mm(eye + x, eye + y), a4, a8)
        tick()
        t_d = each(mm, t_lo, t_hi)
        tick()
        nn = each(mm, t_d, a_o)
        tick()
        n2 = each(mm, nn, nn)
        tick()
        t_o = each(lambda x, y: mm(eye + x, eye + y), nn, n2)
        tick()
        return each(mm, t_o, t_d)

    pending = []

    def tick():
        if pending:
            pending.pop(0)()

    def build_group(g):
        rows = [pl.ds(pl.multiple_of((g * unroll + u) * n, n), n) for u in range(unroll)]
        wl = [wl_ref[r, :] for r in rows]
        k = [k_ref[r, :].astype(F32) for r in rows]
        v = [v_ref[r, :].astype(F32) for r in rows]
        kk = [kk_ref[r, :].astype(F32) for r in rows]
        beta = each(lambda x, r: x * a_ref[r, :].astype(F32), kk, rows)
        cum = each(lambda x: _dot(tril, x, HIGHEST), wl)
        cum_end = each(lambda x: x[n - 1:n, :], cum)
        g_inv = each(lambda x: jnp.exp(-x), cum)
        g_end = each(lambda e, x: jnp.exp(e - x), cum_end, cum)
        a_st = each(lambda x, c, w: stack(-x * jnp.exp(c - w)), kk, cum, wl)
        r_st = each(lambda r, c: stack(r_ref[r, :].astype(F32) * jnp.exp(c)), rows, cum)
        b_st = each(lambda x, y: stack(x * y), beta, g_inv)
        k_st = each(lambda x, y: stack(x * y), k, g_inv)
        v_st = each(stack, v)
        ar = each(lambda x, y: jnp.concatenate([x, y], axis=0), a_st, r_st)
        gb = each(mm_nt, ar, b_st)
        gk = each(mm_nt, ar, k_st)
        a_ab = each(lambda x: jnp.where(strict, x[:two], 0.0), gb)
        a_rb = each(lambda x: jnp.where(incl, x[two:], 0.0), gb)
        a_ak = each(lambda x: jnp.where(strict, x[:two], 0.0), gk)
        a_rk = each(lambda x: jnp.where(incl, x[two:], 0.0), gk)
        av2 = each(lambda x, y, z: mm(jnp.concatenate([x, y], axis=0), z), a_ak, a_rk, v_st)
        t_inv = tri_inv(a_ab)
        pq = each(lambda t, x, y: mm(t, jnp.concatenate([x, y[:two]], axis=1)), t_inv, a_st, av2)
        yr = each(mm, a_rb, pq)
        kv = each(lambda x, y, z: mm_tn(x * y, z), k, g_end, v)
        pq_d = each(lambda x: jnp.concatenate([fold(x[:, :LANES]), fold(x[:, LANES:])], axis=1), pq)
        gh = each(lambda x, y, z: mm_tn(x * y, z), beta, g_end, pq_d)
        while pending:
            tick()
        for u in range(unroll):
            ry = fold(r_st[u] + yr[u][:, :LANES])
            g_bd = jnp.where(same_head, gh[u][:, :LANES], 0.0) + eye * jnp.exp(cum_end[u])
            lin_scr[u] = jnp.concatenate([ry, g_bd], axis=0)
            yq_scr[u] = fold(yr[u][:, LANES:] + av2[u][two:])
            h_scr[u] = jnp.where(same_head, gh[u][:, LANES:] + kv[u], 0.0)

    def sweep_steps(g, state):
        def step(u):
            rows = pl.ds(pl.multiple_of((g * unroll + u) * n, n), n)
            both = mm(lin_scr[u], state[0])
            y_ref[rows, :] = both[:n] + yq_scr[u]
            state[0] = both[n:] + h_scr[u]
        return [functools.partial(step, u) for u in range(unroll)]

    ngroups = nc // unroll
    build_group(0)

    def body(g, st):
        state = [st]
        pending.extend(sweep_steps(g - 1, state))
        build_group(g)
        return state[0]

    st = lax.fori_loop(1, ngroups, body, jnp.zeros((two, two), F32))
    state = [st]
    for step in sweep_steps(ngroups - 1, state):
        step()


def _rwkv_scan(r, wl, k2, v, kk, a, bsz, seq):
    m = r.shape[0]
    npair = BRANCH_W // LANES
    blk = pl.BlockSpec((seq, LANES), lambda b, p: (b, p))
    u = RWKV_UNROLL
    return pl.pallas_call(
        functools.partial(_rwkv_scan_kernel, unroll=u),
        name="rwkv_scan",
        grid=(bsz, npair),
        in_specs=[blk] * 6,
        out_specs=blk,
        out_shape=jax.ShapeDtypeStruct((m, BRANCH_W), F32),
        scratch_shapes=[pltpu.VMEM((u, 3 * RWKV_L, LANES), F32), pltpu.VMEM((u, RWKV_L, LANES), F32),
                        pltpu.VMEM((u, LANES, LANES), F32)],
        compiler_params=_cparams(("parallel", "parallel")),
    )(r, wl, k2, v, kk, a)


def _rwkv_post_kernel(y_ref, bon_ref, sg_ref, g_ref, b_ref, bd_ref, o_ref):
    y = y_ref[...]
    bd = bd_ref[...]
    mu = _head_sum(y, bd) * (1.0 / RWKV_HD)
    yc = y - mu
    var = _head_sum(yc * yc, bd) * (1.0 / RWKV_HD)
    yn = yc * lax.rsqrt(var + RWKV_GN_EPS) * g_ref[...] + b_ref[...]
    o_ref[...] = ((yn + bon_ref[...].astype(F32)) * sg_ref[...].astype(F32)).astype(o_ref.dtype)


def _rwkv_post(y, bon, sg, g, b, bd):
    m, w = y.shape
    tb = RWKV_POST_TB
    rows = pl.BlockSpec((tb, w), lambda i: (i, 0))
    vec = pl.BlockSpec((1, w), lambda i: (0, 0))
    return pl.pallas_call(
        _rwkv_post_kernel,
        name="rwkv_post",
        grid=(m // tb,),
        in_specs=[rows, rows, rows, vec, vec, pl.BlockSpec((LANES, LANES), lambda i: (0, 0))],
        out_specs=rows,
        out_shape=jax.ShapeDtypeStruct((m, w), BF16),
        compiler_params=_cparams(("parallel",)),
    )(y, bon, sg, g.reshape(1, w), b.reshape(1, w), bd)


def _sgu_kernel(u_ref, v_ref, gate_ref, lng_ref, lnb_ref, ws_ref, bs_ref, o_ref):
    inv_sqrt2 = 0.7071067811865476

    def gelu(z):
        return 0.5 * z * (1.0 + lax.erf(z * inv_sqrt2))

    v = gelu(v_ref[...].astype(F32))
    mu = jnp.mean(v, axis=-1, keepdims=True)
    vc = v - mu
    var = jnp.mean(vc * vc, axis=-1, keepdims=True)
    vn = vc * lax.rsqrt(var + SGU_LN_EPS) * lng_ref[...] + lnb_ref[...]
    t = SGU_CHUNK
    ii = lax.broadcasted_iota(I32, (t, t), 0) // CHUNK
    jj = lax.broadcasted_iota(I32, (t, t), 1) // CHUNK
    mask = jj <= ii
    for g in range(SGU_GROUPS):
        ws = jnp.where(mask, ws_ref[g], 0.0)
        sl = slice(g * SGU_GW, (g + 1) * SGU_GW)
        bias = bs_ref[:, g:g + 1]
        for c in range(u_ref.shape[0] // t):
            rows = slice(c * t, (c + 1) * t)
            sv = _dot(ws, vn[rows, sl]) + bias
            u = gelu(u_ref[rows, sl].astype(F32))
            o_ref[rows, sl] = (u * sv * _silu(gate_ref[rows, sl].astype(F32))).astype(o_ref.dtype)


def _sgu(p, lng, lnb, ws, bs_t):
    m = p.shape[0]
    w = BRANCH_W
    tb = SGU_TB
    d0 = P_D0 // w
    return pl.pallas_call(
        _sgu_kernel,
        name="sgu",
        grid=(m // tb,),
        in_specs=[pl.BlockSpec((tb, w), lambda i: (i, d0)),
                  pl.BlockSpec((tb, w), lambda i: (i, d0 + 1)),
                  pl.BlockSpec((tb, w), lambda i: (i, d0 + 2)),
                  pl.BlockSpec((1, w), lambda i: (0, 0)),
                  pl.BlockSpec((1, w), lambda i: (0, 0)),
                  pl.BlockSpec((SGU_GROUPS, SGU_CHUNK, SGU_CHUNK), lambda i: (0, 0, 0)),
                  pl.BlockSpec((SGU_CHUNK, SGU_GROUPS), lambda i: (0, 0))],
        out_specs=pl.BlockSpec((tb, w), lambda i: (i, 0)),
        out_shape=jax.ShapeDtypeStruct((m, w), BF16),
        compiler_params=_cparams(("parallel",)),
    )(p, p, p, lng.reshape(1, w), lnb.reshape(1, w), ws, bs_t)


def _merge_kernel(ya_ref, yb_ref, yc_ref, yd_ref, ga_ref, gb_ref, gc_ref, gd_ref, wp_ref, o_ref):
    acc = None
    for g, (y_ref, g_ref) in enumerate(((ya_ref, ga_ref), (yb_ref, gb_ref), (yc_ref, gc_ref), (yd_ref, gd_ref))):
        term = jax.nn.sigmoid(g_ref[...].astype(F32)) * _dot(y_ref[...], wp_ref[g])
        acc = term if acc is None else acc + term
    o_ref[...] = acc.astype(o_ref.dtype)


def _merge(ya, yb, yc, yd, p, wproj):
    m = ya.shape[0]
    w = BRANCH_W
    tm, tn = MERGE_TM, MERGE_TN
    nj = D_MODEL // tn
    m0 = P_M0 // tn
    ysp = pl.BlockSpec((tm, w), lambda j, i: (i, 0))
    gsp = [pl.BlockSpec((tm, tn), lambda j, i, g=g: (i, m0 + g * nj + j)) for g in range(N_BRANCH)]
    return pl.pallas_call(
        _merge_kernel,
        name="merge",
        grid=(nj, m // tm),
        in_specs=[ysp, ysp, ysp, ysp, *gsp,
                  pl.BlockSpec((N_BRANCH, w, tn), lambda j, i: (0, 0, j))],
        out_specs=pl.BlockSpec((tm, tn), lambda j, i: (i, j)),
        out_shape=jax.ShapeDtypeStruct((m, D_MODEL), BF16),
        compiler_params=_cparams(("parallel", "parallel")),
    )(ya, yb, yc, yd, p, p, p, p, wproj)


def _out_kernel(mg_ref, wo_ref, g_ref, x_ref, o_ref):
    y = _dot(mg_ref[...], wo_ref[...])
    y = y * lax.rsqrt(jnp.mean(y * y, axis=-1, keepdims=True) + NORM_EPS) * g_ref[...]
    o_ref[...] = x_ref[...] + y


def _out_proj(merged, wout, g, x2):
    m, d = x2.shape
    tm = OUT_TM
    return pl.pallas_call(
        _out_kernel,
        name="out_proj",
        grid=(m // tm,),
        in_specs=[pl.BlockSpec((tm, d), lambda i: (i, 0)),
                  pl.BlockSpec((d, d), lambda i: (0, 0)),
                  pl.BlockSpec((1, d), lambda i: (0, 0)),
                  pl.BlockSpec((tm, d), lambda i: (i, 0))],
        out_specs=pl.BlockSpec((tm, d), lambda i: (i, 0)),
        out_shape=jax.ShapeDtypeStruct((m, d), F32),
        compiler_params=_cparams(("parallel",)),
    )(merged, wout, g.reshape(1, d), x2)


def _split_cols(w, sizes):
    offs = np.cumsum((0,) + tuple(sizes))
    return [w[:, int(offs[i]):int(offs[i + 1])] for i in range(len(sizes))]


def _pad_cols(w, n):
    return jnp.pad(w, ((0, 0), (0, n - w.shape[1])))


def _pad_rows(w, n):
    return jnp.pad(w, ((0, n - w.shape[0]), (0, 0)))


def _rope_tables(positions, rot_dim, width):
    inv = ROPE_THETA ** (-jnp.arange(0, rot_dim, 2, dtype=F32) / rot_dim)
    ang = positions.astype(F32).reshape(-1, 1) * inv
    cos, sin = jnp.cos(ang), jnp.sin(ang)
    m = cos.shape[0]
    rest = width - rot_dim
    cos_h = jnp.concatenate([cos, cos, jnp.ones((m, rest), F32)], axis=1)
    sin_h = jnp.concatenate([-sin, sin, jnp.zeros((m, rest), F32)], axis=1)
    reps = LANES // width
    return jnp.tile(cos_h, (1, reps)), jnp.tile(sin_h, (1, reps))


def kernel(x, positions, norm_pre, norm_post, w_in, gla_w_g2, gla_b_g, gla_head_g, rwkv_mu, rwkv_w0,
           rwkv_w_w2, rwkv_a0, rwkv_w_a2, rwkv_k_k, rwkv_k_a, rwkv_r_k, rwkv_lnx_g, rwkv_lnx_b,
           sgu_ln_g, sgu_ln_b, sgu_w_s, sgu_b_s, w_proj, w_out):
    bsz, seq, d = x.shape
    depth = w_in.shape[0]
    m = bsz * seq
    x2 = x.reshape(m, d)
    ca, sa = _rope_tables(positions, DSA_ROT, DSA_HD)
    ci, si = _rope_tables(positions, IDX_ROT, IDX_HD)
    lane_grp = np.arange(LANES) // RWKV_HD
    bd = jnp.asarray((lane_grp[:, None] == lane_grp[None, :]).astype(np.float32))

    for l in range(depth):
        wa, wb, wc, wd, wm = _split_cols(w_in[l], GROUP_COLS)
        a_q, a_k, a_v, a_glr, a_gate = _split_cols(wa, A_COLS)
        b_q, b_k, b_v, b_iq, b_ik, b_iw, b_gate = _split_cols(wb, B_COLS)
        c_r, c_k, c_v, c_wlr, c_alr, c_gate = _split_cols(wc, C_COLS)
        w_main = jnp.concatenate([wm, b_q, b_k, b_v, b_gate, c_r, c_k, c_v, c_gate, wd,
                                  a_q, a_k, a_v, a_gate], axis=1).astype(BF16)
        w_small = jnp.concatenate([b_iq] + [_pad_cols(t, LANES) for t in (b_ik, b_iw, c_wlr, c_alr, a_glr)],
                                  axis=1).astype(BF16)

        h = _rmsnorm_cast(x2, norm_pre[l])
        p = _matmul(h, w_main, MM_TN, BF16)
        ps = _matmul(h, w_small, S_COLS, F32)

        ya = _gla(p, ps, _pad_rows(gla_w_g2[l], LANES), gla_b_g[l], gla_head_g[l], bsz, seq)

        qr, kr, vt, iqr, ikr, iws = _dsa_prep(p, ps, ca, sa, ci, si)
        yb = _dsa(qr, kr, vt, iqr, ikr, iws, p, bsz, seq)

        mu_r, mu_k, mu_v, mu_wlr, mu_alr, mu_gate = _split_cols(rwkv_mu[l][None, :], C_COLS)
        mu_main = jnp.concatenate([mu_r, mu_k, mu_v, mu_gate], axis=1)
        mu_s = jnp.concatenate([_pad_cols(mu_wlr, LANES), _pad_cols(mu_alr, LANES)], axis=1)
        r, wl, k2, v, kk, a, sg, bon = _rwkv_prep(
            p, ps, mu_main, mu_s, rwkv_w0[l], _pad_rows(rwkv_w_w2[l], LANES), rwkv_a0[l],
            _pad_rows(rwkv_w_a2[l], LANES), rwkv_k_k[l], rwkv_k_a[l], rwkv_r_k[l].reshape(-1), bd, seq)
        y = _rwkv_scan(r, wl, k2, v, kk, a, bsz, seq)
        yc = _rwkv_post(y, bon, sg, rwkv_lnx_g[l], rwkv_lnx_b[l], bd)

        yd = _sgu(p, sgu_ln_g[l], sgu_ln_b[l], sgu_w_s[l], sgu_b_s[l].T)

        merged = _merge(ya, yb, yc, yd, p, w_proj[l].astype(BF16))
        x2 = _out_proj(merged, w_out[l].astype(BF16), norm_post[l], x2)
    return x2.reshape(bsz, seq, d)
```

```python
import functools

import numpy as np
import jax
import jax.numpy as jnp
from jax import lax
from jax.experimental import pallas as pl
from jax.experimental.pallas import tpu as pltpu

F32 = jnp.float32
BF16 = jnp.bfloat16
I32 = jnp.int32
HIGHEST = lax.Precision.HIGHEST

D_MODEL = 2048
CHUNK = 64
NORM_EPS = 1e-6
ROPE_THETA = 500000.0
N_BRANCH = 4
BRANCH_W = D_MODEL // 2

GLA_HEADS = 4
GLA_DK = BRANCH_W // 2 // GLA_HEADS
GLA_DV = BRANCH_W // GLA_HEADS
GLA_RANK = 16
GLA_TAU = 16.0

DSA_HEADS = 8
DSA_HD = BRANCH_W // DSA_HEADS
DSA_ROT = DSA_HD // 4
IDX_HEADS = 8
IDX_HD = 64
IDX_ROT = IDX_HD // 4
TOPK_MAX = 256

RWKV_HD = 64
RWKV_HEADS = BRANCH_W // RWKV_HD
RWKV_DECAY_RANK = 96
RWKV_A_RANK = 96
RWKV_DECAY_SCALE = 0.606531
RWKV_GN_EPS = 64e-5
RWKV_KK_EPS = 1e-12

SGU_CHUNK = 128
SGU_GROUPS = 8
SGU_GW = BRANCH_W // SGU_GROUPS
SGU_LN_EPS = 1e-5

A_COLS = (GLA_HEADS * GLA_DK, GLA_HEADS * GLA_DK, BRANCH_W, GLA_RANK, BRANCH_W)
B_COLS = (BRANCH_W, BRANCH_W, BRANCH_W, IDX_HEADS * IDX_HD, IDX_HD, IDX_HEADS, BRANCH_W)
C_COLS = (BRANCH_W, BRANCH_W, BRANCH_W, RWKV_DECAY_RANK, RWKV_A_RANK, BRANCH_W)
D_COLS = (2 * BRANCH_W, BRANCH_W)
GROUP_COLS = (sum(A_COLS), sum(B_COLS), sum(C_COLS), sum(D_COLS), N_BRANCH * D_MODEL)

LANES = 128
VMEM_LIMIT = 56 * 1024 * 1024
MXU_DEPTH = 256
IDX_K = MXU_DEPTH
NEG_BIG = -1e30
LOG2E = 1.4426950408889634
INT_MIN = -(2 ** 31)

P_M0 = 0
P_B0 = P_M0 + N_BRANCH * D_MODEL
P_C0 = P_B0 + 4 * BRANCH_W
P_D0 = P_C0 + 4 * BRANCH_W
P_A0 = P_D0 + 3 * BRANCH_W
P_COLS = P_A0 + 2 * GLA_HEADS * GLA_DK + 2 * BRANCH_W
S_IDX0 = 0
S_IDX_COLS = IDX_HEADS * IDX_HD + 2 * 128
S_RW0 = S_IDX0 + S_IDX_COLS
S_GLR0 = S_RW0 + 2 * 128
S_COLS = S_GLR0 + 128

NORM_TM = 512
MM_TM = 1024
MM_TN = 1024
GLA_TB = 512
DSA_PREP_TB = 512
DSA_TQ = 256
DSA_KB = 512
DSA_VROWS = DSA_HD + 16
DSA_ATT_SPLIT = 1
RWKV_PREP_TB = 256
RWKV_L = 64
RWKV_UNROLL = 8
RWKV_POST_TB = 512
SGU_TB = 512
MERGE_TM = 512
MERGE_TN = 1024
OUT_TM = 512


def _cparams(sem):
    return pltpu.CompilerParams(dimension_semantics=sem, vmem_limit_bytes=VMEM_LIMIT)


def _dot(a, b, precision=None):
    return jnp.dot(a, b, preferred_element_type=F32, precision=precision)


def _dot_nt(a, b, precision=None):
    return lax.dot_general(a, b, (((1,), (1,)), ((), ())), preferred_element_type=F32,
                           precision=precision)


def _dot_tn(a, b, precision=None):
    return lax.dot_general(a, b, (((0,), (0,)), ((), ())), preferred_element_type=F32,
                           precision=precision)


def _bf16_terms(x, n):
    terms = []
    for _ in range(n):
        t = x.astype(BF16)
        terms.append(t)
        x = x - t.astype(F32)
    return terms


def _dot_exact_lhs(a, b, nterms):
    a16 = a.astype(BF16)
    out = None
    for t in _bf16_terms(b, nterms):
        d = _dot(a16, t)
        out = d if out is None else out + d
    return out


def _dot_exact_rhs(a, b, nterms):
    b16 = b.astype(BF16)
    out = None
    for t in _bf16_terms(a, nterms):
        d = _dot(t, b16)
        out = d if out is None else out + d
    return out


def _dot_3pass(a, b):
    a_hi, a_lo = _bf16_terms(a, 2)
    b_hi, b_lo = _bf16_terms(b, 2)
    return _dot(a_hi, b_hi) + _dot(a_lo, b_hi) + _dot(a_hi, b_lo)


def _silu(x):
    return x * jax.nn.sigmoid(x)


def _rmsnorm_cast_kernel(x_ref, g_ref, o_ref):
    x = x_ref[...]
    ms = jnp.mean(x * x, axis=-1, keepdims=True)
    o_ref[...] = (x * lax.rsqrt(ms + NORM_EPS) * g_ref[...]).astype(o_ref.dtype)


def _rmsnorm_cast(x2, g):
    m, d = x2.shape
    return pl.pallas_call(
        _rmsnorm_cast_kernel,
        name="prenorm",
        grid=(m // NORM_TM,),
        in_specs=[pl.BlockSpec((NORM_TM, d), lambda i: (i, 0)),
                  pl.BlockSpec((1, d), lambda i: (0, 0))],
        out_specs=pl.BlockSpec((NORM_TM, d), lambda i: (i, 0)),
        out_shape=jax.ShapeDtypeStruct((m, d), BF16),
        compiler_params=_cparams(("parallel",)),
    )(x2, g.reshape(1, d))


def _matmul_kernel(a_ref, b_ref, o_ref):
    o_ref[...] = _dot(a_ref[...], b_ref[...]).astype(o_ref.dtype)


def _matmul(a, b, tn, out_dtype=F32):
    m, k = a.shape
    n = b.shape[1]
    tm = min(MM_TM, m)
    return pl.pallas_call(
        _matmul_kernel,
        name="proj",
        grid=(m // tm, n // tn),
        in_specs=[pl.BlockSpec((tm, k), lambda i, j: (i, 0)),
                  pl.BlockSpec((k, tn), lambda i, j: (0, j))],
        out_specs=pl.BlockSpec((tm, tn), lambda i, j: (i, j)),
        out_shape=jax.ShapeDtypeStruct((m, n), out_dtype),
        compiler_params=_cparams(("parallel", "parallel")),
    )(a, b)


def _gla_kernel(q_ref, k_ref, v_ref, gate_ref, glr_ref, wg2_ref, bg_ref, hg_ref, o_ref, state_ref):
    @pl.when(pl.program_id(2) == 0)
    def _():
        state_ref[...] = jnp.zeros_like(state_ref)

    ii = lax.broadcasted_iota(I32, (CHUNK, CHUNK), 0)
    jj = lax.broadcasted_iota(I32, (CHUNK, CHUNK), 1)
    tril = (ii >= jj).astype(F32)
    chunks = [pl.ds(c * CHUNK, CHUNK) for c in range(GLA_TB // CHUNK)]
    z = _dot_3pass(glr_ref[...], wg2_ref[...]) + bg_ref[...]
    log_a = (jnp.minimum(z, 0.0) - jnp.log1p(jnp.exp(-jnp.abs(z)))) / GLA_TAU
    cum = [_dot_exact_lhs(tril, log_a[c * CHUNK:(c + 1) * CHUNK], 3) for c in range(len(chunks))]
    total = [x[CHUNK - 1:CHUNK, :] for x in cum]
    k_dec = [k_ref[sl, :].astype(F32) * jnp.exp(t - x) for sl, t, x in zip(chunks, total, cum)]
    kv = [_dot_tn(v_ref[sl, :], kd.astype(BF16)) for sl, kd in zip(chunks, k_dec)]
    states = []
    st = state_ref[...]
    for t, x in zip(total, kv):
        st = st * jnp.exp(t) + x
        states.append(st)
    state_ref[...] = st
    outs = [_dot_nt((q_ref[sl, :].astype(F32) * (GLA_DK ** -0.5)).astype(BF16), s.astype(BF16))
            for sl, s in zip(chunks, states)]
    for sl, o in zip(chunks, outs):
        o = o * lax.rsqrt(jnp.mean(o * o, axis=-1, keepdims=True) + NORM_EPS) * hg_ref[...]
        o_ref[sl, :] = (o * _silu(gate_ref[sl, :].astype(F32))).astype(o_ref.dtype)


def _gla(p, ps, wg2p, bg, hg, bsz, seq):
    m = p.shape[0]
    nt = seq // GLA_TB
    row = lambda b, h, t: b * nt + t
    qb = P_A0 // GLA_DK
    kb = qb + GLA_HEADS
    vb = (P_A0 + 2 * GLA_HEADS * GLA_DK) // GLA_DV
    gb = vb + GLA_HEADS
    lb = S_GLR0 // LANES
    return pl.pallas_call(
        _gla_kernel,
        name="gla",
        grid=(bsz, GLA_HEADS, nt),
        in_specs=[
            pl.BlockSpec((GLA_TB, GLA_DK), lambda b, h, t: (row(b, h, t), qb + h)),
            pl.BlockSpec((GLA_TB, GLA_DK), lambda b, h, t: (row(b, h, t), kb + h)),
            pl.BlockSpec((GLA_TB, GLA_DV), lambda b, h, t: (row(b, h, t), vb + h)),
            pl.BlockSpec((GLA_TB, GLA_DV), lambda b, h, t: (row(b, h, t), gb + h)),
            pl.BlockSpec((GLA_TB, LANES), lambda b, h, t: (row(b, h, t), lb)),
            pl.BlockSpec((LANES, GLA_DK), lambda b, h, t: (0, h)),
            pl.BlockSpec((1, GLA_DK), lambda b, h, t: (0, h)),
            pl.BlockSpec((1, GLA_DV), lambda b, h, t: (0, 0)),
        ],
        out_specs=pl.BlockSpec((GLA_TB, GLA_DV), lambda b, h, t: (row(b, h, t), h)),
        out_shape=jax.ShapeDtypeStruct((m, BRANCH_W), BF16),
        scratch_shapes=[pltpu.VMEM((GLA_DV, GLA_DK), F32)],
        compiler_params=_cparams(("parallel", "parallel", "arbitrary")),
    )(p, p, p, p, ps, wg2p, bg.reshape(1, -1), hg.reshape(1, -1))


def _rope_tile(x, cos_t, sin_t, half, width):
    lane = lax.broadcasted_iota(I32, x.shape, 1) % width
    partner = jnp.where(lane < half, pltpu.roll(x, LANES - half, 1), pltpu.roll(x, half, 1))
    return x * cos_t + partner * sin_t


def _dsa_prep_kernel(q_ref, k_ref, v_ref, idx_ref, ca_ref, sa_ref, ci_ref, si_ref,
                     qo_ref, ko_ref, vo_ref, iqo_ref, iko_ref, iwo_ref):
    ca, sa = ca_ref[...], sa_ref[...]
    ci, si = ci_ref[...], si_ref[...]
    for h in range(DSA_HEADS):
        sl = slice(h * DSA_HD, (h + 1) * DSA_HD)
        q = _rope_tile(q_ref[:, sl].astype(F32), ca, sa, DSA_ROT // 2, DSA_HD) * (DSA_HD ** -0.5 * LOG2E)
        qo_ref[:, sl] = q.astype(qo_ref.dtype)
        ko_ref[:, sl] = _rope_tile(k_ref[:, sl].astype(F32), ca, sa, DSA_ROT // 2, DSA_HD).astype(ko_ref.dtype)
        vo_ref[0, h * DSA_VROWS:h * DSA_VROWS + DSA_HD, :] = v_ref[:, sl].astype(F32).T.astype(vo_ref.dtype)
        extra = lax.broadcasted_iota(I32, (DSA_VROWS - DSA_HD, q_ref.shape[0]), 0) == 0
        vo_ref[0, h * DSA_VROWS + DSA_HD:(h + 1) * DSA_VROWS, :] = jnp.where(extra, 1.0, 0.0).astype(vo_ref.dtype)
    niq = IDX_HEADS * IDX_HD
    low = lax.broadcasted_iota(I32, (q_ref.shape[0], LANES), 1) < IDX_HD

    def hi_lo(x):
        hi = x.astype(BF16).astype(F32)
        return hi, x - hi

    for j in range(niq // LANES):
        x = _rope_tile(idx_ref[:, j * LANES:(j + 1) * LANES], ci, si, IDX_ROT // 2, IDX_HD)
        hi, lo = hi_lo(x)
        hi_sw = pltpu.roll(hi, IDX_HD, 1)
        lo_sw = pltpu.roll(lo, IDX_HD, 1)
        base = 2 * j * IDX_K
        iqo_ref[:, base:base + LANES] = jnp.where(low, hi, hi_sw).astype(BF16)
        iqo_ref[:, base + LANES:base + IDX_K] = jnp.where(low, lo, 0.0).astype(BF16)
        iqo_ref[:, base + IDX_K:base + IDX_K + LANES] = jnp.where(low, hi_sw, hi).astype(BF16)
        iqo_ref[:, base + IDX_K + LANES:base + 2 * IDX_K] = jnp.where(low, lo_sw, 0.0).astype(BF16)
    hi, lo = hi_lo(_rope_tile(idx_ref[:, niq:niq + LANES], ci, si, IDX_ROT // 2, IDX_HD))
    iko_ref[:, :LANES] = jnp.where(low, hi, pltpu.roll(lo, IDX_HD, 1)).astype(BF16)
    iko_ref[:, LANES:] = jnp.where(low, hi, 0.0).astype(BF16)
    iwo_ref[...] = idx_ref[:, niq + LANES:niq + 2 * LANES] * (niq ** -0.5)


def _dsa_prep(p, ps, ca, sa, ci, si):
    m = p.shape[0]
    tb = DSA_KB
    w = BRANCH_W
    rows = lambda i: (i, 0)
    b0 = P_B0 // w
    return pl.pallas_call(
        _dsa_prep_kernel,
        name="dsa_prep",
        grid=(m // tb,),
        in_specs=[pl.BlockSpec((tb, w), lambda i: (i, b0)),
                  pl.BlockSpec((tb, w), lambda i: (i, b0 + 1)),
                  pl.BlockSpec((tb, w), lambda i: (i, b0 + 2)),
                  pl.BlockSpec((tb, S_IDX_COLS), lambda i: (i, S_IDX0 // S_IDX_COLS)),
                  pl.BlockSpec((tb, LANES), rows), pl.BlockSpec((tb, LANES), rows),
                  pl.BlockSpec((tb, LANES), rows), pl.BlockSpec((tb, LANES), rows)],
        out_specs=[pl.BlockSpec((tb, w), rows), pl.BlockSpec((tb, w), rows),
                   pl.BlockSpec((1, DSA_HEADS * DSA_VROWS, tb), lambda i: (i, 0, 0)),
                   pl.BlockSpec((tb, IDX_HEADS * IDX_K), rows), pl.BlockSpec((tb, IDX_K), rows),
                   pl.BlockSpec((tb, LANES), rows)],
        out_shape=[jax.ShapeDtypeStruct((m, w), BF16), jax.ShapeDtypeStruct((m, w), BF16),
                   jax.ShapeDtypeStruct((m // tb, DSA_HEADS * DSA_VROWS, tb), BF16),
                   jax.ShapeDtypeStruct((m, IDX_HEADS * IDX_K), BF16),
                   jax.ShapeDtypeStruct((m, IDX_K), BF16), jax.ShapeDtypeStruct((m, LANES), F32)],
        compiler_params=_cparams(("parallel",)),
    )(p, p, p, ps, ca, sa, ci, si)


def _dsa_kernel(q_ref, k_ref, vt_ref, iq_ref, ik_ref, iw_ref, gate_ref, o_ref, key_scr, hi_scr, tie_scr,
                *, topk):
    tq, kb_w = DSA_TQ, DSA_KB
    q0 = pl.program_id(1) * tq
    nkb = (q0 + tq + kb_w - 1) // kb_w
    qcol = lax.broadcasted_iota(I32, (1, tq), 1)
    qlim = ((q0 + qcol) // CHUNK + 1) * CHUNK
    sub_k = lax.broadcasted_iota(I32, (kb_w, 1), 0)
    sub = 8

    def score_body(kb, carry):
        ik = ik_ref[pl.ds(pl.multiple_of(kb * kb_w, kb_w), kb_w), :]
        dots = [_dot_nt(ik, iq_ref[:, h * IDX_K:(h + 1) * IDX_K]) for h in range(IDX_HEADS)]
        acc = jnp.zeros((kb_w, tq), F32)
        for h, d in enumerate(dots):
            acc = acc + iw_ref[h:h + 1, :] * jnp.maximum(d, 0.0)
        acc = acc + 0.0
        bits = lax.bitcast_convert_type(acc, I32)
        key = bits ^ ((bits >> 31) & 0x7FFFFFFF)
        sidx = kb * kb_w + sub_k
        key = jnp.where(sidx < qlim, key, INT_MIN)
        key_scr[kb] = key
        hi_scr[kb] = (key >> 16).astype(jnp.int16)
        return carry

    lax.fori_loop(0, nkb, score_body, 0)

    def count_tiles(load, pred, rows_per_vreg, dtype):
        def body(kb, c):
            m = pred(load(kb), kb * kb_w + sub_k).astype(dtype)
            if dtype == I32:
                nparts = 8
                rows_part = kb_w // nparts
                parts = [jnp.sum(m[j * rows_part:(j + 1) * rows_part]
                                 .reshape(rows_part // rows_per_vreg, rows_per_vreg, tq), axis=0)
                         for j in range(nparts)]
            else:
                parts = [m[j * rows_per_vreg:(j + 1) * rows_per_vreg] for j in range(kb_w // rows_per_vreg)]
            while len(parts) > 1:
                parts = [a + b for a, b in zip(parts[::2], parts[1::2])]
            return c + parts[0]
        c = lax.fori_loop(0, nkb, body, jnp.zeros((rows_per_vreg, tq), dtype))
        return jnp.sum(c.astype(I32), axis=0, keepdims=True)

    def count(pred):
        return count_tiles(lambda kb: key_scr[kb], pred, sub, I32)

    def count_hi(pred):
        return count_tiles(lambda kb: hi_scr[kb], pred, 2 * sub, jnp.int16)

    def bit_body(it, carry, high_only):
        res, cnt_res = carry
        trial = res | lax.shift_left(jnp.int32(1), 31 - it)
        cand = trial ^ INT_MIN
        if high_only:
            cand_hi = (cand >> 16).astype(jnp.int16)
            cnt = count_hi(lambda kt, s: kt >= cand_hi)
        else:
            cnt = count(lambda kt, s: kt >= cand)
        ok = cnt >= topk
        return jnp.where(ok, trial, res), jnp.where(ok, cnt, cnt_res)

    carry = (jnp.zeros((1, tq), I32), jnp.full((1, tq), nkb * kb_w, I32))
    carry = lax.fori_loop(0, 16, functools.partial(bit_body, high_only=True), carry)
    res, cnt_ge = lax.fori_loop(16, 32, functools.partial(bit_body, high_only=False), carry)
    thr = res ^ INT_MIN
    seq_len = key_scr.shape[0] * kb_w
    tie_scr[...] = jnp.full(tie_scr.shape, seq_len, I32)
    has_tie = (cnt_ge > topk) & (thr != INT_MIN)

    @pl.when(jnp.max(jnp.where(has_tie, 1.0, 0.0)) > 0.5)
    def _():
        need = topk - count(lambda kt, s: kt > thr)
        nbits = max(1, int(np.ceil(np.log2(seq_len))))

        def tie_body(it, resj):
            trial = resj | lax.shift_left(jnp.int32(1), nbits - 1 - it)
            c = count(lambda kt, s: (kt == thr) & (s < trial))
            return jnp.where(c < need, trial, resj)

        tie_scr[...] = lax.fori_loop(0, nbits, tie_body, jnp.zeros((1, tq), I32))

    last_tie = tie_scr[...]

    def bias_body(kb, carry):
        kt = key_scr[kb]
        sidx = kb * kb_w + sub_k
        sel = ((kt > thr) | ((kt == thr) & (sidx <= last_tie))) & (kt != INT_MIN)
        key_scr[kb] = lax.bitcast_convert_type(jnp.where(sel, 0.0, NEG_BIG), I32)
        return carry

    lax.fori_loop(0, nkb, bias_body, 0)

    heads = [slice(h * DSA_HD, (h + 1) * DSA_HD) for h in range(DSA_HEADS)]

    def att_body(kb, carry):
        rows = pl.ds(pl.multiple_of(kb * kb_w, kb_w), kb_w)
        bias = lax.bitcast_convert_type(key_scr[kb], F32)
        sub_w = kb_w // DSA_ATT_SPLIT
        for part in range(DSA_ATT_SPLIT):
            krows = pl.ds(pl.multiple_of(kb * kb_w + part * sub_w, sub_w), sub_w)
            kcols = slice(part * sub_w, (part + 1) * sub_w)
            bias_p = bias[kcols]
            hs = range(DSA_HEADS)
            s = [_dot_nt(k_ref[krows, heads[h]], q_ref[:, heads[h]]) + bias_p for h in hs]
            m_n = [jnp.maximum(carry[h][0], jnp.max(x, axis=0, keepdims=True)) for h, x in zip(hs, s)]
            p = [jnp.exp2(x - m) for x, m in zip(s, m_n)]
            pv = [_dot(vt_ref[kb, h * DSA_VROWS:(h + 1) * DSA_VROWS, kcols], x.astype(BF16))
                  for h, x in zip(hs, p)]
            alpha = [jnp.exp2(carry[h][0] - m) for h, m in zip(hs, m_n)]
            carry = tuple((m, a * carry[h][1] + y) for h, m, a, y in zip(hs, m_n, alpha, pv))
        return carry

    init = tuple((jnp.full((1, tq), NEG_BIG, F32), jnp.zeros((DSA_VROWS, tq), F32)) for _ in heads)
    final = lax.fori_loop(0, nkb, att_body, init)
    for sl, (_, acc_f) in zip(heads, final):
        o = (acc_f[:DSA_HD] / acc_f[DSA_HD:DSA_HD + 1]).T
        o_ref[:, sl] = (o * _silu(gate_ref[:, sl].astype(F32))).astype(o_ref.dtype)


def _dsa(qr, kr, vt, iqr, ikr, iws, p, bsz, seq):
    m = qr.shape[0]
    w = BRANCH_W
    nq = seq // DSA_TQ
    nkb = seq // DSA_KB
    topk = min(TOPK_MAX, seq // 4)
    qrow = lambda b, i: (b * nq + i, 0)
    return pl.pallas_call(
        functools.partial(_dsa_kernel, topk=topk),
        name="dsa",
        grid=(bsz, nq),
        in_specs=[pl.BlockSpec((DSA_TQ, w), qrow),
                  pl.BlockSpec((seq, w), lambda b, i: (b, 0)),
                  pl.BlockSpec((nkb, DSA_HEADS * DSA_VROWS, DSA_KB), lambda b, i: (b, 0, 0)),
                  pl.BlockSpec((DSA_TQ, IDX_HEADS * IDX_K), qrow),
                  pl.BlockSpec((seq, IDX_K), lambda b, i: (b, 0)),
                  pl.BlockSpec((IDX_HEADS, DSA_TQ), lambda b, i: (0, b * nq + i)),
                  pl.BlockSpec((DSA_TQ, w), lambda b, i: (b * nq + i, P_B0 // w + 3))],
        out_specs=pl.BlockSpec((DSA_TQ, w), qrow),
        out_shape=jax.ShapeDtypeStruct((m, w), BF16),
        scratch_shapes=[pltpu.VMEM((nkb, DSA_KB, DSA_TQ), I32),
                        pltpu.VMEM((nkb, DSA_KB, DSA_TQ), jnp.int16),
                        pltpu.VMEM((1, DSA_TQ), I32)],
        compiler_params=_cparams(("parallel", "arbitrary")),
    )(qr, kr, vt, iqr, ikr, iws[:, :IDX_HEADS].T, p)


def _head_sum(x, bd):
    parts = [_dot_exact_rhs(x[:, j * LANES:(j + 1) * LANES], bd, 2) for j in range(x.shape[1] // LANES)]
    return jnp.concatenate(parts, axis=1)


def _rwkv_prep_kernel(c_ref, cs_ref, pc_ref, pcs_ref, mu_ref, mus_ref, w0_ref, ww2_ref, a0_ref, wa2_ref,
                      kkw_ref, ka_ref, rk_ref, bd_ref,
                      r_o, wl_o, k_o, v_o, kk_o, a_o, sg_o, bon_o, *, blocks_per_seq):
    tb = c_ref.shape[0]
    first = (pl.program_id(0) % blocks_per_seq) == 0
    rowi = lax.broadcasted_iota(I32, (tb, 1), 0)

    def shift_lerp(c, prev_rows, mu):
        last = prev_rows.shape[0] - 1
        prev = jnp.where(first, 0.0, prev_rows[last:last + 1, :])
        sh = jnp.where(rowi == 0, prev, pltpu.roll(c, 1, 0))
        return c + (sh - c) * mu

    w = BRANCH_W
    c = shift_lerp(c_ref[...].astype(F32), pc_ref[...].astype(F32), mu_ref[...])
    cs = shift_lerp(cs_ref[...], pcs_ref[...], mus_ref[...])
    r, k, v, gate = c[:, :w], c[:, w:2 * w], c[:, 2 * w:3 * w], c[:, 3 * w:]
    w_log = -RWKV_DECAY_SCALE * jax.nn.sigmoid(w0_ref[...] + _dot_3pass(jnp.tanh(cs[:, :LANES]), ww2_ref[...]))
    a = jax.nn.sigmoid(a0_ref[...] + _dot_3pass(cs[:, LANES:], wa2_ref[...]))
    kk = k * kkw_ref[...]
    k2 = k * (1.0 + (a - 1.0) * ka_ref[...])
    bd = bd_ref[...]
    kk = kk * lax.rsqrt(_head_sum(kk * kk, bd) + RWKV_KK_EPS)
    r_o[...] = r.astype(r_o.dtype)
    wl_o[...] = w_log
    k_o[...] = k2.astype(k_o.dtype)
    v_o[...] = v.astype(v_o.dtype)
    kk_o[...] = kk.astype(kk_o.dtype)
    a_o[...] = a.astype(a_o.dtype)
    sg_o[...] = _silu(gate).astype(sg_o.dtype)
    bon_o[...] = (_head_sum(r * k2 * rk_ref[...], bd) * v).astype(bon_o.dtype)


def _rwkv_prep(p, ps, mu_main, mu_s, w0, ww2p, a0, wa2p, kkw, ka, rk, bd, seq):
    m = p.shape[0]
    tb = RWKV_PREP_TB
    w = BRANCH_W
    pr16, pr8 = 16, 8
    cblk, sblk = P_C0 // (4 * w), S_RW0 // (2 * LANES)
    const = lambda i: (0, 0)
    vec = pl.BlockSpec((1, w), const)
    out = pl.BlockSpec((tb, w), lambda i: (i, 0))
    dts = [BF16, F32, BF16, BF16, BF16, BF16, BF16, BF16]
    return pl.pallas_call(
        functools.partial(_rwkv_prep_kernel, blocks_per_seq=seq // tb),
        name="rwkv_prep",
        grid=(m // tb,),
        in_specs=[pl.BlockSpec((tb, 4 * w), lambda i: (i, cblk)),
                  pl.BlockSpec((tb, 2 * LANES), lambda i: (i, sblk)),
                  pl.BlockSpec((pr16, 4 * w), lambda i: (jnp.maximum(i * (tb // pr16) - 1, 0), cblk)),
                  pl.BlockSpec((pr8, 2 * LANES), lambda i: (jnp.maximum(i * (tb // pr8) - 1, 0), sblk)),
                  pl.BlockSpec((1, 4 * w), const), pl.BlockSpec((1, 2 * LANES), const),
                  vec, pl.BlockSpec((LANES, w), const), vec, pl.BlockSpec((LANES, w), const),
                  vec, vec, vec, pl.BlockSpec((LANES, LANES), const)],
        out_specs=[out] * 8,
        out_shape=[jax.ShapeDtypeStruct((m, w), dt) for dt in dts],
        compiler_params=_cparams(("parallel",)),
    )(p, ps, p, ps, mu_main, mu_s, w0.reshape(1, w), ww2p, a0.reshape(1, w), wa2p,
      kkw.reshape(1, w), ka.reshape(1, w), rk.reshape(1, w), bd)


def _rwkv_scan_kernel(r_ref, wl_ref, k_ref, v_ref, kk_ref, a_ref, y_ref, lin_scr, yq_scr, h_scr, *, unroll):
    n = RWKV_L
    two = 2 * n
    nc = r_ref.shape[0] // n
    i2 = lax.broadcasted_iota(I32, (two, two), 0)
    j2 = lax.broadcasted_iota(I32, (two, two), 1)
    strict = i2 > j2
    incl = i2 >= j2
    same16 = (i2 // 16) == (j2 // 16)
    same_head = (i2 // RWKV_HD) == (j2 // RWKV_HD)
    eye = (i2 == j2).astype(F32)
    ii = lax.broadcasted_iota(I32, (n, n), 0)
    jj = lax.broadcasted_iota(I32, (n, n), 1)
    tril = (ii >= jj).astype(F32)
    head0 = lax.broadcasted_iota(I32, (n, LANES), 1) < RWKV_HD

    def mm(a, b):
        return _dot(a.astype(BF16), b.astype(BF16))

    def mm_nt(a, b):
        return _dot_nt(a.astype(BF16), b.astype(BF16))

    def mm_tn(a, b):
        return _dot_tn(a.astype(BF16), b.astype(BF16))

    def stack(x):
        return jnp.concatenate([jnp.where(head0, x, 0.0), jnp.where(head0, 0.0, x)], axis=0)

    def fold(x):
        return x[:n] + x[n:]

    def each(f, *lists):
        return [f(*args) for args in zip(*lists)]

    def tri_inv(a_low):
        a_d = each(lambda a: jnp.where(same16, a, 0.0), a_low)
        a_o = each(lambda a, d: a - d, a_low, a_d)
        a2 = each(mm, a_d, a_d)
        tick()
        a4 = each(mm, a2, a2)
        t_lo = each(lambda d, s: mm(eye + d, eye + s), a_d, a2)
        tick()
        a8 = each(mm, a4, a4)
        tick()
        t_hi = each(lambda x, y: mm(eye + x, eye + y), a4, a8)
        tick()
        t_d = each(mm, t_lo, t_hi)
        tick()
        nn = each(mm, t_d, a_o)
        tick()
        n2 = each(mm, nn, nn)
        tick()
        t_o = each(lambda x, y: mm(eye + x, eye + y), nn, n2)
        tick()
        return each(mm, t_o, t_d)

    pending = []

    def tick():
        if pending:
            pending.pop(0)()

    def build_group(g):
        rows = [pl.ds(pl.multiple_of((g * unroll + u) * n, n), n) for u in range(unroll)]
        wl = [wl_ref[r, :] for r in rows]
        k = [k_ref[r, :].astype(F32) for r in rows]
        v = [v_ref[r, :].astype(F32) for r in rows]
        kk = [kk_ref[r, :].astype(F32) for r in rows]
        beta = each(lambda x, r: x * a_ref[r, :].astype(F32), kk, rows)
        cum = each(lambda x: _dot_exact_lhs(tril, x, 3), wl)
        cum_end = each(lambda x: x[n - 1:n, :], cum)
        g_inv = each(lambda x: jnp.exp(-x), cum)
        g_end = each(lambda e, x: jnp.exp(e - x), cum_end, cum)
        a_st = each(lambda x, c, w: stack(-x * jnp.exp(c - w)), kk, cum, wl)
        r_st = each(lambda r, c: stack(r_ref[r, :].astype(F32) * jnp.exp(c)), rows, cum)
        b_st = each(lambda x, y: stack(x * y), beta, g_inv)
        k_st = each(lambda x, y: stack(x * y), k, g_inv)
        v_st = each(stack, v)
        ar = each(lambda x, y: jnp.concatenate([x, y], axis=0), a_st, r_st)
        gb = each(mm_nt, ar, b_st)
        gk = each(mm_nt, ar, k_st)
        a_ab = each(lambda x: jnp.where(strict, x[:two], 0.0), gb)
        a_rb = each(lambda x: jnp.where(incl, x[two:], 0.0), gb)
        a_ak = each(lambda x: jnp.where(strict, x[:two], 0.0), gk)
        a_rk = each(lambda x: jnp.where(incl, x[two:], 0.0), gk)
        av2 = each(lambda x, y, z: mm(jnp.concatenate([x, y], axis=0), z), a_ak, a_rk, v_st)
        t_inv = tri_inv(a_ab)
        pq = each(lambda t, x, y: mm(t, jnp.concatenate([x, y[:two]], axis=1)), t_inv, a_st, av2)
        yr = each(mm, a_rb, pq)
        kv = each(lambda x, y, z: mm_tn(x * y, z), k, g_end, v)
        pq_d = each(lambda x: jnp.concatenate([fold(x[:, :LANES]), fold(x[:, LANES:])], axis=1), pq)
        gh = each(lambda x, y, z: mm_tn(x * y, z), beta, g_end, pq_d)
        while pending:
            tick()
        for u in range(unroll):
            ry = fold(r_st[u] + yr[u][:, :LANES])
            g_bd = jnp.where(same_head, gh[u][:, :LANES], 0.0) + eye * jnp.exp(cum_end[u])
            lin_scr[u] = jnp.concatenate([ry, g_bd], axis=0)
            yq_scr[u] = fold(yr[u][:, LANES:] + av2[u][two:])
            h_scr[u] = jnp.where(same_head, gh[u][:, LANES:] + kv[u], 0.0)

    def sweep_steps(g, state):
        def step(u):
            rows = pl.ds(pl.multiple_of((g * unroll + u) * n, n), n)
            both = mm(lin_scr[u], state[0])
            y_ref[rows, :] = both[:n] + yq_scr[u]
            state[0] = both[n:] + h_scr[u]
        return [functools.partial(step, u) for u in range(unroll)]

    ngroups = nc // unroll
    build_group(0)

    def body(g, st):
        state = [st]
        pending.extend(sweep_steps(g - 1, state))
        build_group(g)
        return state[0]

    st = lax.fori_loop(1, ngroups, body, jnp.zeros((two, two), F32))
    state = [st]
    for step in sweep_steps(ngroups - 1, state):
        step()


def _rwkv_scan(r, wl, k2, v, kk, a, bsz, seq):
    m = r.shape[0]
    npair = BRANCH_W // LANES
    blk = pl.BlockSpec((seq, LANES), lambda b, p: (b, p))
    u = RWKV_UNROLL
    return pl.pallas_call(
        functools.partial(_rwkv_scan_kernel, unroll=u),
        name="rwkv_scan",
        grid=(bsz, npair),
        in_specs=[blk] * 6,
        out_specs=blk,
        out_shape=jax.ShapeDtypeStruct((m, BRANCH_W), F32),
        scratch_shapes=[pltpu.VMEM((u, 3 * RWKV_L, LANES), F32), pltpu.VMEM((u, RWKV_L, LANES), F32),
                        pltpu.VMEM((u, LANES, LANES), F32)],
        compiler_params=_cparams(("parallel", "parallel")),
    )(r, wl, k2, v, kk, a)


def _rwkv_post_kernel(y_ref, bon_ref, sg_ref, g_ref, b_ref, bd_ref, o_ref):
    y = y_ref[...]
    bd = bd_ref[...]
    mu = _head_sum(y, bd) * (1.0 / RWKV_HD)
    yc = y - mu
    var = _head_sum(yc * yc, bd) * (1.0 / RWKV_HD)
    yn = yc * lax.rsqrt(var + RWKV_GN_EPS) * g_ref[...] + b_ref[...]
    o_ref[...] = ((yn + bon_ref[...].astype(F32)) * sg_ref[...].astype(F32)).astype(o_ref.dtype)


def _rwkv_post(y, bon, sg, g, b, bd):
    m, w = y.shape
    tb = RWKV_POST_TB
    rows = pl.BlockSpec((tb, w), lambda i: (i, 0))
    vec = pl.BlockSpec((1, w), lambda i: (0, 0))
    return pl.pallas_call(
        _rwkv_post_kernel,
        name="rwkv_post",
        grid=(m // tb,),
        in_specs=[rows, rows, rows, vec, vec, pl.BlockSpec((LANES, LANES), lambda i: (0, 0))],
        out_specs=rows,
        out_shape=jax.ShapeDtypeStruct((m, w), BF16),
        compiler_params=_cparams(("parallel",)),
    )(y, bon, sg, g.reshape(1, w), b.reshape(1, w), bd)


def _sgu_kernel(u_ref, v_ref, gate_ref, lng_ref, lnb_ref, ws_ref, bs_ref, o_ref):
    inv_sqrt2 = 0.7071067811865476

    def gelu(z):
        return 0.5 * z * (1.0 + lax.erf(z * inv_sqrt2))

    v = gelu(v_ref[...].astype(F32))
    mu = jnp.mean(v, axis=-1, keepdims=True)
    vc = v - mu
    var = jnp.mean(vc * vc, axis=-1, keepdims=True)
    vn = vc * lax.rsqrt(var + SGU_LN_EPS) * lng_ref[...] + lnb_ref[...]
    t = SGU_CHUNK
    ii = lax.broadcasted_iota(I32, (t, t), 0) // CHUNK
    jj = lax.broadcasted_iota(I32, (t, t), 1) // CHUNK
    mask = jj <= ii
    for g in range(SGU_GROUPS):
        ws = jnp.where(mask, ws_ref[g], 0.0)
        sl = slice(g * SGU_GW, (g + 1) * SGU_GW)
        bias = bs_ref[:, g:g + 1]
        for c in range(u_ref.shape[0] // t):
            rows = slice(c * t, (c + 1) * t)
            sv = _dot(ws, vn[rows, sl]) + bias
            u = gelu(u_ref[rows, sl].astype(F32))
            o_ref[rows, sl] = (u * sv * _silu(gate_ref[rows, sl].astype(F32))).astype(o_ref.dtype)


def _sgu(p, lng, lnb, ws, bs_t):
    m = p.shape[0]
    w = BRANCH_W
    tb = SGU_TB
    d0 = P_D0 // w
    return pl.pallas_call(
        _sgu_kernel,
        name="sgu",
        grid=(m // tb,),
        in_specs=[pl.BlockSpec((tb, w), lambda i: (i, d0)),
                  pl.BlockSpec((tb, w), lambda i: (i, d0 + 1)),
                  pl.BlockSpec((tb, w), lambda i: (i, d0 + 2)),
                  pl.BlockSpec((1, w), lambda i: (0, 0)),
                  pl.BlockSpec((1, w), lambda i: (0, 0)),
                  pl.BlockSpec((SGU_GROUPS, SGU_CHUNK, SGU_CHUNK), lambda i: (0, 0, 0)),
                  pl.BlockSpec((SGU_CHUNK, SGU_GROUPS), lambda i: (0, 0))],
        out_specs=pl.BlockSpec((tb, w), lambda i: (i, 0)),
        out_shape=jax.ShapeDtypeStruct((m, w), BF16),
        compiler_params=_cparams(("parallel",)),
    )(p, p, p, lng.reshape(1, w), lnb.reshape(1, w), ws, bs_t)


def _merge_kernel(ya_ref, yb_ref, yc_ref, yd_ref, ga_ref, gb_ref, gc_ref, gd_ref, wp_ref, o_ref):
    acc = None
    for g, (y_ref, g_ref) in enumerate(((ya_ref, ga_ref), (yb_ref, gb_ref), (yc_ref, gc_ref), (yd_ref, gd_ref))):
        term = jax.nn.sigmoid(g_ref[...].astype(F32)) * _dot(y_ref[...], wp_ref[g])
        acc = term if acc is None else acc + term
    o_ref[...] = acc.astype(o_ref.dtype)


def _merge(ya, yb, yc, yd, p, wproj):
    m = ya.shape[0]
    w = BRANCH_W
    tm, tn = MERGE_TM, MERGE_TN
    nj = D_MODEL // tn
    m0 = P_M0 // tn
    ysp = pl.BlockSpec((tm, w), lambda j, i: (i, 0))
    gsp = [pl.BlockSpec((tm, tn), lambda j, i, g=g: (i, m0 + g * nj + j)) for g in range(N_BRANCH)]
    return pl.pallas_call(
        _merge_kernel,
        name="merge",
        grid=(nj, m // tm),
        in_specs=[ysp, ysp, ysp, ysp, *gsp,
                  pl.BlockSpec((N_BRANCH, w, tn), lambda j, i: (0, 0, j))],
        out_specs=pl.BlockSpec((tm, tn), lambda j, i: (i, j)),
        out_shape=jax.ShapeDtypeStruct((m, D_MODEL), BF16),
        compiler_params=_cparams(("parallel", "parallel")),
    )(ya, yb, yc, yd, p, p, p, p, wproj)


def _out_kernel(mg_ref, wo_ref, g_ref, x_ref, o_ref):
    y = _dot(mg_ref[...], wo_ref[...])
    y = y * lax.rsqrt(jnp.mean(y * y, axis=-1, keepdims=True) + NORM_EPS) * g_ref[...]
    o_ref[...] = x_ref[...] + y


def _out_proj(merged, wout, g, x2):
    m, d = x2.shape
    tm = OUT_TM
    return pl.pallas_call(
        _out_kernel,
        name="out_proj",
        grid=(m // tm,),
        in_specs=[pl.BlockSpec((tm, d), lambda i: (i, 0)),
                  pl.BlockSpec((d, d), lambda i: (0, 0)),
                  pl.BlockSpec((1, d), lambda i: (0, 0)),
                  pl.BlockSpec((tm, d), lambda i: (i, 0))],
        out_specs=pl.BlockSpec((tm, d), lambda i: (i, 0)),
        out_shape=jax.ShapeDtypeStruct((m, d), F32),
        compiler_params=_cparams(("parallel",)),
    )(merged, wout, g.reshape(1, d), x2)


def _split_cols(w, sizes):
    offs = np.cumsum((0,) + tuple(sizes))
    return [w[:, int(offs[i]):int(offs[i + 1])] for i in range(len(sizes))]


def _pad_cols(w, n):
    return jnp.pad(w, ((0, 0), (0, n - w.shape[1])))


def _pad_rows(w, n):
    return jnp.pad(w, ((0, n - w.shape[0]), (0, 0)))


def _rope_tables(positions, rot_dim, width):
    inv = ROPE_THETA ** (-jnp.arange(0, rot_dim, 2, dtype=F32) / rot_dim)
    ang = positions.astype(F32).reshape(-1, 1) * inv
    cos, sin = jnp.cos(ang), jnp.sin(ang)
    m = cos.shape[0]
    rest = width - rot_dim
    cos_h = jnp.concatenate([cos, cos, jnp.ones((m, rest), F32)], axis=1)
    sin_h = jnp.concatenate([-sin, sin, jnp.zeros((m, rest), F32)], axis=1)
    reps = LANES // width
    return jnp.tile(cos_h, (1, reps)), jnp.tile(sin_h, (1, reps))


def kernel(x, positions, norm_pre, norm_post, w_in, gla_w_g2, gla_b_g, gla_head_g, rwkv_mu, rwkv_w0,
           rwkv_w_w2, rwkv_a0, rwkv_w_a2, rwkv_k_k, rwkv_k_a, rwkv_r_k, rwkv_lnx_g, rwkv_lnx_b,
           sgu_ln_g, sgu_ln_b, sgu_w_s, sgu_b_s, w_proj, w_out):
    bsz, seq, d = x.shape
    depth = w_in.shape[0]
    m = bsz * seq
    x2 = x.reshape(m, d)
    ca, sa = _rope_tables(positions, DSA_ROT, DSA_HD)
    ci, si = _rope_tables(positions, IDX_ROT, IDX_HD)
    lane_grp = np.arange(LANES) // RWKV_HD
    bd = jnp.asarray((lane_grp[:, None] == lane_grp[None, :]).astype(np.float32))

    for l in range(depth):
        wa, wb, wc, wd, wm = _split_cols(w_in[l], GROUP_COLS)
        a_q, a_k, a_v, a_glr, a_gate = _split_cols(wa, A_COLS)
        b_q, b_k, b_v, b_iq, b_ik, b_iw, b_gate = _split_cols(wb, B_COLS)
        c_r, c_k, c_v, c_wlr, c_alr, c_gate = _split_cols(wc, C_COLS)
        w_main = jnp.concatenate([wm, b_q, b_k, b_v, b_gate, c_r, c_k, c_v, c_gate, wd,
                                  a_q, a_k, a_v, a_gate], axis=1).astype(BF16)
        w_small = jnp.concatenate([b_iq] + [_pad_cols(t, LANES) for t in (b_ik, b_iw, c_wlr, c_alr, a_glr)],
                                  axis=1).astype(BF16)

        h = _rmsnorm_cast(x2, norm_pre[l])
        p = _matmul(h, w_main, MM_TN, BF16)
        ps = _matmul(h, w_small, S_COLS, F32)

        ya = _gla(p, ps, _pad_rows(gla_w_g2[l], LANES), gla_b_g[l], gla_head_g[l], bsz, seq)

        qr, kr, vt, iqr, ikr, iws = _dsa_prep(p, ps, ca, sa, ci, si)
        yb = _dsa(qr, kr, vt, iqr, ikr, iws, p, bsz, seq)

        mu_r, mu_k, mu_v, mu_wlr, mu_alr, mu_gate = _split_cols(rwkv_mu[l][None, :], C_COLS)
        mu_main = jnp.concatenate([mu_r, mu_k, mu_v, mu_gate], axis=1)
        mu_s = jnp.concatenate([_pad_cols(mu_wlr, LANES), _pad_cols(mu_alr, LANES)], axis=1)
        r, wl, k2, v, kk, a, sg, bon = _rwkv_prep(
            p, ps, mu_main, mu_s, rwkv_w0[l], _pad_rows(rwkv_w_w2[l], LANES), rwkv_a0[l],
            _pad_rows(rwkv_w_a2[l], LANES), rwkv_k_k[l], rwkv_k_a[l], rwkv_r_k[l].reshape(-1), bd, seq)
        y = _rwkv_scan(r, wl, k2, v, kk, a, bsz, seq)
        yc = _rwkv_post(y, bon, sg, rwkv_lnx_g[l], rwkv_lnx_b[l], bd)

        yd = _sgu(p, sgu_ln_g[l], sgu_ln_b[l], sgu_w_s[l], sgu_b_s[l].T)

        merged = _merge(ya, yb, yc, yd, p, w_proj[l].astype(BF16))
        x2 = _out_proj(merged, w_out[l].astype(BF16), norm_post[l], x2)
    return x2.reshape(bsz, seq, d)
```

```python
import functools

import numpy as np
import jax
import jax.numpy as jnp
from jax import lax
from jax.experimental import pallas as pl
from jax.experimental.pallas import tpu as pltpu

F32 = jnp.float32
BF16 = jnp.bfloat16
I32 = jnp.int32
HIGHEST = lax.Precision.HIGHEST

D_MODEL = 2048
CHUNK = 64
NORM_EPS = 1e-6
ROPE_THETA = 500000.0
N_BRANCH = 4
BRANCH_W = D_MODEL // 2

GLA_HEADS = 4
GLA_DK = BRANCH_W // 2 // GLA_HEADS
GLA_DV = BRANCH_W // GLA_HEADS
GLA_RANK = 16
GLA_TAU = 16.0

DSA_HEADS = 8
DSA_HD = BRANCH_W // DSA_HEADS
DSA_ROT = DSA_HD // 4
IDX_HEADS = 8
IDX_HD = 64
IDX_ROT = IDX_HD // 4
TOPK_MAX = 256

RWKV_HD = 64
RWKV_HEADS = BRANCH_W // RWKV_HD
RWKV_DECAY_RANK = 96
RWKV_A_RANK = 96
RWKV_DECAY_SCALE = 0.606531
RWKV_GN_EPS = 64e-5
RWKV_KK_EPS = 1e-12

SGU_CHUNK = 128
SGU_GROUPS = 8
SGU_GW = BRANCH_W // SGU_GROUPS
SGU_LN_EPS = 1e-5

A_COLS = (GLA_HEADS * GLA_DK, GLA_HEADS * GLA_DK, BRANCH_W, GLA_RANK, BRANCH_W)
B_COLS = (BRANCH_W, BRANCH_W, BRANCH_W, IDX_HEADS * IDX_HD, IDX_HD, IDX_HEADS, BRANCH_W)
C_COLS = (BRANCH_W, BRANCH_W, BRANCH_W, RWKV_DECAY_RANK, RWKV_A_RANK, BRANCH_W)
D_COLS = (2 * BRANCH_W, BRANCH_W)
GROUP_COLS = (sum(A_COLS), sum(B_COLS), sum(C_COLS), sum(D_COLS), N_BRANCH * D_MODEL)

LANES = 128
VMEM_LIMIT = 56 * 1024 * 1024
MXU_DEPTH = 256
IDX_K = MXU_DEPTH
NEG_BIG = -1e30
LOG2E = 1.4426950408889634
INT_MIN = -(2 ** 31)

P_M0 = 0
P_B0 = P_M0 + N_BRANCH * D_MODEL
P_C0 = P_B0 + 4 * BRANCH_W
P_D0 = P_C0 + 4 * BRANCH_W
P_A0 = P_D0 + 3 * BRANCH_W
P_COLS = P_A0 + 2 * GLA_HEADS * GLA_DK + 2 * BRANCH_W
S_IDX0 = 0
S_IDX_COLS = IDX_HEADS * IDX_HD + 2 * 128
S_RW0 = S_IDX0 + S_IDX_COLS
S_GLR0 = S_RW0 + 2 * 128
S_COLS = S_GLR0 + 128

NORM_TM = 512
MM_TM = 1024
MM_TN = 1024
GLA_TB = 512
DSA_PREP_TB = 512
DSA_TQ = 256
DSA_KB = 512
DSA_VROWS = DSA_HD + 16
RWKV_PREP_TB = 256
RWKV_L = 64
RWKV_UNROLL = 8
RWKV_POST_TB = 512
SGU_TB = 512
MERGE_TM = 512
MERGE_TN = 1024
OUT_TM = 512


def _cparams(sem):
    return pltpu.CompilerParams(dimension_semantics=sem, vmem_limit_bytes=VMEM_LIMIT)


def _dot(a, b, precision=None):
    return jnp.dot(a, b, preferred_element_type=F32, precision=precision)


def _dot_nt(a, b, precision=None):
    return lax.dot_general(a, b, (((1,), (1,)), ((), ())), preferred_element_type=F32,
                           precision=precision)


def _dot_tn(a, b, precision=None):
    return lax.dot_general(a, b, (((0,), (0,)), ((), ())), preferred_element_type=F32,
                           precision=precision)


def _bf16_terms(x, n):
    terms = []
    for _ in range(n):
        t = x.astype(BF16)
        terms.append(t)
        x = x - t.astype(F32)
    return terms


def _dot_exact_lhs(a, b, nterms):
    a16 = a.astype(BF16)
    out = None
    for t in _bf16_terms(b, nterms):
        d = _dot(a16, t)
        out = d if out is None else out + d
    return out


def _dot_exact_rhs(a, b, nterms):
    b16 = b.astype(BF16)
    out = None
    for t in _bf16_terms(a, nterms):
        d = _dot(t, b16)
        out = d if out is None else out + d
    return out


def _dot_3pass(a, b):
    a_hi, a_lo = _bf16_terms(a, 2)
    b_hi, b_lo = _bf16_terms(b, 2)
    return _dot(a_hi, b_hi) + _dot(a_lo, b_hi) + _dot(a_hi, b_lo)


def _silu(x):
    return x * jax.nn.sigmoid(x)


def _rmsnorm_cast_kernel(x_ref, g_ref, o_ref):
    x = x_ref[...]
    ms = jnp.mean(x * x, axis=-1, keepdims=True)
    o_ref[...] = (x * lax.rsqrt(ms + NORM_EPS) * g_ref[...]).astype(o_ref.dtype)


def _rmsnorm_cast(x2, g):
    m, d = x2.shape
    return pl.pallas_call(
        _rmsnorm_cast_kernel,
        name="prenorm",
        grid=(m // NORM_TM,),
        in_specs=[pl.BlockSpec((NORM_TM, d), lambda i: (i, 0)),
                  pl.BlockSpec((1, d), lambda i: (0, 0))],
        out_specs=pl.BlockSpec((NORM_TM, d), lambda i: (i, 0)),
        out_shape=jax.ShapeDtypeStruct((m, d), BF16),
        compiler_params=_cparams(("parallel",)),
    )(x2, g.reshape(1, d))


def _matmul_kernel(a_ref, b_ref, o_ref):
    o_ref[...] = _dot(a_ref[...], b_ref[...]).astype(o_ref.dtype)


def _matmul(a, b, tn, out_dtype=F32):
    m, k = a.shape
    n = b.shape[1]
    tm = min(MM_TM, m)
    return pl.pallas_call(
        _matmul_kernel,
        name="proj",
        grid=(m // tm, n // tn),
        in_specs=[pl.BlockSpec((tm, k), lambda i, j: (i, 0)),
                  pl.BlockSpec((k, tn), lambda i, j: (0, j))],
        out_specs=pl.BlockSpec((tm, tn), lambda i, j: (i, j)),
        out_shape=jax.ShapeDtypeStruct((m, n), out_dtype),
        compiler_params=_cparams(("parallel", "parallel")),
    )(a, b)


def _gla_kernel(q_ref, k_ref, v_ref, gate_ref, glr_ref, wg2_ref, bg_ref, hg_ref, o_ref, state_ref):
    @pl.when(pl.program_id(2) == 0)
    def _():
        state_ref[...] = jnp.zeros_like(state_ref)

    ii = lax.broadcasted_iota(I32, (CHUNK, CHUNK), 0)
    jj = lax.broadcasted_iota(I32, (CHUNK, CHUNK), 1)
    tril = (ii >= jj).astype(F32)
    chunks = [pl.ds(c * CHUNK, CHUNK) for c in range(GLA_TB // CHUNK)]
    z = _dot_3pass(glr_ref[...], wg2_ref[...]) + bg_ref[...]
    log_a = (jnp.minimum(z, 0.0) - jnp.log1p(jnp.exp(-jnp.abs(z)))) / GLA_TAU
    cum = [_dot_exact_lhs(tril, log_a[c * CHUNK:(c + 1) * CHUNK], 3) for c in range(len(chunks))]
    total = [x[CHUNK - 1:CHUNK, :] for x in cum]
    k_dec = [k_ref[sl, :].astype(F32) * jnp.exp(t - x) for sl, t, x in zip(chunks, total, cum)]
    kv = [_dot_tn(v_ref[sl, :], kd.astype(BF16)) for sl, kd in zip(chunks, k_dec)]
    states = []
    st = state_ref[...]
    for t, x in zip(total, kv):
        st = st * jnp.exp(t) + x
        states.append(st)
    state_ref[...] = st
    outs = [_dot_nt((q_ref[sl, :].astype(F32) * (GLA_DK ** -0.5)).astype(BF16), s.astype(BF16))
            for sl, s in zip(chunks, states)]
    for sl, o in zip(chunks, outs):
        o = o * lax.rsqrt(jnp.mean(o * o, axis=-1, keepdims=True) + NORM_EPS) * hg_ref[...]
        o_ref[sl, :] = (o * _silu(gate_ref[sl, :].astype(F32))).astype(o_ref.dtype)


def _gla(p, ps, wg2p, bg, hg, bsz, seq):
    m = p.shape[0]
    nt = seq // GLA_TB
    row = lambda b, h, t: b * nt + t
    qb = P_A0 // GLA_DK
    kb = qb + GLA_HEADS
    vb = (P_A0 + 2 * GLA_HEADS * GLA_DK) // GLA_DV
    gb = vb + GLA_HEADS
    lb = S_GLR0 // LANES
    return pl.pallas_call(
        _gla_kernel,
        name="gla",
        grid=(bsz, GLA_HEADS, nt),
        in_specs=[
            pl.BlockSpec((GLA_TB, GLA_DK), lambda b, h, t: (row(b, h, t), qb + h)),
            pl.BlockSpec((GLA_TB, GLA_DK), lambda b, h, t: (row(b, h, t), kb + h)),
            pl.BlockSpec((GLA_TB, GLA_DV), lambda b, h, t: (row(b, h, t), vb + h)),
            pl.BlockSpec((GLA_TB, GLA_DV), lambda b, h, t: (row(b, h, t), gb + h)),
            pl.BlockSpec((GLA_TB, LANES), lambda b, h, t: (row(b, h, t), lb)),
            pl.BlockSpec((LANES, GLA_DK), lambda b, h, t: (0, h)),
            pl.BlockSpec((1, GLA_DK), lambda b, h, t: (0, h)),
            pl.BlockSpec((1, GLA_DV), lambda b, h, t: (0, 0)),
        ],
        out_specs=pl.BlockSpec((GLA_TB, GLA_DV), lambda b, h, t: (row(b, h, t), h)),
        out_shape=jax.ShapeDtypeStruct((m, BRANCH_W), BF16),
        scratch_shapes=[pltpu.VMEM((GLA_DV, GLA_DK), F32)],
        compiler_params=_cparams(("parallel", "parallel", "arbitrary")),
    )(p, p, p, p, ps, wg2p, bg.reshape(1, -1), hg.reshape(1, -1))


def _rope_tile(x, cos_t, sin_t, half, width):
    lane = lax.broadcasted_iota(I32, x.shape, 1) % width
    partner = jnp.where(lane < half, pltpu.roll(x, LANES - half, 1), pltpu.roll(x, half, 1))
    return x * cos_t + partner * sin_t


def _dsa_prep_kernel(q_ref, k_ref, v_ref, idx_ref, ca_ref, sa_ref, ci_ref, si_ref,
                     qo_ref, ko_ref, vo_ref, iqo_ref, iko_ref, iwo_ref):
    ca, sa = ca_ref[...], sa_ref[...]
    ci, si = ci_ref[...], si_ref[...]
    for h in range(DSA_HEADS):
        sl = slice(h * DSA_HD, (h + 1) * DSA_HD)
        q = _rope_tile(q_ref[:, sl].astype(F32), ca, sa, DSA_ROT // 2, DSA_HD) * (DSA_HD ** -0.5 * LOG2E)
        qo_ref[:, sl] = q.astype(qo_ref.dtype)
        ko_ref[:, sl] = _rope_tile(k_ref[:, sl].astype(F32), ca, sa, DSA_ROT // 2, DSA_HD).astype(ko_ref.dtype)
        vo_ref[0, h * DSA_VROWS:h * DSA_VROWS + DSA_HD, :] = v_ref[:, sl].astype(F32).T.astype(vo_ref.dtype)
        extra = lax.broadcasted_iota(I32, (DSA_VROWS - DSA_HD, q_ref.shape[0]), 0) == 0
        vo_ref[0, h * DSA_VROWS + DSA_HD:(h + 1) * DSA_VROWS, :] = jnp.where(extra, 1.0, 0.0).astype(vo_ref.dtype)
    niq = IDX_HEADS * IDX_HD
    low = lax.broadcasted_iota(I32, (q_ref.shape[0], LANES), 1) < IDX_HD

    def hi_lo(x):
        hi = x.astype(BF16).astype(F32)
        return hi, x - hi

    for j in range(niq // LANES):
        x = _rope_tile(idx_ref[:, j * LANES:(j + 1) * LANES], ci, si, IDX_ROT // 2, IDX_HD)
        hi, lo = hi_lo(x)
        hi_sw = pltpu.roll(hi, IDX_HD, 1)
        lo_sw = pltpu.roll(lo, IDX_HD, 1)
        base = 2 * j * IDX_K
        iqo_ref[:, base:base + LANES] = jnp.where(low, hi, hi_sw).astype(BF16)
        iqo_ref[:, base + LANES:base + IDX_K] = jnp.where(low, lo, 0.0).astype(BF16)
        iqo_ref[:, base + IDX_K:base + IDX_K + LANES] = jnp.where(low, hi_sw, hi).astype(BF16)
        iqo_ref[:, base + IDX_K + LANES:base + 2 * IDX_K] = jnp.where(low, lo_sw, 0.0).astype(BF16)
    hi, lo = hi_lo(_rope_tile(idx_ref[:, niq:niq + LANES], ci, si, IDX_ROT // 2, IDX_HD))
    iko_ref[:, :LANES] = jnp.where(low, hi, pltpu.roll(lo, IDX_HD, 1)).astype(BF16)
    iko_ref[:, LANES:] = jnp.where(low, hi, 0.0).astype(BF16)
    iwo_ref[...] = idx_ref[:, niq + LANES:niq + 2 * LANES] * (niq ** -0.5)


def _dsa_prep(p, ps, ca, sa, ci, si):
    m = p.shape[0]
    tb = DSA_KB
    w = BRANCH_W
    rows = lambda i: (i, 0)
    b0 = P_B0 // w
    return pl.pallas_call(
        _dsa_prep_kernel,
        name="dsa_prep",
        grid=(m // tb,),
        in_specs=[pl.BlockSpec((tb, w), lambda i: (i, b0)),
                  pl.BlockSpec((tb, w), lambda i: (i, b0 + 1)),
                  pl.BlockSpec((tb, w), lambda i: (i, b0 + 2)),
                  pl.BlockSpec((tb, S_IDX_COLS), lambda i: (i, S_IDX0 // S_IDX_COLS)),
                  pl.BlockSpec((tb, LANES), rows), pl.BlockSpec((tb, LANES), rows),
                  pl.BlockSpec((tb, LANES), rows), pl.BlockSpec((tb, LANES), rows)],
        out_specs=[pl.BlockSpec((tb, w), rows), pl.BlockSpec((tb, w), rows),
                   pl.BlockSpec((1, DSA_HEADS * DSA_VROWS, tb), lambda i: (i, 0, 0)),
                   pl.BlockSpec((tb, IDX_HEADS * IDX_K), rows), pl.BlockSpec((tb, IDX_K), rows),
                   pl.BlockSpec((tb, LANES), rows)],
        out_shape=[jax.ShapeDtypeStruct((m, w), BF16), jax.ShapeDtypeStruct((m, w), BF16),
                   jax.ShapeDtypeStruct((m // tb, DSA_HEADS * DSA_VROWS, tb), BF16),
                   jax.ShapeDtypeStruct((m, IDX_HEADS * IDX_K), BF16),
                   jax.ShapeDtypeStruct((m, IDX_K), BF16), jax.ShapeDtypeStruct((m, LANES), F32)],
        compiler_params=_cparams(("parallel",)),
    )(p, p, p, ps, ca, sa, ci, si)


def _dsa_kernel(q_ref, k_ref, vt_ref, iq_ref, ik_ref, iw_ref, gate_ref, o_ref, key_scr, hi_scr, tie_scr,
                *, topk):
    tq, kb_w = DSA_TQ, DSA_KB
    q0 = pl.program_id(1) * tq
    nkb = (q0 + tq + kb_w - 1) // kb_w
    qcol = lax.broadcasted_iota(I32, (1, tq), 1)
    qlim = ((q0 + qcol) // CHUNK + 1) * CHUNK
    sub_k = lax.broadcasted_iota(I32, (kb_w, 1), 0)
    sub = 8

    def score_body(kb, carry):
        ik = ik_ref[pl.ds(pl.multiple_of(kb * kb_w, kb_w), kb_w), :]
        dots = [_dot_nt(ik, iq_ref[:, h * IDX_K:(h + 1) * IDX_K]) for h in range(IDX_HEADS)]
        acc = jnp.zeros((kb_w, tq), F32)
        for h, d in enumerate(dots):
            acc = acc + iw_ref[h:h + 1, :] * jnp.maximum(d, 0.0)
        acc = acc + 0.0
        bits = lax.bitcast_convert_type(acc, I32)
        key = bits ^ ((bits >> 31) & 0x7FFFFFFF)
        sidx = kb * kb_w + sub_k
        key = jnp.where(sidx < qlim, key, INT_MIN)
        key_scr[kb] = key
        hi_scr[kb] = (key >> 16).astype(jnp.int16)
        return carry

    lax.fori_loop(0, nkb, score_body, 0)

    def count_tiles(load, pred, rows_per_vreg, dtype):
        def body(kb, c):
            m = pred(load(kb), kb * kb_w + sub_k).astype(dtype)
            if dtype == I32:
                nparts = 8
                rows_part = kb_w // nparts
                parts = [jnp.sum(m[j * rows_part:(j + 1) * rows_part]
                                 .reshape(rows_part // rows_per_vreg, rows_per_vreg, tq), axis=0)
                         for j in range(nparts)]
            else:
                parts = [m[j * rows_per_vreg:(j + 1) * rows_per_vreg] for j in range(kb_w // rows_per_vreg)]
            while len(parts) > 1:
                parts = [a + b for a, b in zip(parts[::2], parts[1::2])]
            return c + parts[0]
        c = lax.fori_loop(0, nkb, body, jnp.zeros((rows_per_vreg, tq), dtype))
        return jnp.sum(c.astype(I32), axis=0, keepdims=True)

    def count(pred):
        return count_tiles(lambda kb: key_scr[kb], pred, sub, I32)

    def count_hi(pred):
        return count_tiles(lambda kb: hi_scr[kb], pred, 2 * sub, jnp.int16)

    def bit_body(it, carry, high_only):
        res, cnt_res = carry
        trial = res | lax.shift_left(jnp.int32(1), 31 - it)
        cand = trial ^ INT_MIN
        if high_only:
            cand_hi = (cand >> 16).astype(jnp.int16)
            cnt = count_hi(lambda kt, s: kt >= cand_hi)
        else:
            cnt = count(lambda kt, s: kt >= cand)
        ok = cnt >= topk
        return jnp.where(ok, trial, res), jnp.where(ok, cnt, cnt_res)

    carry = (jnp.zeros((1, tq), I32), jnp.full((1, tq), nkb * kb_w, I32))
    carry = lax.fori_loop(0, 16, functools.partial(bit_body, high_only=True), carry)
    res, cnt_ge = lax.fori_loop(16, 32, functools.partial(bit_body, high_only=False), carry)
    thr = res ^ INT_MIN
    seq_len = key_scr.shape[0] * kb_w
    tie_scr[...] = jnp.full(tie_scr.shape, seq_len, I32)
    has_tie = (cnt_ge > topk) & (thr != INT_MIN)

    @pl.when(jnp.max(jnp.where(has_tie, 1.0, 0.0)) > 0.5)
    def _():
        need = topk - count(lambda kt, s: kt > thr)
        nbits = max(1, int(np.ceil(np.log2(seq_len))))

        def tie_body(it, resj):
            trial = resj | lax.shift_left(jnp.int32(1), nbits - 1 - it)
            c = count(lambda kt, s: (kt == thr) & (s < trial))
            return jnp.where(c < need, trial, resj)

        tie_scr[...] = lax.fori_loop(0, nbits, tie_body, jnp.zeros((1, tq), I32))

    last_tie = tie_scr[...]

    def bias_body(kb, carry):
        kt = key_scr[kb]
        sidx = kb * kb_w + sub_k
        sel = ((kt > thr) | ((kt == thr) & (sidx <= last_tie))) & (kt != INT_MIN)
        key_scr[kb] = lax.bitcast_convert_type(jnp.where(sel, 0.0, NEG_BIG), I32)
        return carry

    lax.fori_loop(0, nkb, bias_body, 0)

    heads = [slice(h * DSA_HD, (h + 1) * DSA_HD) for h in range(DSA_HEADS)]

    def att_body(kb, stats):
        rows = pl.ds(pl.multiple_of(kb * kb_w, kb_w), kb_w)
        bias = lax.bitcast_convert_type(key_scr[kb], F32).astype(BF16)
        s = [_dot_nt(k_ref[rows, sl], q_ref[:, sl]).astype(BF16) + bias for sl in heads]
        m_n = [jnp.maximum(c[0], jnp.max(x, axis=0, keepdims=True).astype(F32)) for c, x in zip(stats, s)]
        p = [jnp.exp2(x - m.astype(BF16)) for x, m in zip(s, m_n)]
        pv = [_dot(vt_ref[kb, h * DSA_VROWS:(h + 1) * DSA_VROWS, :], x) for h, x in enumerate(p)]
        alpha = [jnp.exp2(c[0] - m) for c, m in zip(stats, m_n)]
        return tuple((m, a * c[1] + y) for m, a, c, y in zip(m_n, alpha, stats, pv))

    init = tuple((jnp.full((1, tq), NEG_BIG, F32), jnp.zeros((DSA_VROWS, tq), F32)) for _ in heads)
    final = lax.fori_loop(0, nkb, att_body, init)
    for sl, (_, acc_f) in zip(heads, final):
        o = (acc_f[:DSA_HD] / acc_f[DSA_HD:DSA_HD + 1]).T
        o_ref[:, sl] = (o * _silu(gate_ref[:, sl].astype(F32))).astype(o_ref.dtype)


def _dsa(qr, kr, vt, iqr, ikr, iws, p, bsz, seq):
    m = qr.shape[0]
    w = BRANCH_W
    nq = seq // DSA_TQ
    nkb = seq // DSA_KB
    topk = min(TOPK_MAX, seq // 4)
    qrow = lambda b, i: (b * nq + i, 0)
    return pl.pallas_call(
        functools.partial(_dsa_kernel, topk=topk),
        name="dsa",
        grid=(bsz, nq),
        in_specs=[pl.BlockSpec((DSA_TQ, w), qrow),
                  pl.BlockSpec((seq, w), lambda b, i: (b, 0), pipeline_mode=pl.Buffered(1)),
                  pl.BlockSpec((nkb, DSA_HEADS * DSA_VROWS, DSA_KB), lambda b, i: (b, 0, 0),
                               pipeline_mode=pl.Buffered(1)),
                  pl.BlockSpec((DSA_TQ, IDX_HEADS * IDX_K), qrow),
                  pl.BlockSpec((seq, IDX_K), lambda b, i: (b, 0), pipeline_mode=pl.Buffered(1)),
                  pl.BlockSpec((IDX_HEADS, DSA_TQ), lambda b, i: (0, b * nq + i)),
                  pl.BlockSpec((DSA_TQ, w), lambda b, i: (b * nq + i, P_B0 // w + 3))],
        out_specs=pl.BlockSpec((DSA_TQ, w), qrow),
        out_shape=jax.ShapeDtypeStruct((m, w), BF16),
        scratch_shapes=[pltpu.VMEM((nkb, DSA_KB, DSA_TQ), I32),
                        pltpu.VMEM((nkb, DSA_KB, DSA_TQ), jnp.int16),
                        pltpu.VMEM((1, DSA_TQ), I32)],
        compiler_params=_cparams(("parallel", "arbitrary")),
    )(qr, kr, vt, iqr, ikr, iws[:, :IDX_HEADS].T, p)


def _head_sum(x, bd):
    parts = [_dot_exact_rhs(x[:, j * LANES:(j + 1) * LANES], bd, 2) for j in range(x.shape[1] // LANES)]
    return jnp.concatenate(parts, axis=1)


def _rwkv_prep_kernel(c_ref, cs_ref, pc_ref, pcs_ref, mu_ref, mus_ref, w0_ref, ww2_ref, a0_ref, wa2_ref,
                      kkw_ref, ka_ref, rk_ref, bd_ref,
                      r_o, wl_o, k_o, v_o, kk_o, a_o, sg_o, bon_o, *, blocks_per_seq):
    tb = c_ref.shape[0]
    first = (pl.program_id(0) % blocks_per_seq) == 0
    rowi = lax.broadcasted_iota(I32, (tb, 1), 0)

    def shift_lerp(c, prev_rows, mu):
        last = prev_rows.shape[0] - 1
        prev = jnp.where(first, 0.0, prev_rows[last:last + 1, :])
        sh = jnp.where(rowi == 0, prev, pltpu.roll(c, 1, 0))
        return c + (sh - c) * mu

    w = BRANCH_W
    c = shift_lerp(c_ref[...].astype(F32), pc_ref[...].astype(F32), mu_ref[...])
    cs = shift_lerp(cs_ref[...], pcs_ref[...], mus_ref[...])
    r, k, v, gate = c[:, :w], c[:, w:2 * w], c[:, 2 * w:3 * w], c[:, 3 * w:]
    w_log = -RWKV_DECAY_SCALE * jax.nn.sigmoid(w0_ref[...] + _dot_3pass(jnp.tanh(cs[:, :LANES]), ww2_ref[...]))
    a = jax.nn.sigmoid(a0_ref[...] + _dot_3pass(cs[:, LANES:], wa2_ref[...]))
    kk = k * kkw_ref[...]
    k2 = k * (1.0 + (a - 1.0) * ka_ref[...])
    bd = bd_ref[...]
    kk = kk * lax.rsqrt(_head_sum(kk * kk, bd) + RWKV_KK_EPS)
    r_o[...] = r.astype(r_o.dtype)
    wl_o[...] = w_log
    k_o[...] = k2.astype(k_o.dtype)
    v_o[...] = v.astype(v_o.dtype)
    kk_o[...] = kk.astype(kk_o.dtype)
    a_o[...] = a.astype(a_o.dtype)
    sg_o[...] = _silu(gate).astype(sg_o.dtype)
    bon_o[...] = (_head_sum(r * k2 * rk_ref[...], bd) * v).astype(bon_o.dtype)


def _rwkv_prep(p, ps, mu_main, mu_s, w0, ww2p, a0, wa2p, kkw, ka, rk, bd, seq):
    m = p.shape[0]
    tb = RWKV_PREP_TB
    w = BRANCH_W
    pr16, pr8 = 16, 8
    cblk, sblk = P_C0 // (4 * w), S_RW0 // (2 * LANES)
    const = lambda i: (0, 0)
    vec = pl.BlockSpec((1, w), const)
    out = pl.BlockSpec((tb, w), lambda i: (i, 0))
    dts = [BF16, F32, BF16, BF16, BF16, BF16, BF16, BF16]
    return pl.pallas_call(
        functools.partial(_rwkv_prep_kernel, blocks_per_seq=seq // tb),
        name="rwkv_prep",
        grid=(m // tb,),
        in_specs=[pl.BlockSpec((tb, 4 * w), lambda i: (i, cblk)),
                  pl.BlockSpec((tb, 2 * LANES), lambda i: (i, sblk)),
                  pl.BlockSpec((pr16, 4 * w), lambda i: (jnp.maximum(i * (tb // pr16) - 1, 0), cblk)),
                  pl.BlockSpec((pr8, 2 * LANES), lambda i: (jnp.maximum(i * (tb // pr8) - 1, 0), sblk)),
                  pl.BlockSpec((1, 4 * w), const), pl.BlockSpec((1, 2 * LANES), const),
                  vec, pl.BlockSpec((LANES, w), const), vec, pl.BlockSpec((LANES, w), const),
                  vec, vec, vec, pl.BlockSpec((LANES, LANES), const)],
        out_specs=[out] * 8,
        out_shape=[jax.ShapeDtypeStruct((m, w), dt) for dt in dts],
        compiler_params=_cparams(("parallel",)),
    )(p, ps, p, ps, mu_main, mu_s, w0.reshape(1, w), ww2p, a0.reshape(1, w), wa2p,
      kkw.reshape(1, w), ka.reshape(1, w), rk.reshape(1, w), bd)


def _rwkv_scan_kernel(r_ref, wl_ref, k_ref, v_ref, kk_ref, a_ref, y_ref, lin_scr, yq_scr, h_scr, *, unroll, nseq):
    n = RWKV_L
    two = 2 * n
    nc = r_ref.shape[0] // n
    i2 = lax.broadcasted_iota(I32, (two, two), 0)
    j2 = lax.broadcasted_iota(I32, (two, two), 1)
    strict = i2 > j2
    incl = i2 >= j2
    same16 = (i2 // 16) == (j2 // 16)
    same_head = (i2 // RWKV_HD) == (j2 // RWKV_HD)
    eye = (i2 == j2).astype(F32)
    ii = lax.broadcasted_iota(I32, (n, n), 0)
    jj = lax.broadcasted_iota(I32, (n, n), 1)
    tril = (ii >= jj).astype(F32)
    head0 = lax.broadcasted_iota(I32, (n, LANES), 1) < RWKV_HD

    def mm(a, b):
        return _dot(a.astype(BF16), b.astype(BF16))

    def mm_nt(a, b):
        return _dot_nt(a.astype(BF16), b.astype(BF16))

    def mm_tn(a, b):
        return _dot_tn(a.astype(BF16), b.astype(BF16))

    def stack(x):
        return jnp.concatenate([jnp.where(head0, x, 0.0), jnp.where(head0, 0.0, x)], axis=0)

    def fold(x):
        return x[:n] + x[n:]

    def each(f, *lists):
        return [f(*args) for args in zip(*lists)]

    def tri_inv(a_low):
        a_d = each(lambda a: jnp.where(same16, a, 0.0), a_low)
        a_o = each(lambda a, d: a - d, a_low, a_d)
        a2 = each(mm, a_d, a_d)
        tick()
        a4 = each(mm, a2, a2)
        t_lo = each(lambda d, s: mm(eye + d, eye + s), a_d, a2)
        tick()
        a8 = each(mm, a4, a4)
        tick()
        t_hi = each(lambda x, y: mm(eye + x, eye + y), a4, a8)
        tick()
        t_d = each(mm, t_lo, t_hi)
        tick()
        nn = each(mm, t_d, a_o)
        tick()
        n2 = each(mm, nn, nn)
        tick()
        t_o = each(lambda x, y: mm(eye + x, eye + y), nn, n2)
        tick()
        return each(mm, t_o, t_d)

    pending = []

    def tick():
        if pending:
            pending.pop(0)()

    def build_group(g):
        rows = [pl.ds(pl.multiple_of((g * unroll + u) * n, n), n) for u in range(unroll)]
        wl = [wl_ref[r, :] for r in rows]
        k = [k_ref[r, :].astype(F32) for r in rows]
        v = [v_ref[r, :].astype(F32) for r in rows]
        kk = [kk_ref[r, :].astype(F32) for r in rows]
        beta = each(lambda x, r: x * a_ref[r, :].astype(F32), kk, rows)
        cum = each(lambda x: _dot_exact_lhs(tril, x, 3), wl)
        cum_end = each(lambda x: x[n - 1:n, :], cum)
        g_inv = each(lambda x: jnp.exp(-x), cum)
        g_end = each(lambda e, x: jnp.exp(e - x), cum_end, cum)
        a_st = each(lambda x, c, w: stack(-x * jnp.exp(c - w)), kk, cum, wl)
        r_st = each(lambda r, c: stack(r_ref[r, :].astype(F32) * jnp.exp(c)), rows, cum)
        b_st = each(lambda x, y: stack(x * y), beta, g_inv)
        k_st = each(lambda x, y: stack(x * y), k, g_inv)
        v_st = each(stack, v)
        ar = each(lambda x, y: jnp.concatenate([x, y], axis=0), a_st, r_st)
        gb = each(mm_nt, ar, b_st)
        gk = each(mm_nt, ar, k_st)
        a_ab = each(lambda x: jnp.where(strict, x[:two], 0.0), gb)
        a_rb = each(lambda x: jnp.where(incl, x[two:], 0.0), gb)
        a_ak = each(lambda x: jnp.where(strict, x[:two], 0.0), gk)
        a_rk = each(lambda x: jnp.where(incl, x[two:], 0.0), gk)
        av2 = each(lambda x, y, z: mm(jnp.concatenate([x, y], axis=0), z), a_ak, a_rk, v_st)
        t_inv = tri_inv(a_ab)
        pq = each(lambda t, x, y: mm(t, jnp.concatenate([x, y[:two]], axis=1)), t_inv, a_st, av2)
        yr = each(mm, a_rb, pq)
        kv = each(lambda x, y, z: mm_tn(x * y, z), k, g_end, v)
        pq_d = each(lambda x: jnp.concatenate([fold(x[:, :LANES]), fold(x[:, LANES:])], axis=1), pq)
        gh = each(lambda x, y, z: mm_tn(x * y, z), beta, g_end, pq_d)
        while pending:
            tick()
        for u in range(unroll):
            ry = fold(r_st[u] + yr[u][:, :LANES])
            g_bd = jnp.where(same_head, gh[u][:, :LANES], 0.0) + eye * jnp.exp(cum_end[u])
            lin_scr[u] = jnp.concatenate([ry, g_bd], axis=0)
            yq_scr[u] = fold(yr[u][:, LANES:] + av2[u][two:])
            h_scr[u] = jnp.where(same_head, gh[u][:, LANES:] + kv[u], 0.0)

    def sweep_steps(g, state):
        def step(u):
            rows = pl.ds(pl.multiple_of((g * unroll + u) * n, n), n)
            both = mm(lin_scr[u], state[0])
            y_ref[rows, :] = both[:n] + yq_scr[u]
            state[0] = both[n:] + h_scr[u]
        return [functools.partial(step, u) for u in range(unroll)]

    ngroups = nc // unroll
    groups_per_seq = ngroups // nseq
    build_group(0)

    def body(g, st):
        state = [st]
        pending.extend(sweep_steps(g - 1, state))
        build_group(g)
        return jnp.where(g % groups_per_seq == 0, 0.0, state[0])

    st = lax.fori_loop(1, ngroups, body, jnp.zeros((two, two), F32))
    state = [st]
    for step in sweep_steps(ngroups - 1, state):
        step()


def _rwkv_scan(r, wl, k2, v, kk, a, bsz, seq):
    m = r.shape[0]
    npair = BRANCH_W // LANES
    nseq = 2 if bsz % 2 == 0 else 1
    blk = pl.BlockSpec((nseq * seq, LANES), lambda b, p: (b, p))
    u = RWKV_UNROLL
    return pl.pallas_call(
        functools.partial(_rwkv_scan_kernel, unroll=u, nseq=nseq),
        name="rwkv_scan",
        grid=(bsz // nseq, npair),
        in_specs=[blk] * 6,
        out_specs=blk,
        out_shape=jax.ShapeDtypeStruct((m, BRANCH_W), F32),
        scratch_shapes=[pltpu.VMEM((u, 3 * RWKV_L, LANES), F32), pltpu.VMEM((u, RWKV_L, LANES), F32),
                        pltpu.VMEM((u, LANES, LANES), F32)],
        compiler_params=_cparams(("parallel", "parallel")),
    )(r, wl, k2, v, kk, a)


def _rwkv_post_kernel(y_ref, bon_ref, sg_ref, g_ref, b_ref, bd_ref, o_ref):
    y = y_ref[...]
    bd = bd_ref[...]
    mu = _head_sum(y, bd) * (1.0 / RWKV_HD)
    yc = y - mu
    var = _head_sum(yc * yc, bd) * (1.0 / RWKV_HD)
    yn = yc * lax.rsqrt(var + RWKV_GN_EPS) * g_ref[...] + b_ref[...]
    o_ref[...] = ((yn + bon_ref[...].astype(F32)) * sg_ref[...].astype(F32)).astype(o_ref.dtype)


def _rwkv_post(y, bon, sg, g, b, bd):
    m, w = y.shape
    tb = RWKV_POST_TB
    rows = pl.BlockSpec((tb, w), lambda i: (i, 0))
    vec = pl.BlockSpec((1, w), lambda i: (0, 0))
    return pl.pallas_call(
        _rwkv_post_kernel,
        name="rwkv_post",
        grid=(m // tb,),
        in_specs=[rows, rows, rows, vec, vec, pl.BlockSpec((LANES, LANES), lambda i: (0, 0))],
        out_specs=rows,
        out_shape=jax.ShapeDtypeStruct((m, w), BF16),
        compiler_params=_cparams(("parallel",)),
    )(y, bon, sg, g.reshape(1, w), b.reshape(1, w), bd)


def _sgu_kernel(u_ref, v_ref, gate_ref, lng_ref, lnb_ref, ws_ref, bs_ref, o_ref):
    inv_sqrt2 = 0.7071067811865476

    def gelu(z):
        return 0.5 * z * (1.0 + lax.erf(z * inv_sqrt2))

    v = gelu(v_ref[...].astype(F32))
    mu = jnp.mean(v, axis=-1, keepdims=True)
    vc = v - mu
    var = jnp.mean(vc * vc, axis=-1, keepdims=True)
    vn = vc * lax.rsqrt(var + SGU_LN_EPS) * lng_ref[...] + lnb_ref[...]
    t = SGU_CHUNK
    ii = lax.broadcasted_iota(I32, (t, t), 0) // CHUNK
    jj = lax.broadcasted_iota(I32, (t, t), 1) // CHUNK
    mask = jj <= ii
    for g in range(SGU_GROUPS):
        ws = jnp.where(mask, ws_ref[g], 0.0)
        sl = slice(g * SGU_GW, (g + 1) * SGU_GW)
        bias = bs_ref[:, g:g + 1]
        for c in range(u_ref.shape[0] // t):
            rows = slice(c * t, (c + 1) * t)
            sv = _dot(ws, vn[rows, sl]) + bias
            u = gelu(u_ref[rows, sl].astype(F32))
            o_ref[rows, sl] = (u * sv * _silu(gate_ref[rows, sl].astype(F32))).astype(o_ref.dtype)


def _sgu(p, lng, lnb, ws, bs_t):
    m = p.shape[0]
    w = BRANCH_W
    tb = SGU_TB
    d0 = P_D0 // w
    return pl.pallas_call(
        _sgu_kernel,
        name="sgu",
        grid=(m // tb,),
        in_specs=[pl.BlockSpec((tb, w), lambda i: (i, d0)),
                  pl.BlockSpec((tb, w), lambda i: (i, d0 + 1)),
                  pl.BlockSpec((tb, w), lambda i: (i, d0 + 2)),
                  pl.BlockSpec((1, w), lambda i: (0, 0)),
                  pl.BlockSpec((1, w), lambda i: (0, 0)),
                  pl.BlockSpec((SGU_GROUPS, SGU_CHUNK, SGU_CHUNK), lambda i: (0, 0, 0)),
                  pl.BlockSpec((SGU_CHUNK, SGU_GROUPS), lambda i: (0, 0))],
        out_specs=pl.BlockSpec((tb, w), lambda i: (i, 0)),
        out_shape=jax.ShapeDtypeStruct((m, w), BF16),
        compiler_params=_cparams(("parallel",)),
    )(p, p, p, lng.reshape(1, w), lnb.reshape(1, w), ws, bs_t)


def _merge_kernel(ya_ref, yb_ref, yc_ref, yd_ref, ga_ref, gb_ref, gc_ref, gd_ref, wp_ref, o_ref):
    acc = None
    for g, (y_ref, g_ref) in enumerate(((ya_ref, ga_ref), (yb_ref, gb_ref), (yc_ref, gc_ref), (yd_ref, gd_ref))):
        term = jax.nn.sigmoid(g_ref[...].astype(F32)) * _dot(y_ref[...], wp_ref[g])
        acc = term if acc is None else acc + term
    o_ref[...] = acc.astype(o_ref.dtype)


def _merge(ya, yb, yc, yd, p, wproj):
    m = ya.shape[0]
    w = BRANCH_W
    tm, tn = MERGE_TM, MERGE_TN
    nj = D_MODEL // tn
    m0 = P_M0 // tn
    ysp = pl.BlockSpec((tm, w), lambda j, i: (i, 0))
    gsp = [pl.BlockSpec((tm, tn), lambda j, i, g=g: (i, m0 + g * nj + j)) for g in range(N_BRANCH)]
    return pl.pallas_call(
        _merge_kernel,
        name="merge",
        grid=(nj, m // tm),
        in_specs=[ysp, ysp, ysp, ysp, *gsp,
                  pl.BlockSpec((N_BRANCH, w, tn), lambda j, i: (0, 0, j))],
        out_specs=pl.BlockSpec((tm, tn), lambda j, i: (i, j)),
        out_shape=jax.ShapeDtypeStruct((m, D_MODEL), BF16),
        compiler_params=_cparams(("parallel", "parallel")),
    )(ya, yb, yc, yd, p, p, p, p, wproj)


def _out_kernel(mg_ref, wo_ref, g_ref, x_ref, o_ref):
    y = _dot(mg_ref[...], wo_ref[...])
    y = y * lax.rsqrt(jnp.mean(y * y, axis=-1, keepdims=True) + NORM_EPS) * g_ref[...]
    o_ref[...] = x_ref[...] + y


def _out_proj(merged, wout, g, x2):
    m, d = x2.shape
    tm = OUT_TM
    return pl.pallas_call(
        _out_kernel,
        name="out_proj",
        grid=(m // tm,),
        in_specs=[pl.BlockSpec((tm, d), lambda i: (i, 0)),
                  pl.BlockSpec((d, d), lambda i: (0, 0)),
                  pl.BlockSpec((1, d), lambda i: (0, 0)),
                  pl.BlockSpec((tm, d), lambda i: (i, 0))],
        out_specs=pl.BlockSpec((tm, d), lambda i: (i, 0)),
        out_shape=jax.ShapeDtypeStruct((m, d), F32),
        compiler_params=_cparams(("parallel",)),
    )(merged, wout, g.reshape(1, d), x2)


def _split_cols(w, sizes):
    offs = np.cumsum((0,) + tuple(sizes))
    return [w[:, int(offs[i]):int(offs[i + 1])] for i in range(len(sizes))]


def _pad_cols(w, n):
    return jnp.pad(w, ((0, 0), (0, n - w.shape[1])))


def _pad_rows(w, n):
    return jnp.pad(w, ((0, n - w.shape[0]), (0, 0)))


def _rope_tables(positions, rot_dim, width):
    inv = ROPE_THETA ** (-jnp.arange(0, rot_dim, 2, dtype=F32) / rot_dim)
    ang = positions.astype(F32).reshape(-1, 1) * inv
    cos, sin = jnp.cos(ang), jnp.sin(ang)
    m = cos.shape[0]
    rest = width - rot_dim
    cos_h = jnp.concatenate([cos, cos, jnp.ones((m, rest), F32)], axis=1)
    sin_h = jnp.concatenate([-sin, sin, jnp.zeros((m, rest), F32)], axis=1)
    reps = LANES // width
    return jnp.tile(cos_h, (1, reps)), jnp.tile(sin_h, (1, reps))


def kernel(x, positions, norm_pre, norm_post, w_in, gla_w_g2, gla_b_g, gla_head_g, rwkv_mu, rwkv_w0,
           rwkv_w_w2, rwkv_a0, rwkv_w_a2, rwkv_k_k, rwkv_k_a, rwkv_r_k, rwkv_lnx_g, rwkv_lnx_b,
           sgu_ln_g, sgu_ln_b, sgu_w_s, sgu_b_s, w_proj, w_out):
    bsz, seq, d = x.shape
    depth = w_in.shape[0]
    m = bsz * seq
    x2 = x.reshape(m, d)
    ca, sa = _rope_tables(positions, DSA_ROT, DSA_HD)
    ci, si = _rope_tables(positions, IDX_ROT, IDX_HD)
    lane_grp = np.arange(LANES) // RWKV_HD
    bd = jnp.asarray((lane_grp[:, None] == lane_grp[None, :]).astype(np.float32))

    for l in range(depth):
        wa, wb, wc, wd, wm = _split_cols(w_in[l], GROUP_COLS)
        a_q, a_k, a_v, a_glr, a_gate = _split_cols(wa, A_COLS)
        b_q, b_k, b_v, b_iq, b_ik, b_iw, b_gate = _split_cols(wb, B_COLS)
        c_r, c_k, c_v, c_wlr, c_alr, c_gate = _split_cols(wc, C_COLS)
        w_main = jnp.concatenate([t.astype(BF16) for t in (wm, b_q, b_k, b_v, b_gate, c_r, c_k, c_v, c_gate, wd,
                                                            a_q, a_k, a_v, a_gate)], axis=1)
        w_small = jnp.concatenate([b_iq] + [_pad_cols(t, LANES) for t in (b_ik, b_iw, c_wlr, c_alr, a_glr)],
                                  axis=1).astype(BF16)

        h = _rmsnorm_cast(x2, norm_pre[l])
        p = _matmul(h, w_main, MM_TN, BF16)
        ps = _matmul(h, w_small, S_COLS, F32)

        ya = _gla(p, ps, _pad_rows(gla_w_g2[l], LANES), gla_b_g[l], gla_head_g[l], bsz, seq)

        qr, kr, vt, iqr, ikr, iws = _dsa_prep(p, ps, ca, sa, ci, si)
        yb = _dsa(qr, kr, vt, iqr, ikr, iws, p, bsz, seq)

        mu_r, mu_k, mu_v, mu_wlr, mu_alr, mu_gate = _split_cols(rwkv_mu[l][None, :], C_COLS)
        mu_main = jnp.concatenate([mu_r, mu_k, mu_v, mu_gate], axis=1)
        mu_s = jnp.concatenate([_pad_cols(mu_wlr, LANES), _pad_cols(mu_alr, LANES)], axis=1)
        r, wl, k2, v, kk, a, sg, bon = _rwkv_prep(
            p, ps, mu_main, mu_s, rwkv_w0[l], _pad_rows(rwkv_w_w2[l], LANES), rwkv_a0[l],
            _pad_rows(rwkv_w_a2[l], LANES), rwkv_k_k[l], rwkv_k_a[l], rwkv_r_k[l].reshape(-1), bd, seq)
        y = _rwkv_scan(r, wl, k2, v, kk, a, bsz, seq)
        yc = _rwkv_post(y, bon, sg, rwkv_lnx_g[l], rwkv_lnx_b[l], bd)

        yd = _sgu(p, sgu_ln_g[l], sgu_ln_b[l], sgu_w_s[l], sgu_b_s[l].T)

        merged = _merge(ya, yb, yc, yd, p, w_proj[l].astype(BF16))
        x2 = _out_proj(merged, w_out[l].astype(BF16), norm_post[l], x2)
    return x2.reshape(bsz, seq, d)
```

```python
import functools

import numpy as np
import jax
import jax.numpy as jnp
from jax import lax
from jax.experimental import pallas as pl
from jax.experimental.pallas import tpu as pltpu

F32 = jnp.float32
BF16 = jnp.bfloat16
I32 = jnp.int32
HIGHEST = lax.Precision.HIGHEST

D_MODEL = 2048
CHUNK = 64
NORM_EPS = 1e-6
ROPE_THETA = 500000.0
N_BRANCH = 4
BRANCH_W = D_MODEL // 2

GLA_HEADS = 4
GLA_DK = BRANCH_W // 2 // GLA_HEADS
GLA_DV = BRANCH_W // GLA_HEADS
GLA_RANK = 16
GLA_TAU = 16.0

DSA_HEADS = 8
DSA_HD = BRANCH_W // DSA_HEADS
DSA_ROT = DSA_HD // 4
IDX_HEADS = 8
IDX_HD = 64
IDX_ROT = IDX_HD // 4
TOPK_MAX = 256

RWKV_HD = 64
RWKV_HEADS = BRANCH_W // RWKV_HD
RWKV_DECAY_RANK = 96
RWKV_A_RANK = 96
RWKV_DECAY_SCALE = 0.606531
RWKV_GN_EPS = 64e-5
RWKV_KK_EPS = 1e-12

SGU_CHUNK = 128
SGU_GROUPS = 8
SGU_GW = BRANCH_W // SGU_GROUPS
SGU_LN_EPS = 1e-5

A_COLS = (GLA_HEADS * GLA_DK, GLA_HEADS * GLA_DK, BRANCH_W, GLA_RANK, BRANCH_W)
B_COLS = (BRANCH_W, BRANCH_W, BRANCH_W, IDX_HEADS * IDX_HD, IDX_HD, IDX_HEADS, BRANCH_W)
C_COLS = (BRANCH_W, BRANCH_W, BRANCH_W, RWKV_DECAY_RANK, RWKV_A_RANK, BRANCH_W)
D_COLS = (2 * BRANCH_W, BRANCH_W)
GROUP_COLS = (sum(A_COLS), sum(B_COLS), sum(C_COLS), sum(D_COLS), N_BRANCH * D_MODEL)

LANES = 128
VMEM_LIMIT = 56 * 1024 * 1024
MXU_DEPTH = 256
IDX_K = MXU_DEPTH
NEG_BIG = -1e30
LOG2E = 1.4426950408889634
INT_MIN = -(2 ** 31)

P_M0 = 0
P_B0 = P_M0 + N_BRANCH * D_MODEL
P_C0 = P_B0 + 4 * BRANCH_W
P_D0 = P_C0 + 4 * BRANCH_W
P_A0 = P_D0 + 3 * BRANCH_W
P_COLS = P_A0 + 2 * GLA_HEADS * GLA_DK + 2 * BRANCH_W
S_IDX0 = 0
S_IDX_COLS = IDX_HEADS * IDX_HD + 2 * 128
S_RW0 = S_IDX0 + S_IDX_COLS
S_GLR0 = S_RW0 + 2 * 128
S_COLS = S_GLR0 + 128

MM_TM = 1024
MM_TN = 1024
GLA_TB = 512
DSA_PREP_TB = 512
DSA_TQ = 256
DSA_KB = 512
DSA_VROWS = DSA_HD + 16
RWKV_PREP_TB = 256
RWKV_L = 64
RWKV_UNROLL = 8
RWKV_POST_TB = 512
SGU_TB = 512
MERGE_TM = 512
MERGE_TN = 1024
OUT_TM = 512


def _cparams(sem):
    return pltpu.CompilerParams(dimension_semantics=sem, vmem_limit_bytes=VMEM_LIMIT)


def _dot(a, b, precision=None):
    return jnp.dot(a, b, preferred_element_type=F32, precision=precision)


def _dot_nt(a, b, precision=None):
    return lax.dot_general(a, b, (((1,), (1,)), ((), ())), preferred_element_type=F32,
                           precision=precision)


def _dot_tn(a, b, precision=None):
    return lax.dot_general(a, b, (((0,), (0,)), ((), ())), preferred_element_type=F32,
                           precision=precision)


def _bf16_terms(x, n):
    terms = []
    for _ in range(n):
        t = x.astype(BF16)
        terms.append(t)
        x = x - t.astype(F32)
    return terms


def _dot_exact_lhs(a, b, nterms):
    a16 = a.astype(BF16)
    out = None
    for t in _bf16_terms(b, nterms):
        d = _dot(a16, t)
        out = d if out is None else out + d
    return out


def _dot_exact_rhs(a, b, nterms):
    b16 = b.astype(BF16)
    out = None
    for t in _bf16_terms(a, nterms):
        d = _dot(t, b16)
        out = d if out is None else out + d
    return out


def _dot_3pass(a, b):
    a_hi, a_lo = _bf16_terms(a, 2)
    b_hi, b_lo = _bf16_terms(b, 2)
    return _dot(a_hi, b_hi) + _dot(a_lo, b_hi) + _dot(a_hi, b_lo)


def _silu(x):
    return x * jax.nn.sigmoid(x)


def _norm_matmul_kernel(x_ref, g_ref, b_ref, o_ref, h_ref):
    @pl.when(pl.program_id(1) == 0)
    def _():
        x = x_ref[...]
        ms = jnp.mean(x * x, axis=-1, keepdims=True)
        h_ref[...] = (x * lax.rsqrt(ms + NORM_EPS) * g_ref[...]).astype(h_ref.dtype)

    o_ref[...] = _dot(h_ref[...], b_ref[...]).astype(o_ref.dtype)


def _norm_matmul(x2, g, b, tn):
    m, k = x2.shape
    n = b.shape[1]
    tm = min(MM_TM, m)
    return pl.pallas_call(
        _norm_matmul_kernel,
        name="proj",
        grid=(m // tm, n // tn),
        in_specs=[pl.BlockSpec((tm, k), lambda i, j: (i, 0)),
                  pl.BlockSpec((1, k), lambda i, j: (0, 0)),
                  pl.BlockSpec((k, tn), lambda i, j: (0, j))],
        out_specs=[pl.BlockSpec((tm, tn), lambda i, j: (i, j)),
                   pl.BlockSpec((tm, k), lambda i, j: (i, 0))],
        out_shape=[jax.ShapeDtypeStruct((m, n), BF16), jax.ShapeDtypeStruct((m, k), BF16)],
        compiler_params=_cparams(("parallel", "arbitrary")),
    )(x2, g.reshape(1, k), b)


def _matmul_kernel(a_ref, b_ref, o_ref):
    o_ref[...] = _dot(a_ref[...], b_ref[...]).astype(o_ref.dtype)


def _matmul(a, b, tn, out_dtype=F32):
    m, k = a.shape
    n = b.shape[1]
    tm = min(MM_TM, m)
    return pl.pallas_call(
        _matmul_kernel,
        name="proj",
        grid=(m // tm, n // tn),
        in_specs=[pl.BlockSpec((tm, k), lambda i, j: (i, 0)),
                  pl.BlockSpec((k, tn), lambda i, j: (0, j))],
        out_specs=pl.BlockSpec((tm, tn), lambda i, j: (i, j)),
        out_shape=jax.ShapeDtypeStruct((m, n), out_dtype),
        compiler_params=_cparams(("parallel", "parallel")),
    )(a, b)


def _gla_kernel(q_ref, k_ref, v_ref, gate_ref, glr_ref, wg2_ref, bg_ref, hg_ref, o_ref, state_ref):
    @pl.when(pl.program_id(2) == 0)
    def _():
        state_ref[...] = jnp.zeros_like(state_ref)

    ii = lax.broadcasted_iota(I32, (CHUNK, CHUNK), 0)
    jj = lax.broadcasted_iota(I32, (CHUNK, CHUNK), 1)
    tril = (ii >= jj).astype(F32)
    chunks = [pl.ds(c * CHUNK, CHUNK) for c in range(GLA_TB // CHUNK)]
    z = _dot_3pass(glr_ref[...], wg2_ref[...]) + bg_ref[...]
    log_a = (jnp.minimum(z, 0.0) - jnp.log1p(jnp.exp(-jnp.abs(z)))) / GLA_TAU
    cum = [_dot_exact_lhs(tril, log_a[c * CHUNK:(c + 1) * CHUNK], 3) for c in range(len(chunks))]
    total = [x[CHUNK - 1:CHUNK, :] for x in cum]
    k_dec = [k_ref[sl, :].astype(F32) * jnp.exp(t - x) for sl, t, x in zip(chunks, total, cum)]
    kv = [_dot_tn(v_ref[sl, :], kd.astype(BF16)) for sl, kd in zip(chunks, k_dec)]
    states = []
    st = state_ref[...]
    for t, x in zip(total, kv):
        st = st * jnp.exp(t) + x
        states.append(st)
    state_ref[...] = st
    outs = [_dot_nt((q_ref[sl, :].astype(F32) * (GLA_DK ** -0.5)).astype(BF16), s.astype(BF16))
            for sl, s in zip(chunks, states)]
    for sl, o in zip(chunks, outs):
        o = o * lax.rsqrt(jnp.mean(o * o, axis=-1, keepdims=True) + NORM_EPS) * hg_ref[...]
        o_ref[sl, :] = (o * _silu(gate_ref[sl, :].astype(F32))).astype(o_ref.dtype)


def _gla(p, ps, wg2p, bg, hg, bsz, seq):
    m = p.shape[0]
    nt = seq // GLA_TB
    row = lambda b, h, t: b * nt + t
    qb = P_A0 // GLA_DK
    kb = qb + GLA_HEADS
    vb = (P_A0 + 2 * GLA_HEADS * GLA_DK) // GLA_DV
    gb = vb + GLA_HEADS
    lb = S_GLR0 // LANES
    return pl.pallas_call(
        _gla_kernel,
        name="gla",
        grid=(bsz, GLA_HEADS, nt),
        in_specs=[
            pl.BlockSpec((GLA_TB, GLA_DK), lambda b, h, t: (row(b, h, t), qb + h)),
            pl.BlockSpec((GLA_TB, GLA_DK), lambda b, h, t: (row(b, h, t), kb + h)),
            pl.BlockSpec((GLA_TB, GLA_DV), lambda b, h, t: (row(b, h, t), vb + h)),
            pl.BlockSpec((GLA_TB, GLA_DV), lambda b, h, t: (row(b, h, t), gb + h)),
            pl.BlockSpec((GLA_TB, LANES), lambda b, h, t: (row(b, h, t), lb)),
            pl.BlockSpec((LANES, GLA_DK), lambda b, h, t: (0, h)),
            pl.BlockSpec((1, GLA_DK), lambda b, h, t: (0, h)),
            pl.BlockSpec((1, GLA_DV), lambda b, h, t: (0, 0)),
        ],
        out_specs=pl.BlockSpec((GLA_TB, GLA_DV), lambda b, h, t: (row(b, h, t), h)),
        out_shape=jax.ShapeDtypeStruct((m, BRANCH_W), BF16),
        scratch_shapes=[pltpu.VMEM((GLA_DV, GLA_DK), F32)],
        compiler_params=_cparams(("parallel", "parallel", "arbitrary")),
    )(p, p, p, p, ps, wg2p, bg.reshape(1, -1), hg.reshape(1, -1))


def _rope_tile(x, cos_t, sin_t, half, width):
    lane = lax.broadcasted_iota(I32, x.shape, 1) % width
    partner = jnp.where(lane < half, pltpu.roll(x, LANES - half, 1), pltpu.roll(x, half, 1))
    return x * cos_t + partner * sin_t


def _dsa_prep_kernel(q_ref, k_ref, v_ref, idx_ref, ca_ref, sa_ref, ci_ref, si_ref,
                     qo_ref, ko_ref, vo_ref, iqo_ref, iko_ref, iwo_ref):
    ca, sa = ca_ref[...], sa_ref[...]
    ci, si = ci_ref[...], si_ref[...]
    for h in range(DSA_HEADS):
        sl = slice(h * DSA_HD, (h + 1) * DSA_HD)
        q = _rope_tile(q_ref[:, sl].astype(F32), ca, sa, DSA_ROT // 2, DSA_HD) * (DSA_HD ** -0.5 * LOG2E)
        qo_ref[:, sl] = q.astype(qo_ref.dtype)
        ko_ref[:, sl] = _rope_tile(k_ref[:, sl].astype(F32), ca, sa, DSA_ROT // 2, DSA_HD).astype(ko_ref.dtype)
        vo_ref[0, h * DSA_VROWS:h * DSA_VROWS + DSA_HD, :] = v_ref[:, sl].astype(F32).T.astype(vo_ref.dtype)
        extra = lax.broadcasted_iota(I32, (DSA_VROWS - DSA_HD, q_ref.shape[0]), 0) == 0
        vo_ref[0, h * DSA_VROWS + DSA_HD:(h + 1) * DSA_VROWS, :] = jnp.where(extra, 1.0, 0.0).astype(vo_ref.dtype)
    niq = IDX_HEADS * IDX_HD
    low = lax.broadcasted_iota(I32, (q_ref.shape[0], LANES), 1) < IDX_HD

    def hi_lo(x):
        hi = x.astype(BF16).astype(F32)
        return hi, x - hi

    for j in range(niq // LANES):
        x = _rope_tile(idx_ref[:, j * LANES:(j + 1) * LANES], ci, si, IDX_ROT // 2, IDX_HD)
        hi, lo = hi_lo(x)
        hi_sw = pltpu.roll(hi, IDX_HD, 1)
        lo_sw = pltpu.roll(lo, IDX_HD, 1)
        base = 2 * j * IDX_K
        iqo_ref[:, base:base + LANES] = jnp.where(low, hi, hi_sw).astype(BF16)
        iqo_ref[:, base + LANES:base + IDX_K] = jnp.where(low, lo, 0.0).astype(BF16)
        iqo_ref[:, base + IDX_K:base + IDX_K + LANES] = jnp.where(low, hi_sw, hi).astype(BF16)
        iqo_ref[:, base + IDX_K + LANES:base + 2 * IDX_K] = jnp.where(low, lo_sw, 0.0).astype(BF16)
    hi, lo = hi_lo(_rope_tile(idx_ref[:, niq:niq + LANES], ci, si, IDX_ROT // 2, IDX_HD))
    iko_ref[:, :LANES] = jnp.where(low, hi, pltpu.roll(lo, IDX_HD, 1)).astype(BF16)
    iko_ref[:, LANES:] = jnp.where(low, hi, 0.0).astype(BF16)
    iwo_ref[...] = idx_ref[:, niq + LANES:niq + 2 * LANES] * (niq ** -0.5)


def _dsa_prep(p, ps, ca, sa, ci, si):
    m = p.shape[0]
    tb = DSA_KB
    w = BRANCH_W
    rows = lambda i: (i, 0)
    b0 = P_B0 // w
    return pl.pallas_call(
        _dsa_prep_kernel,
        name="dsa_prep",
        grid=(m // tb,),
        in_specs=[pl.BlockSpec((tb, w), lambda i: (i, b0)),
                  pl.BlockSpec((tb, w), lambda i: (i, b0 + 1)),
                  pl.BlockSpec((tb, w), lambda i: (i, b0 + 2)),
                  pl.BlockSpec((tb, S_IDX_COLS), lambda i: (i, S_IDX0 // S_IDX_COLS)),
                  pl.BlockSpec((tb, LANES), rows), pl.BlockSpec((tb, LANES), rows),
                  pl.BlockSpec((tb, LANES), rows), pl.BlockSpec((tb, LANES), rows)],
        out_specs=[pl.BlockSpec((tb, w), rows), pl.BlockSpec((tb, w), rows),
                   pl.BlockSpec((1, DSA_HEADS * DSA_VROWS, tb), lambda i: (i, 0, 0)),
                   pl.BlockSpec((tb, IDX_HEADS * IDX_K), rows), pl.BlockSpec((tb, IDX_K), rows),
                   pl.BlockSpec((tb, LANES), rows)],
        out_shape=[jax.ShapeDtypeStruct((m, w), BF16), jax.ShapeDtypeStruct((m, w), BF16),
                   jax.ShapeDtypeStruct((m // tb, DSA_HEADS * DSA_VROWS, tb), BF16),
                   jax.ShapeDtypeStruct((m, IDX_HEADS * IDX_K), BF16),
                   jax.ShapeDtypeStruct((m, IDX_K), BF16), jax.ShapeDtypeStruct((m, LANES), F32)],
        compiler_params=_cparams(("parallel",)),
    )(p, p, p, ps, ca, sa, ci, si)


def _dsa_kernel(q_ref, k_ref, vt_ref, iq_ref, ik_ref, iw_ref, gate_ref, o_ref, key_scr, hi_scr, tie_scr,
                *, topk):
    tq, kb_w = DSA_TQ, DSA_KB
    q0 = pl.program_id(1) * tq
    nkb = (q0 + tq + kb_w - 1) // kb_w
    qcol = lax.broadcasted_iota(I32, (1, tq), 1)
    qlim = ((q0 + qcol) // CHUNK + 1) * CHUNK
    sub_k = lax.broadcasted_iota(I32, (kb_w, 1), 0)
    sub = 8

    def score_body(kb, carry):
        ik = ik_ref[pl.ds(pl.multiple_of(kb * kb_w, kb_w), kb_w), :]
        dots = [_dot_nt(ik, iq_ref[:, h * IDX_K:(h + 1) * IDX_K]) for h in range(IDX_HEADS)]
        acc = jnp.zeros((kb_w, tq), F32)
        for h, d in enumerate(dots):
            acc = acc + iw_ref[h:h + 1, :] * jnp.maximum(d, 0.0)
        acc = acc + 0.0
        bits = lax.bitcast_convert_type(acc, I32)
        key = bits ^ ((bits >> 31) & 0x7FFFFFFF)
        sidx = kb * kb_w + sub_k
        key = jnp.where(sidx < qlim, key, INT_MIN)
        key_scr[kb] = key
        hi_scr[kb] = (key >> 16).astype(jnp.int16)
        return carry

    lax.fori_loop(0, nkb, score_body, 0)

    def count_tiles(load, pred, rows_per_vreg, dtype):
        def body(kb, c):
            m = pred(load(kb), kb * kb_w + sub_k).astype(dtype)
            if dtype == I32:
                nparts = 8
                rows_part = kb_w // nparts
                parts = [jnp.sum(m[j * rows_part:(j + 1) * rows_part]
                                 .reshape(rows_part // rows_per_vreg, rows_per_vreg, tq), axis=0)
                         for j in range(nparts)]
            else:
                parts = [m[j * rows_per_vreg:(j + 1) * rows_per_vreg] for j in range(kb_w // rows_per_vreg)]
            while len(parts) > 1:
                parts = [a + b for a, b in zip(parts[::2], parts[1::2])]
            return c + parts[0]
        c = lax.fori_loop(0, nkb, body, jnp.zeros((rows_per_vreg, tq), dtype))
        return jnp.sum(c.astype(I32), axis=0, keepdims=True)

    def count(pred):
        return count_tiles(lambda kb: key_scr[kb], pred, sub, I32)

    def count_hi(pred):
        return count_tiles(lambda kb: hi_scr[kb], pred, 2 * sub, jnp.int16)

    def bit_body(it, carry, high_only):
        res, cnt_res = carry
        trial = res | lax.shift_left(jnp.int32(1), 31 - it)
        cand = trial ^ INT_MIN
        if high_only:
            cand_hi = (cand >> 16).astype(jnp.int16)
            cnt = count_hi(lambda kt, s: kt >= cand_hi)
        else:
            cnt = count(lambda kt, s: kt >= cand)
        ok = cnt >= topk
        return jnp.where(ok, trial, res), jnp.where(ok, cnt, cnt_res)

    carry = (jnp.zeros((1, tq), I32), jnp.full((1, tq), nkb * kb_w, I32))
    carry = lax.fori_loop(0, 16, functools.partial(bit_body, high_only=True), carry)
    res, cnt_ge = lax.fori_loop(16, 32, functools.partial(bit_body, high_only=False), carry)
    thr = res ^ INT_MIN
    seq_len = key_scr.shape[0] * kb_w
    tie_scr[...] = jnp.full(tie_scr.shape, seq_len, I32)
    has_tie = (cnt_ge > topk) & (thr != INT_MIN)

    @pl.when(jnp.max(jnp.where(has_tie, 1.0, 0.0)) > 0.5)
    def _():
        need = topk - count(lambda kt, s: kt > thr)
        nbits = max(1, int(np.ceil(np.log2(seq_len))))

        def tie_body(it, resj):
            trial = resj | lax.shift_left(jnp.int32(1), nbits - 1 - it)
            c = count(lambda kt, s: (kt == thr) & (s < trial))
            return jnp.where(c < need, trial, resj)

        tie_scr[...] = lax.fori_loop(0, nbits, tie_body, jnp.zeros((1, tq), I32))

    last_tie = tie_scr[...]

    def bias_body(kb, carry):
        kt = key_scr[kb]
        sidx = kb * kb_w + sub_k
        sel = ((kt > thr) | ((kt == thr) & (sidx <= last_tie))) & (kt != INT_MIN)
        key_scr[kb] = lax.bitcast_convert_type(jnp.where(sel, 0.0, NEG_BIG), I32)
        return carry

    lax.fori_loop(0, nkb, bias_body, 0)

    heads = [slice(h * DSA_HD, (h + 1) * DSA_HD) for h in range(DSA_HEADS)]

    def att_body(kb, stats):
        rows = pl.ds(pl.multiple_of(kb * kb_w, kb_w), kb_w)
        bias = lax.bitcast_convert_type(key_scr[kb], F32).astype(BF16)
        s = [_dot_nt(k_ref[rows, sl], q_ref[:, sl]).astype(BF16) + bias for sl in heads]
        m_n = [jnp.maximum(c[0], jnp.max(x, axis=0, keepdims=True).astype(F32)) for c, x in zip(stats, s)]
        p = [jnp.exp2(x - m.astype(BF16)) for x, m in zip(s, m_n)]
        pv = [_dot(vt_ref[kb, h * DSA_VROWS:(h + 1) * DSA_VROWS, :], x) for h, x in enumerate(p)]
        alpha = [jnp.exp2(c[0] - m) for c, m in zip(stats, m_n)]
        return tuple((m, a * c[1] + y) for m, a, c, y in zip(m_n, alpha, stats, pv))

    init = tuple((jnp.full((1, tq), NEG_BIG, F32), jnp.zeros((DSA_VROWS, tq), F32)) for _ in heads)
    final = lax.fori_loop(0, nkb, att_body, init)
    for sl, (_, acc_f) in zip(heads, final):
        o = (acc_f[:DSA_HD] / acc_f[DSA_HD:DSA_HD + 1]).T
        o_ref[:, sl] = (o * _silu(gate_ref[:, sl].astype(F32))).astype(o_ref.dtype)


def _dsa(qr, kr, vt, iqr, ikr, iws, p, bsz, seq):
    m = qr.shape[0]
    w = BRANCH_W
    nq = seq // DSA_TQ
    nkb = seq // DSA_KB
    topk = min(TOPK_MAX, seq // 4)
    qrow = lambda b, i: (b * nq + i, 0)
    return pl.pallas_call(
        functools.partial(_dsa_kernel, topk=topk),
        name="dsa",
        grid=(bsz, nq),
        in_specs=[pl.BlockSpec((DSA_TQ, w), qrow),
                  pl.BlockSpec((seq, w), lambda b, i: (b, 0), pipeline_mode=pl.Buffered(1)),
                  pl.BlockSpec((nkb, DSA_HEADS * DSA_VROWS, DSA_KB), lambda b, i: (b, 0, 0),
                               pipeline_mode=pl.Buffered(1)),
                  pl.BlockSpec((DSA_TQ, IDX_HEADS * IDX_K), qrow),
                  pl.BlockSpec((seq, IDX_K), lambda b, i: (b, 0), pipeline_mode=pl.Buffered(1)),
                  pl.BlockSpec((IDX_HEADS, DSA_TQ), lambda b, i: (0, b * nq + i)),
                  pl.BlockSpec((DSA_TQ, w), lambda b, i: (b * nq + i, P_B0 // w + 3))],
        out_specs=pl.BlockSpec((DSA_TQ, w), qrow),
        out_shape=jax.ShapeDtypeStruct((m, w), BF16),
        scratch_shapes=[pltpu.VMEM((nkb, DSA_KB, DSA_TQ), I32),
                        pltpu.VMEM((nkb, DSA_KB, DSA_TQ), jnp.int16),
                        pltpu.VMEM((1, DSA_TQ), I32)],
        compiler_params=_cparams(("parallel", "arbitrary")),
    )(qr, kr, vt, iqr, ikr, iws[:, :IDX_HEADS].T, p)


def _head_sum(x, bd):
    parts = [_dot_exact_rhs(x[:, j * LANES:(j + 1) * LANES], bd, 2) for j in range(x.shape[1] // LANES)]
    return jnp.concatenate(parts, axis=1)


def _rwkv_prep_kernel(c_ref, cs_ref, pc_ref, pcs_ref, mu_ref, mus_ref, w0_ref, ww2_ref, a0_ref, wa2_ref,
                      kkw_ref, ka_ref, rk_ref, bd_ref,
                      r_o, wl_o, k_o, v_o, kk_o, a_o, sg_o, bon_o, *, blocks_per_seq):
    tb = c_ref.shape[0]
    first = (pl.program_id(0) % blocks_per_seq) == 0
    rowi = lax.broadcasted_iota(I32, (tb, 1), 0)

    def shift_lerp(c, prev_rows, mu):
        last = prev_rows.shape[0] - 1
        prev = jnp.where(first, 0.0, prev_rows[last:last + 1, :])
        sh = jnp.where(rowi == 0, prev, pltpu.roll(c, 1, 0))
        return c + (sh - c) * mu

    w = BRANCH_W
    c = shift_lerp(c_ref[...].astype(F32), pc_ref[...].astype(F32), mu_ref[...])
    cs = shift_lerp(cs_ref[...], pcs_ref[...], mus_ref[...])
    r, k, v, gate = c[:, :w], c[:, w:2 * w], c[:, 2 * w:3 * w], c[:, 3 * w:]
    w_log = -RWKV_DECAY_SCALE * jax.nn.sigmoid(w0_ref[...] + _dot_3pass(jnp.tanh(cs[:, :LANES]), ww2_ref[...]))
    a = jax.nn.sigmoid(a0_ref[...] + _dot_3pass(cs[:, LANES:], wa2_ref[...]))
    kk = k * kkw_ref[...]
    k2 = k * (1.0 + (a - 1.0) * ka_ref[...])
    bd = bd_ref[...]
    kk = kk * lax.rsqrt(_head_sum(kk * kk, bd) + RWKV_KK_EPS)
    r_o[...] = r.astype(r_o.dtype)
    wl_o[...] = w_log
    k_o[...] = k2.astype(k_o.dtype)
    v_o[...] = v.astype(v_o.dtype)
    kk_o[...] = kk.astype(kk_o.dtype)
    a_o[...] = a.astype(a_o.dtype)
    sg_o[...] = _silu(gate).astype(sg_o.dtype)
    bon_o[...] = (_head_sum(r * k2 * rk_ref[...], bd) * v).astype(bon_o.dtype)


def _rwkv_prep(p, ps, mu_main, mu_s, w0, ww2p, a0, wa2p, kkw, ka, rk, bd, seq):
    m = p.shape[0]
    tb = RWKV_PREP_TB
    w = BRANCH_W
    pr16, pr8 = 16, 8
    cblk, sblk = P_C0 // (4 * w), S_RW0 // (2 * LANES)
    const = lambda i: (0, 0)
    vec = pl.BlockSpec((1, w), const)
    out = pl.BlockSpec((tb, w), lambda i: (i, 0))
    dts = [BF16, F32, BF16, BF16, BF16, BF16, BF16, BF16]
    return pl.pallas_call(
        functools.partial(_rwkv_prep_kernel, blocks_per_seq=seq // tb),
        name="rwkv_prep",
        grid=(m // tb,),
        in_specs=[pl.BlockSpec((tb, 4 * w), lambda i: (i, cblk)),
                  pl.BlockSpec((tb, 2 * LANES), lambda i: (i, sblk)),
                  pl.BlockSpec((pr16, 4 * w), lambda i: (jnp.maximum(i * (tb // pr16) - 1, 0), cblk)),
                  pl.BlockSpec((pr8, 2 * LANES), lambda i: (jnp.maximum(i * (tb // pr8) - 1, 0), sblk)),
                  pl.BlockSpec((1, 4 * w), const), pl.BlockSpec((1, 2 * LANES), const),
                  vec, pl.BlockSpec((LANES, w), const), vec, pl.BlockSpec((LANES, w), const),
                  vec, vec, vec, pl.BlockSpec((LANES, LANES), const)],
        out_specs=[out] * 8,
        out_shape=[jax.ShapeDtypeStruct((m, w), dt) for dt in dts],
        compiler_params=_cparams(("parallel",)),
    )(p, ps, p, ps, mu_main, mu_s, w0.reshape(1, w), ww2p, a0.reshape(1, w), wa2p,
      kkw.reshape(1, w), ka.reshape(1, w), rk.reshape(1, w), bd)


def _rwkv_scan_kernel(r_ref, wl_ref, k_ref, v_ref, kk_ref, a_ref, y_ref, lin_scr, yq_scr, h_scr, *, unroll, nseq):
    n = RWKV_L
    two = 2 * n
    nc = r_ref.shape[0] // n
    i2 = lax.broadcasted_iota(I32, (two, two), 0)
    j2 = lax.broadcasted_iota(I32, (two, two), 1)
    same_head = (i2 // RWKV_HD) == (j2 // RWKV_HD)
    eye2 = (i2 == j2).astype(F32)
    ti = lax.broadcasted_iota(I32, (n, LANES), 0)
    ci = lax.broadcasted_iota(I32, (n, LANES), 1) % n
    strict = ti > ci
    incl = ti >= ci
    same16 = (ti // 16) == (ci // 16)
    eye = (ti == ci).astype(F32)
    ii = lax.broadcasted_iota(I32, (n, n), 0)
    jj = lax.broadcasted_iota(I32, (n, n), 1)
    tril = (ii >= jj).astype(F32)
    head0 = lax.broadcasted_iota(I32, (n, LANES), 1) < RWKV_HD

    def mm(a, b):
        return _dot(a.astype(BF16), b.astype(BF16))

    def mm_nt(a, b):
        return _dot_nt(a.astype(BF16), b.astype(BF16))

    def mm_tn(a, b):
        return _dot_tn(a.astype(BF16), b.astype(BF16))

    def stack(x):
        return jnp.concatenate([jnp.where(head0, x, 0.0), jnp.where(head0, 0.0, x)], axis=0)

    def fold(x):
        return x[:n] + x[n:]

    def each(f, *lists):
        return [f(*args) for args in zip(*lists)]

    def tri_inv(a_low):
        a_d = each(lambda a: jnp.where(same16, a, 0.0), a_low)
        a_o = each(lambda a, d: a - d, a_low, a_d)
        ad_st = each(stack, a_d)
        a2 = each(mm, a_d, ad_st)
        tick()
        a34 = each(lambda s, d: mm(s, jnp.concatenate([d, stack(s)], axis=1)), a2, ad_st)
        tick()
        a4_st = each(lambda x: stack(x[:, LANES:]), a34)
        s4 = each(lambda d, s, x: eye + d + s + x[:, :LANES], a_d, a2, a34)
        a8 = each(lambda x, y: mm(x[:, LANES:], y), a34, a4_st)
        tick()
        s8 = each(lambda s, x: s + mm(s, x), s4, a4_st)
        tick()
        t_d = each(lambda s, x: s + mm(s, stack(x)), s8, a8)
        tick()
        nn = each(lambda t, o: mm(t, stack(o)), t_d, a_o)
        tick()
        nx = each(lambda x, t: mm(x, jnp.concatenate([stack(x), stack(t)], axis=1)), nn, t_d)
        tick()
        x1 = each(lambda t, x: t + x[:, LANES:], t_d, nx)
        return each(lambda x, y: y + mm(x[:, :LANES], stack(y)), nx, x1)

    pending = []

    def tick():
        for _ in range(-(-unroll // 8)):
            if pending:
                pending.pop(0)()

    def build_group(g):
        rows = [pl.ds(pl.multiple_of((g * unroll + u) * n, n), n) for u in range(unroll)]
        wl = [wl_ref[r, :] for r in rows]
        k = [k_ref[r, :].astype(F32) for r in rows]
        v = [v_ref[r, :].astype(F32) for r in rows]
        kk = [kk_ref[r, :].astype(F32) for r in rows]
        beta = each(lambda x, r: x * a_ref[r, :].astype(F32), kk, rows)
        cum = each(lambda x: _dot_exact_lhs(tril, x, 3), wl)
        cum_end = each(lambda x: x[n - 1:n, :], cum)
        g_inv = each(lambda x: jnp.exp(-x), cum)
        g_end = each(lambda e, x: jnp.exp(e - x), cum_end, cum)
        a_t = each(lambda x, c, w: -x * jnp.exp(c - w), kk, cum, wl)
        r_t = each(lambda r, c: r_ref[r, :].astype(F32) * jnp.exp(c), rows, cum)
        bk_st = each(lambda x, y, z: jnp.concatenate([stack(x * z), stack(y * z)], axis=0), beta, k, g_inv)
        ar = each(lambda x, y: jnp.concatenate([x, y], axis=0), a_t, r_t)
        gram = each(mm_nt, ar, bk_st)
        a_ab = each(lambda x: jnp.where(strict, x[:n, :LANES], 0.0), gram)
        a_rb = each(lambda x: jnp.where(incl, x[n:, :LANES], 0.0), gram)
        a_ak = each(lambda x: jnp.where(strict, x[:n, LANES:], 0.0), gram)
        a_rk = each(lambda x: jnp.where(incl, x[n:, LANES:], 0.0), gram)
        av2 = each(lambda x, y, z: mm(jnp.concatenate([x, y], axis=0), stack(z)), a_ak, a_rk, v)
        t_inv = tri_inv(a_ab)
        tick()
        pq = each(lambda t, x, y: mm(t, jnp.concatenate([stack(x), stack(y[:n])], axis=1)), t_inv, a_t, av2)
        yr = each(lambda x, y: mm(x, jnp.concatenate([stack(y[:, :LANES]), stack(y[:, LANES:])], axis=1)), a_rb, pq)
        kv = each(lambda x, y, z: mm_tn(x * y, z), k, g_end, v)
        gh = each(lambda x, y, z: mm_tn(x * y, z), beta, g_end, pq)
        while pending:
            tick()
        for u in range(unroll):
            ry = r_t[u] + yr[u][:, :LANES]
            g_bd = jnp.where(same_head, gh[u][:, :LANES], 0.0) + eye2 * jnp.exp(cum_end[u])
            lin_scr[u] = jnp.concatenate([ry, g_bd], axis=0)
            yq_scr[u] = yr[u][:, LANES:] + av2[u][n:]
            h_scr[u] = jnp.where(same_head, gh[u][:, LANES:] + kv[u], 0.0)

    def sweep_steps(g, state):
        def step(u):
            rows = pl.ds(pl.multiple_of((g * unroll + u) * n, n), n)
            both = mm(lin_scr[u], state[0])
            y_ref[rows, :] = both[:n] + yq_scr[u]
            state[0] = both[n:] + h_scr[u]
        return [functools.partial(step, u) for u in range(unroll)]

    ngroups = nc // unroll
    groups_per_seq = ngroups // nseq
    build_group(0)

    def body(g, st):
        state = [st]
        pending.extend(sweep_steps(g - 1, state))
        build_group(g)
        return jnp.where(g % groups_per_seq == 0, 0.0, state[0])

    st = lax.fori_loop(1, ngroups, body, jnp.zeros((two, two), F32))
    state = [st]
    for step in sweep_steps(ngroups - 1, state):
        step()


def _rwkv_scan(r, wl, k2, v, kk, a, bsz, seq):
    m = r.shape[0]
    npair = BRANCH_W // LANES
    nseq = 2 if bsz % 2 == 0 else 1
    blk = pl.BlockSpec((nseq * seq, LANES), lambda b, p: (b, p))
    u = RWKV_UNROLL
    return pl.pallas_call(
        functools.partial(_rwkv_scan_kernel, unroll=u, nseq=nseq),
        name="rwkv_scan",
        grid=(bsz // nseq, npair),
        in_specs=[blk] * 6,
        out_specs=blk,
        out_shape=jax.ShapeDtypeStruct((m, BRANCH_W), F32),
        scratch_shapes=[pltpu.VMEM((u, 3 * RWKV_L, LANES), F32), pltpu.VMEM((u, RWKV_L, LANES), F32),
                        pltpu.VMEM((u, LANES, LANES), F32)],
        compiler_params=_cparams(("parallel", "parallel")),
    )(r, wl, k2, v, kk, a)


def _rwkv_post_kernel(y_ref, bon_ref, sg_ref, g_ref, b_ref, bd_ref, o_ref):
    y = y_ref[...]
    bd = bd_ref[...]
    mu = _head_sum(y, bd) * (1.0 / RWKV_HD)
    yc = y - mu
    var = _head_sum(yc * yc, bd) * (1.0 / RWKV_HD)
    yn = yc * lax.rsqrt(var + RWKV_GN_EPS) * g_ref[...] + b_ref[...]
    o_ref[...] = ((yn + bon_ref[...].astype(F32)) * sg_ref[...].astype(F32)).astype(o_ref.dtype)


def _rwkv_post(y, bon, sg, g, b, bd):
    m, w = y.shape
    tb = RWKV_POST_TB
    rows = pl.BlockSpec((tb, w), lambda i: (i, 0))
    vec = pl.BlockSpec((1, w), lambda i: (0, 0))
    return pl.pallas_call(
        _rwkv_post_kernel,
        name="rwkv_post",
        grid=(m // tb,),
        in_specs=[rows, rows, rows, vec, vec, pl.BlockSpec((LANES, LANES), lambda i: (0, 0))],
        out_specs=rows,
        out_shape=jax.ShapeDtypeStruct((m, w), BF16),
        compiler_params=_cparams(("parallel",)),
    )(y, bon, sg, g.reshape(1, w), b.reshape(1, w), bd)


def _sgu_kernel(u_ref, v_ref, gate_ref, lng_ref, lnb_ref, ws_ref, bs_ref, o_ref):
    inv_sqrt2 = 0.7071067811865476

    def gelu(z):
        return 0.5 * z * (1.0 + lax.erf(z * inv_sqrt2))

    v = gelu(v_ref[...].astype(F32))
    mu = jnp.mean(v, axis=-1, keepdims=True)
    vc = v - mu
    var = jnp.mean(vc * vc, axis=-1, keepdims=True)
    vn = vc * lax.rsqrt(var + SGU_LN_EPS) * lng_ref[...] + lnb_ref[...]
    t = SGU_CHUNK
    ii = lax.broadcasted_iota(I32, (t, t), 0) // CHUNK
    jj = lax.broadcasted_iota(I32, (t, t), 1) // CHUNK
    mask = jj <= ii
    for g in range(SGU_GROUPS):
        ws = jnp.where(mask, ws_ref[g], 0.0)
        sl = slice(g * SGU_GW, (g + 1) * SGU_GW)
        bias = bs_ref[:, g:g + 1]
        for c in range(u_ref.shape[0] // t):
            rows = slice(c * t, (c + 1) * t)
            sv = _dot(ws, vn[rows, sl]) + bias
            u = gelu(u_ref[rows, sl].astype(F32))
            o_ref[rows, sl] = (u * sv * _silu(gate_ref[rows, sl].astype(F32))).astype(o_ref.dtype)


def _sgu(p, lng, lnb, ws, bs_t):
    m = p.shape[0]
    w = BRANCH_W
    tb = SGU_TB
    d0 = P_D0 // w
    return pl.pallas_call(
        _sgu_kernel,
        name="sgu",
        grid=(m // tb,),
        in_specs=[pl.BlockSpec((tb, w), lambda i: (i, d0)),
                  pl.BlockSpec((tb, w), lambda i: (i, d0 + 1)),
                  pl.BlockSpec((tb, w), lambda i: (i, d0 + 2)),
                  pl.BlockSpec((1, w), lambda i: (0, 0)),
                  pl.BlockSpec((1, w), lambda i: (0, 0)),
                  pl.BlockSpec((SGU_GROUPS, SGU_CHUNK, SGU_CHUNK), lambda i: (0, 0, 0)),
                  pl.BlockSpec((SGU_CHUNK, SGU_GROUPS), lambda i: (0, 0))],
        out_specs=pl.BlockSpec((tb, w), lambda i: (i, 0)),
        out_shape=jax.ShapeDtypeStruct((m, w), BF16),
        compiler_params=_cparams(("parallel",)),
    )(p, p, p, lng.reshape(1, w), lnb.reshape(1, w), ws, bs_t)


def _merge_kernel(ya_ref, yb_ref, yc_ref, yd_ref, ga_ref, gb_ref, gc_ref, gd_ref, wp_ref, o_ref):
    acc = None
    for g, (y_ref, g_ref) in enumerate(((ya_ref, ga_ref), (yb_ref, gb_ref), (yc_ref, gc_ref), (yd_ref, gd_ref))):
        term = jax.nn.sigmoid(g_ref[...].astype(F32)) * _dot(y_ref[...], wp_ref[g])
        acc = term if acc is None else acc + term
    o_ref[...] = acc.astype(o_ref.dtype)


def _merge(ya, yb, yc, yd, p, wproj):
    m = ya.shape[0]
    w = BRANCH_W
    tm, tn = MERGE_TM, MERGE_TN
    nj = D_MODEL // tn
    m0 = P_M0 // tn
    ysp = pl.BlockSpec((tm, w), lambda j, i: (i, 0))
    gsp = [pl.BlockSpec((tm, tn), lambda j, i, g=g: (i, m0 + g * nj + j)) for g in range(N_BRANCH)]
    return pl.pallas_call(
        _merge_kernel,
        name="merge",
        grid=(nj, m // tm),
        in_specs=[ysp, ysp, ysp, ysp, *gsp,
                  pl.BlockSpec((N_BRANCH, w, tn), lambda j, i: (0, 0, j))],
        out_specs=pl.BlockSpec((tm, tn), lambda j, i: (i, j)),
        out_shape=jax.ShapeDtypeStruct((m, D_MODEL), BF16),
        compiler_params=_cparams(("parallel", "parallel")),
    )(ya, yb, yc, yd, p, p, p, p, wproj)


def _out_kernel(mg_ref, wo_ref, g_ref, x_ref, o_ref):
    y = _dot(mg_ref[...], wo_ref[...])
    y = y * lax.rsqrt(jnp.mean(y * y, axis=-1, keepdims=True) + NORM_EPS) * g_ref[...]
    o_ref[...] = x_ref[...] + y


def _out_proj(merged, wout, g, x2):
    m, d = x2.shape
    tm = OUT_TM
    return pl.pallas_call(
        _out_kernel,
        name="out_proj",
        grid=(m // tm,),
        in_specs=[pl.BlockSpec((tm, d), lambda i: (i, 0)),
                  pl.BlockSpec((d, d), lambda i: (0, 0)),
                  pl.BlockSpec((1, d), lambda i: (0, 0)),
                  pl.BlockSpec((tm, d), lambda i: (i, 0))],
        out_specs=pl.BlockSpec((tm, d), lambda i: (i, 0)),
        out_shape=jax.ShapeDtypeStruct((m, d), F32),
        compiler_params=_cparams(("parallel",)),
    )(merged, wout, g.reshape(1, d), x2)


def _split_cols(w, sizes):
    offs = np.cumsum((0,) + tuple(sizes))
    return [w[:, int(offs[i]):int(offs[i + 1])] for i in range(len(sizes))]


def _pad_cols(w, n):
    return jnp.pad(w, ((0, 0), (0, n - w.shape[1])))


def _pad_rows(w, n):
    return jnp.pad(w, ((0, n - w.shape[0]), (0, 0)))


def _rope_tables(positions, rot_dim, width):
    inv = ROPE_THETA ** (-jnp.arange(0, rot_dim, 2, dtype=F32) / rot_dim)
    ang = positions.astype(F32).reshape(-1, 1) * inv
    cos, sin = jnp.cos(ang), jnp.sin(ang)
    m = cos.shape[0]
    rest = width - rot_dim
    cos_h = jnp.concatenate([cos, cos, jnp.ones((m, rest), F32)], axis=1)
    sin_h = jnp.concatenate([-sin, sin, jnp.zeros((m, rest), F32)], axis=1)
    reps = LANES // width
    return jnp.tile(cos_h, (1, reps)), jnp.tile(sin_h, (1, reps))


def kernel(x, positions, norm_pre, norm_post, w_in, gla_w_g2, gla_b_g, gla_head_g, rwkv_mu, rwkv_w0,
           rwkv_w_w2, rwkv_a0, rwkv_w_a2, rwkv_k_k, rwkv_k_a, rwkv_r_k, rwkv_lnx_g, rwkv_lnx_b,
           sgu_ln_g, sgu_ln_b, sgu_w_s, sgu_b_s, w_proj, w_out):
    bsz, seq, d = x.shape
    depth = w_in.shape[0]
    m = bsz * seq
    x2 = x.reshape(m, d)
    ca, sa = _rope_tables(positions, DSA_ROT, DSA_HD)
    ci, si = _rope_tables(positions, IDX_ROT, IDX_HD)
    lane_grp = np.arange(LANES) // RWKV_HD
    bd = jnp.asarray((lane_grp[:, None] == lane_grp[None, :]).astype(np.float32))

    for l in range(depth):
        wa, wb, wc, wd, wm = _split_cols(w_in[l], GROUP_COLS)
        a_q, a_k, a_v, a_glr, a_gate = _split_cols(wa, A_COLS)
        b_q, b_k, b_v, b_iq, b_ik, b_iw, b_gate = _split_cols(wb, B_COLS)
        c_r, c_k, c_v, c_wlr, c_alr, c_gate = _split_cols(wc, C_COLS)
        w_main = jnp.concatenate([t.astype(BF16) for t in (wm, b_q, b_k, b_v, b_gate, c_r, c_k, c_v, c_gate, wd,
                                                            a_q, a_k, a_v, a_gate)], axis=1)
        w_small = jnp.concatenate([b_iq] + [_pad_cols(t, LANES) for t in (b_ik, b_iw, c_wlr, c_alr, a_glr)],
                                  axis=1).astype(BF16)

        p, h = _norm_matmul(x2, norm_pre[l], w_main, MM_TN)
        ps = _matmul(h, w_small, S_COLS, F32)

        ya = _gla(p, ps, _pad_rows(gla_w_g2[l], LANES), gla_b_g[l], gla_head_g[l], bsz, seq)

        qr, kr, vt, iqr, ikr, iws = _dsa_prep(p, ps, ca, sa, ci, si)
        yb = _dsa(qr, kr, vt, iqr, ikr, iws, p, bsz, seq)

        mu_r, mu_k, mu_v, mu_wlr, mu_alr, mu_gate = _split_cols(rwkv_mu[l][None, :], C_COLS)
        mu_main = jnp.concatenate([mu_r, mu_k, mu_v, mu_gate], axis=1)
        mu_s = jnp.concatenate([_pad_cols(mu_wlr, LANES), _pad_cols(mu_alr, LANES)], axis=1)
        r, wl, k2, v, kk, a, sg, bon = _rwkv_prep(
            p, ps, mu_main, mu_s, rwkv_w0[l], _pad_rows(rwkv_w_w2[l], LANES), rwkv_a0[l],
            _pad_rows(rwkv_w_a2[l], LANES), rwkv_k_k[l], rwkv_k_a[l], rwkv_r_k[l].reshape(-1), bd, seq)
        y = _rwkv_scan(r, wl, k2, v, kk, a, bsz, seq)
        yc = _rwkv_post(y, bon, sg, rwkv_lnx_g[l], rwkv_lnx_b[l], bd)

        yd = _sgu(p, sgu_ln_g[l], sgu_ln_b[l], sgu_w_s[l], sgu_b_s[l].T)

        merged = _merge(ya, yb, yc, yd, p, w_proj[l].astype(BF16))
        x2 = _out_proj(merged, w_out[l].astype(BF16), norm_post[l], x2)
    return x2.reshape(bsz, seq, d)
```

```python
import functools

import numpy as np
import jax
import jax.numpy as jnp
from jax import lax
from jax.experimental import pallas as pl
from jax.experimental.pallas import tpu as pltpu

F32 = jnp.float32
BF16 = jnp.bfloat16
I32 = jnp.int32
HIGHEST = lax.Precision.HIGHEST

D_MODEL = 2048
CHUNK = 64
NORM_EPS = 1e-6
ROPE_THETA = 500000.0
N_BRANCH = 4
BRANCH_W = D_MODEL // 2

GLA_HEADS = 4
GLA_DK = BRANCH_W // 2 // GLA_HEADS
GLA_DV = BRANCH_W // GLA_HEADS
GLA_RANK = 16
GLA_TAU = 16.0

DSA_HEADS = 8
DSA_HD = BRANCH_W // DSA_HEADS
DSA_ROT = DSA_HD // 4
IDX_HEADS = 8
IDX_HD = 64
IDX_ROT = IDX_HD // 4
TOPK_MAX = 256

RWKV_HD = 64
RWKV_HEADS = BRANCH_W // RWKV_HD
RWKV_DECAY_RANK = 96
RWKV_A_RANK = 96
RWKV_DECAY_SCALE = 0.606531
RWKV_GN_EPS = 64e-5
RWKV_KK_EPS = 1e-12

SGU_CHUNK = 128
SGU_GROUPS = 8
SGU_GW = BRANCH_W // SGU_GROUPS
SGU_LN_EPS = 1e-5

A_COLS = (GLA_HEADS * GLA_DK, GLA_HEADS * GLA_DK, BRANCH_W, GLA_RANK, BRANCH_W)
B_COLS = (BRANCH_W, BRANCH_W, BRANCH_W, IDX_HEADS * IDX_HD, IDX_HD, IDX_HEADS, BRANCH_W)
C_COLS = (BRANCH_W, BRANCH_W, BRANCH_W, RWKV_DECAY_RANK, RWKV_A_RANK, BRANCH_W)
D_COLS = (2 * BRANCH_W, BRANCH_W)
GROUP_COLS = (sum(A_COLS), sum(B_COLS), sum(C_COLS), sum(D_COLS), N_BRANCH * D_MODEL)

LANES = 128
VMEM_LIMIT = 56 * 1024 * 1024
MXU_DEPTH = 256
IDX_K = MXU_DEPTH
NEG_BIG = -1e30
LOG2E = 1.4426950408889634
INT_MIN = -(2 ** 31)

P_M0 = 0
P_B0 = P_M0 + N_BRANCH * D_MODEL
P_C0 = P_B0 + 4 * BRANCH_W
P_D0 = P_C0 + 4 * BRANCH_W
P_A0 = P_D0 + 3 * BRANCH_W
P_COLS = P_A0 + 2 * GLA_HEADS * GLA_DK + 2 * BRANCH_W
S_IDX0 = 0
S_IDX_COLS = IDX_HEADS * IDX_HD + 2 * 128
S_RW0 = S_IDX0 + S_IDX_COLS
S_GLR0 = S_RW0 + 2 * 128
S_COLS = S_GLR0 + 128

MM_TM = 1024
MM_TN = 1024
GLA_TB = 512
DSA_PREP_TB = 512
DSA_TQ = 256
DSA_KB = 512
DSA_VROWS = DSA_HD + 16
RWKV_PREP_TB = 256
RWKV_L = 64
RWKV_UNROLL = 8
RWKV_POST_TB = 512
SGU_TB = 512
MERGE_TM = 512
MERGE_TN = 1024
OUT_TM = 512


def _cparams(sem):
    return pltpu.CompilerParams(dimension_semantics=sem, vmem_limit_bytes=VMEM_LIMIT)


def _dot(a, b, precision=None):
    return jnp.dot(a, b, preferred_element_type=F32, precision=precision)


def _dot_nt(a, b, precision=None):
    return lax.dot_general(a, b, (((1,), (1,)), ((), ())), preferred_element_type=F32,
                           precision=precision)


def _dot_tn(a, b, precision=None):
    return lax.dot_general(a, b, (((0,), (0,)), ((), ())), preferred_element_type=F32,
                           precision=precision)


def _bf16_terms(x, n):
    terms = []
    for _ in range(n):
        t = x.astype(BF16)
        terms.append(t)
        x = x - t.astype(F32)
    return terms


def _dot_exact_lhs(a, b, nterms):
    a16 = a.astype(BF16)
    out = None
    for t in _bf16_terms(b, nterms):
        d = _dot(a16, t)
        out = d if out is None else out + d
    return out


def _dot_exact_rhs(a, b, nterms):
    b16 = b.astype(BF16)
    out = None
    for t in _bf16_terms(a, nterms):
        d = _dot(t, b16)
        out = d if out is None else out + d
    return out


def _dot_3pass(a, b):
    a_hi, a_lo = _bf16_terms(a, 2)
    b_hi, b_lo = _bf16_terms(b, 2)
    return _dot(a_hi, b_hi) + _dot(a_lo, b_hi) + _dot(a_hi, b_lo)


def _silu(x):
    return x * jax.nn.sigmoid(x)


def _norm_matmul_kernel(x_ref, g_ref, b_ref, o_ref, h_ref):
    @pl.when(pl.program_id(1) == 0)
    def _():
        x = x_ref[...]
        ms = jnp.mean(x * x, axis=-1, keepdims=True)
        h_ref[...] = (x * lax.rsqrt(ms + NORM_EPS) * g_ref[...]).astype(h_ref.dtype)

    o_ref[...] = _dot(h_ref[...], b_ref[...]).astype(o_ref.dtype)


def _norm_matmul(x2, g, b, tn):
    m, k = x2.shape
    n = b.shape[1]
    tm = min(MM_TM, m)
    return pl.pallas_call(
        _norm_matmul_kernel,
        name="proj",
        grid=(m // tm, n // tn),
        in_specs=[pl.BlockSpec((tm, k), lambda i, j: (i, 0)),
                  pl.BlockSpec((1, k), lambda i, j: (0, 0)),
                  pl.BlockSpec((k, tn), lambda i, j: (0, j))],
        out_specs=[pl.BlockSpec((tm, tn), lambda i, j: (i, j)),
                   pl.BlockSpec((tm, k), lambda i, j: (i, 0))],
        out_shape=[jax.ShapeDtypeStruct((m, n), BF16), jax.ShapeDtypeStruct((m, k), BF16)],
        compiler_params=_cparams(("parallel", "arbitrary")),
    )(x2, g.reshape(1, k), b)


def _matmul_kernel(a_ref, b_ref, o_ref):
    o_ref[...] = _dot(a_ref[...], b_ref[...]).astype(o_ref.dtype)


def _matmul(a, b, tn, out_dtype=F32):
    m, k = a.shape
    n = b.shape[1]
    tm = min(MM_TM, m)
    return pl.pallas_call(
        _matmul_kernel,
        name="proj",
        grid=(m // tm, n // tn),
        in_specs=[pl.BlockSpec((tm, k), lambda i, j: (i, 0)),
                  pl.BlockSpec((k, tn), lambda i, j: (0, j))],
        out_specs=pl.BlockSpec((tm, tn), lambda i, j: (i, j)),
        out_shape=jax.ShapeDtypeStruct((m, n), out_dtype),
        compiler_params=_cparams(("parallel", "parallel")),
    )(a, b)


def _gla_kernel(q_ref, k_ref, v_ref, gate_ref, glr_ref, wg2_ref, bg_ref, hg_ref, o_ref, state_ref):
    @pl.when(pl.program_id(2) == 0)
    def _():
        state_ref[...] = jnp.zeros_like(state_ref)

    ii = lax.broadcasted_iota(I32, (CHUNK, CHUNK), 0)
    jj = lax.broadcasted_iota(I32, (CHUNK, CHUNK), 1)
    tril = (ii >= jj).astype(F32)
    chunks = [pl.ds(c * CHUNK, CHUNK) for c in range(GLA_TB // CHUNK)]
    z = _dot_3pass(glr_ref[...], wg2_ref[...]) + bg_ref[...]
    log_a = (jnp.minimum(z, 0.0) - jnp.log1p(jnp.exp(-jnp.abs(z)))) / GLA_TAU
    cum = [_dot_exact_lhs(tril, log_a[c * CHUNK:(c + 1) * CHUNK], 3) for c in range(len(chunks))]
    total = [x[CHUNK - 1:CHUNK, :] for x in cum]
    k_dec = [k_ref[sl, :].astype(F32) * jnp.exp(t - x) for sl, t, x in zip(chunks, total, cum)]
    kv = [_dot_tn(v_ref[sl, :], kd.astype(BF16)) for sl, kd in zip(chunks, k_dec)]
    states = []
    st = state_ref[...]
    for t, x in zip(total, kv):
        st = st * jnp.exp(t) + x
        states.append(st)
    state_ref[...] = st
    outs = [_dot_nt((q_ref[sl, :].astype(F32) * (GLA_DK ** -0.5)).astype(BF16), s.astype(BF16))
            for sl, s in zip(chunks, states)]
    for sl, o in zip(chunks, outs):
        o = o * lax.rsqrt(jnp.mean(o * o, axis=-1, keepdims=True) + NORM_EPS) * hg_ref[...]
        o_ref[sl, :] = (o * _silu(gate_ref[sl, :].astype(F32))).astype(o_ref.dtype)


def _gla(p, ps, wg2p, bg, hg, bsz, seq):
    m = p.shape[0]
    nt = seq // GLA_TB
    row = lambda b, h, t: b * nt + t
    qb = P_A0 // GLA_DK
    kb = qb + GLA_HEADS
    vb = (P_A0 + 2 * GLA_HEADS * GLA_DK) // GLA_DV
    gb = vb + GLA_HEADS
    lb = S_GLR0 // LANES
    return pl.pallas_call(
        _gla_kernel,
        name="gla",
        grid=(bsz, GLA_HEADS, nt),
        in_specs=[
            pl.BlockSpec((GLA_TB, GLA_DK), lambda b, h, t: (row(b, h, t), qb + h)),
            pl.BlockSpec((GLA_TB, GLA_DK), lambda b, h, t: (row(b, h, t), kb + h)),
            pl.BlockSpec((GLA_TB, GLA_DV), lambda b, h, t: (row(b, h, t), vb + h)),
            pl.BlockSpec((GLA_TB, GLA_DV), lambda b, h, t: (row(b, h, t), gb + h)),
            pl.BlockSpec((GLA_TB, LANES), lambda b, h, t: (row(b, h, t), lb)),
            pl.BlockSpec((LANES, GLA_DK), lambda b, h, t: (0, h)),
            pl.BlockSpec((1, GLA_DK), lambda b, h, t: (0, h)),
            pl.BlockSpec((1, GLA_DV), lambda b, h, t: (0, 0)),
        ],
        out_specs=pl.BlockSpec((GLA_TB, GLA_DV), lambda b, h, t: (row(b, h, t), h)),
        out_shape=jax.ShapeDtypeStruct((m, BRANCH_W), BF16),
        scratch_shapes=[pltpu.VMEM((GLA_DV, GLA_DK), F32)],
        compiler_params=_cparams(("parallel", "parallel", "arbitrary")),
    )(p, p, p, p, ps, wg2p, bg.reshape(1, -1), hg.reshape(1, -1))


def _rope_tile(x, cos_t, sin_t, half, width):
    lane = lax.broadcasted_iota(I32, x.shape, 1) % width
    partner = jnp.where(lane < half, pltpu.roll(x, LANES - half, 1), pltpu.roll(x, half, 1))
    return x * cos_t + partner * sin_t


def _dsa_prep_kernel(q_ref, k_ref, v_ref, idx_ref, ca_ref, sa_ref, ci_ref, si_ref,
                     qo_ref, ko_ref, vo_ref, iqo_ref, iko_ref, iwo_ref):
    ca, sa = ca_ref[...], sa_ref[...]
    ci, si = ci_ref[...], si_ref[...]
    for h in range(DSA_HEADS):
        sl = slice(h * DSA_HD, (h + 1) * DSA_HD)
        q = _rope_tile(q_ref[:, sl].astype(F32), ca, sa, DSA_ROT // 2, DSA_HD) * (DSA_HD ** -0.5 * LOG2E)
        qo_ref[:, sl] = q.astype(qo_ref.dtype)
        ko_ref[:, sl] = _rope_tile(k_ref[:, sl].astype(F32), ca, sa, DSA_ROT // 2, DSA_HD).astype(ko_ref.dtype)
        vo_ref[0, h * DSA_VROWS:h * DSA_VROWS + DSA_HD, :] = v_ref[:, sl].astype(F32).T.astype(vo_ref.dtype)
        extra = lax.broadcasted_iota(I32, (DSA_VROWS - DSA_HD, q_ref.shape[0]), 0) == 0
        vo_ref[0, h * DSA_VROWS + DSA_HD:(h + 1) * DSA_VROWS, :] = jnp.where(extra, 1.0, 0.0).astype(vo_ref.dtype)
    niq = IDX_HEADS * IDX_HD
    low = lax.broadcasted_iota(I32, (q_ref.shape[0], LANES), 1) < IDX_HD

    def hi_lo(x):
        hi = x.astype(BF16).astype(F32)
        return hi, x - hi

    for j in range(niq // LANES):
        x = _rope_tile(idx_ref[:, j * LANES:(j + 1) * LANES], ci, si, IDX_ROT // 2, IDX_HD)
        hi, lo = hi_lo(x)
        hi_sw = pltpu.roll(hi, IDX_HD, 1)
        lo_sw = pltpu.roll(lo, IDX_HD, 1)
        base = 2 * j * IDX_K
        iqo_ref[:, base:base + LANES] = jnp.where(low, hi, hi_sw).astype(BF16)
        iqo_ref[:, base + LANES:base + IDX_K] = jnp.where(low, lo, 0.0).astype(BF16)
        iqo_ref[:, base + IDX_K:base + IDX_K + LANES] = jnp.where(low, hi_sw, hi).astype(BF16)
        iqo_ref[:, base + IDX_K + LANES:base + 2 * IDX_K] = jnp.where(low, lo_sw, 0.0).astype(BF16)
    hi, lo = hi_lo(_rope_tile(idx_ref[:, niq:niq + LANES], ci, si, IDX_ROT // 2, IDX_HD))
    iko_ref[:, :LANES] = jnp.where(low, hi, pltpu.roll(lo, IDX_HD, 1)).astype(BF16)
    iko_ref[:, LANES:] = jnp.where(low, hi, 0.0).astype(BF16)
    iwo_ref[...] = idx_ref[:, niq + LANES:niq + 2 * LANES] * (niq ** -0.5)


def _dsa_prep(p, ps, ca, sa, ci, si):
    m = p.shape[0]
    tb = DSA_KB
    w = BRANCH_W
    rows = lambda i: (i, 0)
    b0 = P_B0 // w
    return pl.pallas_call(
        _dsa_prep_kernel,
        name="dsa_prep",
        grid=(m // tb,),
        in_specs=[pl.BlockSpec((tb, w), lambda i: (i, b0)),
                  pl.BlockSpec((tb, w), lambda i: (i, b0 + 1)),
                  pl.BlockSpec((tb, w), lambda i: (i, b0 + 2)),
                  pl.BlockSpec((tb, S_IDX_COLS), lambda i: (i, S_IDX0 // S_IDX_COLS)),
                  pl.BlockSpec((tb, LANES), rows), pl.BlockSpec((tb, LANES), rows),
                  pl.BlockSpec((tb, LANES), rows), pl.BlockSpec((tb, LANES), rows)],
        out_specs=[pl.BlockSpec((tb, w), rows), pl.BlockSpec((tb, w), rows),
                   pl.BlockSpec((1, DSA_HEADS * DSA_VROWS, tb), lambda i: (i, 0, 0)),
                   pl.BlockSpec((tb, IDX_HEADS * IDX_K), rows), pl.BlockSpec((tb, IDX_K), rows),
                   pl.BlockSpec((tb, LANES), rows)],
        out_shape=[jax.ShapeDtypeStruct((m, w), BF16), jax.ShapeDtypeStruct((m, w), BF16),
                   jax.ShapeDtypeStruct((m // tb, DSA_HEADS * DSA_VROWS, tb), BF16),
                   jax.ShapeDtypeStruct((m, IDX_HEADS * IDX_K), BF16),
                   jax.ShapeDtypeStruct((m, IDX_K), BF16), jax.ShapeDtypeStruct((m, LANES), F32)],
        compiler_params=_cparams(("parallel",)),
    )(p, p, p, ps, ca, sa, ci, si)


def _dsa_kernel(q_ref, k_ref, vt_ref, iq_ref, ik_ref, iw_ref, gate_ref, o_ref, key_scr, hi_scr, tie_scr,
                *, topk):
    tq, kb_w = DSA_TQ, DSA_KB
    q0 = pl.program_id(1) * tq
    nkb = (q0 + tq + kb_w - 1) // kb_w
    qcol = lax.broadcasted_iota(I32, (1, tq), 1)
    qlim = ((q0 + qcol) // CHUNK + 1) * CHUNK
    sub_k = lax.broadcasted_iota(I32, (kb_w, 1), 0)
    sub = 8

    def score_body(kb, carry):
        ik = ik_ref[pl.ds(pl.multiple_of(kb * kb_w, kb_w), kb_w), :]
        dots = [_dot_nt(ik, iq_ref[:, h * IDX_K:(h + 1) * IDX_K]) for h in range(IDX_HEADS)]
        acc = jnp.zeros((kb_w, tq), F32)
        for h, d in enumerate(dots):
            acc = acc + iw_ref[h:h + 1, :] * jnp.maximum(d, 0.0)
        acc = acc + 0.0
        bits = lax.bitcast_convert_type(acc, I32)
        key = bits ^ ((bits >> 31) & 0x7FFFFFFF)
        sidx = kb * kb_w + sub_k
        key = jnp.where(sidx < qlim, key, INT_MIN)
        key_scr[kb] = key
        hi_scr[kb] = (key >> 16).astype(jnp.int16)
        return carry

    lax.fori_loop(0, nkb, score_body, 0)

    def count_tiles(load, pred, rows_per_vreg, dtype):
        def body(kb, c):
            m = pred(load(kb), kb * kb_w + sub_k).astype(dtype)
            if dtype == I32:
                nparts = 8
                rows_part = kb_w // nparts
                parts = [jnp.sum(m[j * rows_part:(j + 1) * rows_part]
                                 .reshape(rows_part // rows_per_vreg, rows_per_vreg, tq), axis=0)
                         for j in range(nparts)]
            else:
                parts = [m[j * rows_per_vreg:(j + 1) * rows_per_vreg] for j in range(kb_w // rows_per_vreg)]
            while len(parts) > 1:
                parts = [a + b for a, b in zip(parts[::2], parts[1::2])]
            return c + parts[0]
        c = lax.fori_loop(0, nkb, body, jnp.zeros((rows_per_vreg, tq), dtype))
        return jnp.sum(c.astype(I32), axis=0, keepdims=True)

    def count(pred):
        return count_tiles(lambda kb: key_scr[kb], pred, sub, I32)

    def count_hi(pred):
        return count_tiles(lambda kb: hi_scr[kb], pred, 2 * sub, jnp.int16)

    def hi_body(it, carry):
        res, cnt_res = carry
        trial = res | lax.shift_left(jnp.int32(1), 31 - it)
        cand_hi = ((trial ^ INT_MIN) >> 16).astype(jnp.int16)
        cnt = count_hi(lambda kt, s: kt >= cand_hi)
        ok = cnt >= topk
        return jnp.where(ok, trial, res), jnp.where(ok, cnt, cnt_res)

    res_hi, cnt_ge_hi = lax.fori_loop(0, 16, hi_body,
                                      (jnp.zeros((1, tq), I32), jnp.full((1, tq), nkb * kb_w, I32)))
    thr_hi = ((res_hi ^ INT_MIN) >> 16).astype(jnp.int16)
    cnt_gt_hi = count_hi(lambda kt, s: kt > thr_hi)
    lo_min = -(2 ** 15)

    def lo_prep(kb, carry):
        lo = ((key_scr[kb] & 0xFFFF) + lo_min).astype(jnp.int16)
        hi_scr[kb] = jnp.where(hi_scr[kb] == thr_hi, lo, jnp.int16(lo_min))
        return carry

    lax.fori_loop(0, nkb, lo_prep, 0)
    places = topk - cnt_gt_hi

    def lo_body(it, carry):
        res, cnt_res = carry
        trial = res | lax.shift_left(jnp.int32(1), 15 - it)
        cand_lo = (trial + lo_min).astype(jnp.int16)
        cnt = count_hi(lambda kt, s: kt >= cand_lo)
        ok = cnt >= places
        return jnp.where(ok, trial, res), jnp.where(ok, cnt, cnt_res)

    res_lo, cnt_ge_lo = lax.fori_loop(0, 16, lo_body, (jnp.zeros((1, tq), I32), cnt_ge_hi - cnt_gt_hi))
    thr = (res_hi ^ INT_MIN) | res_lo
    cnt_ge = cnt_gt_hi + cnt_ge_lo
    seq_len = key_scr.shape[0] * kb_w
    tie_scr[...] = jnp.full(tie_scr.shape, seq_len, I32)
    has_tie = (cnt_ge > topk) & (thr != INT_MIN)

    @pl.when(jnp.max(jnp.where(has_tie, 1.0, 0.0)) > 0.5)
    def _():
        need = topk - count(lambda kt, s: kt > thr)
        nbits = max(1, int(np.ceil(np.log2(seq_len))))

        def tie_body(it, resj):
            trial = resj | lax.shift_left(jnp.int32(1), nbits - 1 - it)
            c = count(lambda kt, s: (kt == thr) & (s < trial))
            return jnp.where(c < need, trial, resj)

        tie_scr[...] = lax.fori_loop(0, nbits, tie_body, jnp.zeros((1, tq), I32))

    last_tie = tie_scr[...]

    def bias_body(kb, carry):
        kt = key_scr[kb]
        sidx = kb * kb_w + sub_k
        sel = ((kt > thr) | ((kt == thr) & (sidx <= last_tie))) & (kt != INT_MIN)
        key_scr[kb] = lax.bitcast_convert_type(jnp.where(sel, 0.0, NEG_BIG), I32)
        return carry

    lax.fori_loop(0, nkb, bias_body, 0)

    heads = [slice(h * DSA_HD, (h + 1) * DSA_HD) for h in range(DSA_HEADS)]

    def att_body(kb, stats):
        rows = pl.ds(pl.multiple_of(kb * kb_w, kb_w), kb_w)
        bias = lax.bitcast_convert_type(key_scr[kb], F32).astype(BF16)
        s = [_dot_nt(k_ref[rows, sl], q_ref[:, sl]).astype(BF16) + bias for sl in heads]
        m_n = [jnp.maximum(c[0], jnp.max(x, axis=0, keepdims=True).astype(F32)) for c, x in zip(stats, s)]
        p = [jnp.exp2(x - m.astype(BF16)) for x, m in zip(s, m_n)]
        pv = [_dot(vt_ref[kb, h * DSA_VROWS:(h + 1) * DSA_VROWS, :], x) for h, x in enumerate(p)]
        alpha = [jnp.exp2(c[0] - m) for c, m in zip(stats, m_n)]
        return tuple((m, a * c[1] + y) for m, a, c, y in zip(m_n, alpha, stats, pv))

    init = tuple((jnp.full((1, tq), NEG_BIG, F32), jnp.zeros((DSA_VROWS, tq), F32)) for _ in heads)
    final = lax.fori_loop(0, nkb, att_body, init)
    for sl, (_, acc_f) in zip(heads, final):
        o = (acc_f[:DSA_HD] / acc_f[DSA_HD:DSA_HD + 1]).T
        o_ref[:, sl] = (o * _silu(gate_ref[:, sl].astype(F32))).astype(o_ref.dtype)


def _dsa(qr, kr, vt, iqr, ikr, iws, p, bsz, seq):
    m = qr.shape[0]
    w = BRANCH_W
    nq = seq // DSA_TQ
    nkb = seq // DSA_KB
    topk = min(TOPK_MAX, seq // 4)
    qrow = lambda b, i: (b * nq + i, 0)
    return pl.pallas_call(
        functools.partial(_dsa_kernel, topk=topk),
        name="dsa",
        grid=(bsz, nq),
        in_specs=[pl.BlockSpec((DSA_TQ, w), qrow),
                  pl.BlockSpec((seq, w), lambda b, i: (b, 0), pipeline_mode=pl.Buffered(1)),
                  pl.BlockSpec((nkb, DSA_HEADS * DSA_VROWS, DSA_KB), lambda b, i: (b, 0, 0),
                               pipeline_mode=pl.Buffered(1)),
                  pl.BlockSpec((DSA_TQ, IDX_HEADS * IDX_K), qrow),
                  pl.BlockSpec((seq, IDX_K), lambda b, i: (b, 0), pipeline_mode=pl.Buffered(1)),
                  pl.BlockSpec((IDX_HEADS, DSA_TQ), lambda b, i: (0, b * nq + i)),
                  pl.BlockSpec((DSA_TQ, w), lambda b, i: (b * nq + i, P_B0 // w + 3))],
        out_specs=pl.BlockSpec((DSA_TQ, w), qrow),
        out_shape=jax.ShapeDtypeStruct((m, w), BF16),
        scratch_shapes=[pltpu.VMEM((nkb, DSA_KB, DSA_TQ), I32),
                        pltpu.VMEM((nkb, DSA_KB, DSA_TQ), jnp.int16),
                        pltpu.VMEM((1, DSA_TQ), I32)],
        compiler_params=_cparams(("parallel", "arbitrary")),
    )(qr, kr, vt, iqr, ikr, iws[:, :IDX_HEADS].T, p)


def _head_sum(x, bd):
    parts = [_dot_exact_rhs(x[:, j * LANES:(j + 1) * LANES], bd, 2) for j in range(x.shape[1] // LANES)]
    return jnp.concatenate(parts, axis=1)


def _rwkv_prep_kernel(c_ref, cs_ref, pc_ref, pcs_ref, mu_ref, mus_ref, w0_ref, ww2_ref, a0_ref, wa2_ref,
                      kkw_ref, ka_ref, rk_ref, bd_ref,
                      r_o, wl_o, k_o, v_o, kk_o, a_o, sg_o, bon_o, *, blocks_per_seq):
    tb = c_ref.shape[0]
    first = (pl.program_id(0) % blocks_per_seq) == 0
    rowi = lax.broadcasted_iota(I32, (tb, 1), 0)

    def shift_lerp(c, prev_rows, mu):
        last = prev_rows.shape[0] - 1
        prev = jnp.where(first, 0.0, prev_rows[last:last + 1, :])
        sh = jnp.where(rowi == 0, prev, pltpu.roll(c, 1, 0))
        return c + (sh - c) * mu

    w = BRANCH_W
    c = shift_lerp(c_ref[...].astype(F32), pc_ref[...].astype(F32), mu_ref[...])
    cs = shift_lerp(cs_ref[...], pcs_ref[...], mus_ref[...])
    r, k, v, gate = c[:, :w], c[:, w:2 * w], c[:, 2 * w:3 * w], c[:, 3 * w:]
    w_log = -RWKV_DECAY_SCALE * jax.nn.sigmoid(w0_ref[...] + _dot_3pass(jnp.tanh(cs[:, :LANES]), ww2_ref[...]))
    a = jax.nn.sigmoid(a0_ref[...] + _dot_3pass(cs[:, LANES:], wa2_ref[...]))
    kk = k * kkw_ref[...]
    k2 = k * (1.0 + (a - 1.0) * ka_ref[...])
    bd = bd_ref[...]
    kk = kk * lax.rsqrt(_head_sum(kk * kk, bd) + RWKV_KK_EPS)
    r_o[...] = r.astype(r_o.dtype)
    wl_o[...] = w_log
    k_o[...] = k2.astype(k_o.dtype)
    v_o[...] = v.astype(v_o.dtype)
    kk_o[...] = kk.astype(kk_o.dtype)
    a_o[...] = a.astype(a_o.dtype)
    sg_o[...] = _silu(gate).astype(sg_o.dtype)
    bon_o[...] = (_head_sum(r * k2 * rk_ref[...], bd) * v).astype(bon_o.dtype)


def _rwkv_prep(p, ps, mu_main, mu_s, w0, ww2p, a0, wa2p, kkw, ka, rk, bd, seq):
    m = p.shape[0]
    tb = RWKV_PREP_TB
    w = BRANCH_W
    pr16, pr8 = 16, 8
    cblk, sblk = P_C0 // (4 * w), S_RW0 // (2 * LANES)
    const = lambda i: (0, 0)
    vec = pl.BlockSpec((1, w), const)
    out = pl.BlockSpec((tb, w), lambda i: (i, 0))
    dts = [BF16, F32, BF16, BF16, BF16, BF16, BF16, BF16]
    return pl.pallas_call(
        functools.partial(_rwkv_prep_kernel, blocks_per_seq=seq // tb),
        name="rwkv_prep",
        grid=(m // tb,),
        in_specs=[pl.BlockSpec((tb, 4 * w), lambda i: (i, cblk)),
                  pl.BlockSpec((tb, 2 * LANES), lambda i: (i, sblk)),
                  pl.BlockSpec((pr16, 4 * w), lambda i: (jnp.maximum(i * (tb // pr16) - 1, 0), cblk)),
                  pl.BlockSpec((pr8, 2 * LANES), lambda i: (jnp.maximum(i * (tb // pr8) - 1, 0), sblk)),
                  pl.BlockSpec((1, 4 * w), const), pl.BlockSpec((1, 2 * LANES), const),
                  vec, pl.BlockSpec((LANES, w), const), vec, pl.BlockSpec((LANES, w), const),
                  vec, vec, vec, pl.BlockSpec((LANES, LANES), const)],
        out_specs=[out] * 8,
        out_shape=[jax.ShapeDtypeStruct((m, w), dt) for dt in dts],
        compiler_params=_cparams(("parallel",)),
    )(p, ps, p, ps, mu_main, mu_s, w0.reshape(1, w), ww2p, a0.reshape(1, w), wa2p,
      kkw.reshape(1, w), ka.reshape(1, w), rk.reshape(1, w), bd)


def _rwkv_scan_kernel(r_ref, wl_ref, k_ref, v_ref, kk_ref, a_ref, y_ref, lin_scr, yq_scr, h_scr, *, unroll, nseq):
    n = RWKV_L
    two = 2 * n
    nc = r_ref.shape[0] // n
    i2 = lax.broadcasted_iota(I32, (two, two), 0)
    j2 = lax.broadcasted_iota(I32, (two, two), 1)
    same_head = (i2 // RWKV_HD) == (j2 // RWKV_HD)
    eye2 = (i2 == j2).astype(F32)
    ti = lax.broadcasted_iota(I32, (n, LANES), 0)
    ci = lax.broadcasted_iota(I32, (n, LANES), 1) % n
    strict = ti > ci
    incl = ti >= ci
    same16 = (ti // 16) == (ci // 16)
    eye = (ti == ci).astype(F32)
    ii = lax.broadcasted_iota(I32, (n, n), 0)
    jj = lax.broadcasted_iota(I32, (n, n), 1)
    tril = (ii >= jj).astype(F32)
    head0 = lax.broadcasted_iota(I32, (n, LANES), 1) < RWKV_HD

    def mm(a, b):
        return _dot(a.astype(BF16), b.astype(BF16))

    def mm_nt(a, b):
        return _dot_nt(a.astype(BF16), b.astype(BF16))

    def mm_tn(a, b):
        return _dot_tn(a.astype(BF16), b.astype(BF16))

    def stack(x):
        return jnp.concatenate([jnp.where(head0, x, 0.0), jnp.where(head0, 0.0, x)], axis=0)

    def fold(x):
        return x[:n] + x[n:]

    def each(f, *lists):
        return [f(*args) for args in zip(*lists)]

    def tri_inv(a_low):
        a_d = each(lambda a: jnp.where(same16, a, 0.0), a_low)
        a_o = each(lambda a, d: a - d, a_low, a_d)
        ad_st = each(stack, a_d)
        a2 = each(mm, a_d, ad_st)
        tick()
        a34 = each(lambda s, d: mm(s, jnp.concatenate([d, stack(s)], axis=1)), a2, ad_st)
        tick()
        a4_st = each(lambda x: stack(x[:, LANES:]), a34)
        s4 = each(lambda d, s, x: eye + d + s + x[:, :LANES], a_d, a2, a34)
        a8 = each(lambda x, y: mm(x[:, LANES:], y), a34, a4_st)
        tick()
        s8 = each(lambda s, x: s + mm(s, x), s4, a4_st)
        tick()
        t_d = each(lambda s, x: s + mm(s, stack(x)), s8, a8)
        tick()
        nn = each(lambda t, o: mm(t, stack(o)), t_d, a_o)
        tick()
        nx = each(lambda x, t: mm(x, jnp.concatenate([stack(x), stack(t)], axis=1)), nn, t_d)
        tick()
        x1 = each(lambda t, x: t + x[:, LANES:], t_d, nx)
        return each(lambda x, y: y + mm(x[:, :LANES], stack(y)), nx, x1)

    pending = []

    def tick():
        for _ in range(-(-unroll // 8)):
            if pending:
                pending.pop(0)()

    def build_group(g):
        rows = [pl.ds(pl.multiple_of((g * unroll + u) * n, n), n) for u in range(unroll)]
        wl = [wl_ref[r, :] for r in rows]
        k = [k_ref[r, :].astype(F32) for r in rows]
        v = [v_ref[r, :].astype(F32) for r in rows]
        kk = [kk_ref[r, :].astype(F32) for r in rows]
        beta = each(lambda x, r: x * a_ref[r, :].astype(F32), kk, rows)
        cum = each(lambda x: _dot_exact_lhs(tril, x, 3), wl)
        cum_end = each(lambda x: x[n - 1:n, :], cum)
        g_inv = each(lambda x: jnp.exp(-x), cum)
        g_end = each(lambda e, x: jnp.exp(e - x), cum_end, cum)
        a_t = each(lambda x, c, w: -x * jnp.exp(c - w), kk, cum, wl)
        r_t = each(lambda r, c: r_ref[r, :].astype(F32) * jnp.exp(c), rows, cum)
        bk_st = each(lambda x, y, z: jnp.concatenate([stack(x * z), stack(y * z)], axis=0), beta, k, g_inv)
        ar = each(lambda x, y: jnp.concatenate([x, y], axis=0), a_t, r_t)
        gram = each(mm_nt, ar, bk_st)
        a_ab = each(lambda x: jnp.where(strict, x[:n, :LANES], 0.0), gram)
        a_rb = each(lambda x: jnp.where(incl, x[n:, :LANES], 0.0), gram)
        a_ak = each(lambda x: jnp.where(strict, x[:n, LANES:], 0.0), gram)
        a_rk = each(lambda x: jnp.where(incl, x[n:, LANES:], 0.0), gram)
        av2 = each(lambda x, y, z: mm(jnp.concatenate([x, y], axis=0), stack(z)), a_ak, a_rk, v)
        t_inv = tri_inv(a_ab)
        tick()
        pq = each(lambda t, x, y: mm(t, jnp.concatenate([stack(x), stack(y[:n])], axis=1)), t_inv, a_t, av2)
        yr = each(lambda x, y: mm(x, jnp.concatenate([stack(y[:, :LANES]), stack(y[:, LANES:])], axis=1)), a_rb, pq)
        kv = each(lambda x, y, z: mm_tn(x * y, z), k, g_end, v)
        gh = each(lambda x, y, z: mm_tn(x * y, z), beta, g_end, pq)
        while pending:
            tick()
        for u in range(unroll):
            ry = r_t[u] + yr[u][:, :LANES]
            g_bd = jnp.where(same_head, gh[u][:, :LANES], 0.0) + eye2 * jnp.exp(cum_end[u])
            lin_scr[u] = jnp.concatenate([ry, g_bd], axis=0)
            yq_scr[u] = yr[u][:, LANES:] + av2[u][n:]
            h_scr[u] = jnp.where(same_head, gh[u][:, LANES:] + kv[u], 0.0)

    def sweep_steps(g, state):
        def step(u):
            rows = pl.ds(pl.multiple_of((g * unroll + u) * n, n), n)
            both = mm(lin_scr[u], state[0])
            y_ref[rows, :] = both[:n] + yq_scr[u]
            state[0] = both[n:] + h_scr[u]
        return [functools.partial(step, u) for u in range(unroll)]

    ngroups = nc // unroll
    groups_per_seq = ngroups // nseq
    build_group(0)

    def body(g, st):
        state = [st]
        pending.extend(sweep_steps(g - 1, state))
        build_group(g)
        return jnp.where(g % groups_per_seq == 0, 0.0, state[0])

    st = lax.fori_loop(1, ngroups, body, jnp.zeros((two, two), F32))
    state = [st]
    for step in sweep_steps(ngroups - 1, state):
        step()


def _rwkv_scan(r, wl, k2, v, kk, a, bsz, seq):
    m = r.shape[0]
    npair = BRANCH_W // LANES
    nseq = 2 if bsz % 2 == 0 else 1
    blk = pl.BlockSpec((nseq * seq, LANES), lambda b, p: (b, p))
    u = RWKV_UNROLL
    return pl.pallas_call(
        functools.partial(_rwkv_scan_kernel, unroll=u, nseq=nseq),
        name="rwkv_scan",
        grid=(bsz // nseq, npair),
        in_specs=[blk] * 6,
        out_specs=blk,
        out_shape=jax.ShapeDtypeStruct((m, BRANCH_W), F32),
        scratch_shapes=[pltpu.VMEM((u, 3 * RWKV_L, LANES), F32), pltpu.VMEM((u, RWKV_L, LANES), F32),
                        pltpu.VMEM((u, LANES, LANES), F32)],
        compiler_params=_cparams(("parallel", "parallel")),
    )(r, wl, k2, v, kk, a)


def _rwkv_post_kernel(y_ref, bon_ref, sg_ref, g_ref, b_ref, bd_ref, o_ref):
    y = y_ref[...]
    bd = bd_ref[...]
    mu = _head_sum(y, bd) * (1.0 / RWKV_HD)
    yc = y - mu
    var = _head_sum(yc * yc, bd) * (1.0 / RWKV_HD)
    yn = yc * lax.rsqrt(var + RWKV_GN_EPS) * g_ref[...] + b_ref[...]
    o_ref[...] = ((yn + bon_ref[...].astype(F32)) * sg_ref[...].astype(F32)).astype(o_ref.dtype)


def _rwkv_post(y, bon, sg, g, b, bd):
    m, w = y.shape
    tb = RWKV_POST_TB
    rows = pl.BlockSpec((tb, w), lambda i: (i, 0))
    vec = pl.BlockSpec((1, w), lambda i: (0, 0))
    return pl.pallas_call(
        _rwkv_post_kernel,
        name="rwkv_post",
        grid=(m // tb,),
        in_specs=[rows, rows, rows, vec, vec, pl.BlockSpec((LANES, LANES), lambda i: (0, 0))],
        out_specs=rows,
        out_shape=jax.ShapeDtypeStruct((m, w), BF16),
        compiler_params=_cparams(("parallel",)),
    )(y, bon, sg, g.reshape(1, w), b.reshape(1, w), bd)


def _sgu_kernel(u_ref, v_ref, gate_ref, lng_ref, lnb_ref, ws_ref, bs_ref, o_ref):
    inv_sqrt2 = 0.7071067811865476

    def gelu(z):
        return 0.5 * z * (1.0 + lax.erf(z * inv_sqrt2))

    v = gelu(v_ref[...].astype(F32))
    mu = jnp.mean(v, axis=-1, keepdims=True)
    vc = v - mu
    var = jnp.mean(vc * vc, axis=-1, keepdims=True)
    vn = vc * lax.rsqrt(var + SGU_LN_EPS) * lng_ref[...] + lnb_ref[...]
    t = SGU_CHUNK
    ii = lax.broadcasted_iota(I32, (t, t), 0) // CHUNK
    jj = lax.broadcasted_iota(I32, (t, t), 1) // CHUNK
    mask = jj <= ii
    for g in range(SGU_GROUPS):
        ws = jnp.where(mask, ws_ref[g], 0.0)
        sl = slice(g * SGU_GW, (g + 1) * SGU_GW)
        bias = bs_ref[:, g:g + 1]
        for c in range(u_ref.shape[0] // t):
            rows = slice(c * t, (c + 1) * t)
            sv = _dot(ws, vn[rows, sl]) + bias
            u = gelu(u_ref[rows, sl].astype(F32))
            o_ref[rows, sl] = (u * sv * _silu(gate_ref[rows, sl].astype(F32))).astype(o_ref.dtype)


def _sgu(p, lng, lnb, ws, bs_t):
    m = p.shape[0]
    w = BRANCH_W
    tb = SGU_TB
    d0 = P_D0 // w
    return pl.pallas_call(
        _sgu_kernel,
        name="sgu",
        grid=(m // tb,),
        in_specs=[pl.BlockSpec((tb, w), lambda i: (i, d0)),
                  pl.BlockSpec((tb, w), lambda i: (i, d0 + 1)),
                  pl.BlockSpec((tb, w), lambda i: (i, d0 + 2)),
                  pl.BlockSpec((1, w), lambda i: (0, 0)),
                  pl.BlockSpec((1, w), lambda i: (0, 0)),
                  pl.BlockSpec((SGU_GROUPS, SGU_CHUNK, SGU_CHUNK), lambda i: (0, 0, 0)),
                  pl.BlockSpec((SGU_CHUNK, SGU_GROUPS), lambda i: (0, 0))],
        out_specs=pl.BlockSpec((tb, w), lambda i: (i, 0)),
        out_shape=jax.ShapeDtypeStruct((m, w), BF16),
        compiler_params=_cparams(("parallel",)),
    )(p, p, p, lng.reshape(1, w), lnb.reshape(1, w), ws, bs_t)


def _merge_kernel(ya_ref, yb_ref, yc_ref, yd_ref, ga_ref, gb_ref, gc_ref, gd_ref, wp_ref, o_ref):
    acc = None
    for g, (y_ref, g_ref) in enumerate(((ya_ref, ga_ref), (yb_ref, gb_ref), (yc_ref, gc_ref), (yd_ref, gd_ref))):
        term = jax.nn.sigmoid(g_ref[...].astype(F32)) * _dot(y_ref[...], wp_ref[g])
        acc = term if acc is None else acc + term
    o_ref[...] = acc.astype(o_ref.dtype)


def _merge(ya, yb, yc, yd, p, wproj):
    m = ya.shape[0]
    w = BRANCH_W
    tm, tn = MERGE_TM, MERGE_TN
    nj = D_MODEL // tn
    m0 = P_M0 // tn
    ysp = pl.BlockSpec((tm, w), lambda j, i: (i, 0))
    gsp = [pl.BlockSpec((tm, tn), lambda j, i, g=g: (i, m0 + g * nj + j)) for g in range(N_BRANCH)]
    return pl.pallas_call(
        _merge_kernel,
        name="merge",
        grid=(nj, m // tm),
        in_specs=[ysp, ysp, ysp, ysp, *gsp,
                  pl.BlockSpec((N_BRANCH, w, tn), lambda j, i: (0, 0, j))],
        out_specs=pl.BlockSpec((tm, tn), lambda j, i: (i, j)),
        out_shape=jax.ShapeDtypeStruct((m, D_MODEL), BF16),
        compiler_params=_cparams(("parallel", "parallel")),
    )(ya, yb, yc, yd, p, p, p, p, wproj)


def _out_kernel(mg_ref, wo_ref, g_ref, x_ref, o_ref):
    y = _dot(mg_ref[...], wo_ref[...])
    y = y * lax.rsqrt(jnp.mean(y * y, axis=-1, keepdims=True) + NORM_EPS) * g_ref[...]
    o_ref[...] = x_ref[...] + y


def _out_proj(merged, wout, g, x2):
    m, d = x2.shape
    tm = OUT_TM
    return pl.pallas_call(
        _out_kernel,
        name="out_proj",
        grid=(m // tm,),
        in_specs=[pl.BlockSpec((tm, d), lambda i: (i, 0)),
                  pl.BlockSpec((d, d), lambda i: (0, 0)),
                  pl.BlockSpec((1, d), lambda i: (0, 0)),
                  pl.BlockSpec((tm, d), lambda i: (i, 0))],
        out_specs=pl.BlockSpec((tm, d), lambda i: (i, 0)),
        out_shape=jax.ShapeDtypeStruct((m, d), F32),
        compiler_params=_cparams(("parallel",)),
    )(merged, wout, g.reshape(1, d), x2)


def _split_cols(w, sizes):
    offs = np.cumsum((0,) + tuple(sizes))
    return [w[:, int(offs[i]):int(offs[i + 1])] for i in range(len(sizes))]


def _pad_cols(w, n):
    return jnp.pad(w, ((0, 0), (0, n - w.shape[1])))


def _pad_rows(w, n):
    return jnp.pad(w, ((0, n - w.shape[0]), (0, 0)))


def _rope_tables(positions, rot_dim, width):
    inv = ROPE_THETA ** (-jnp.arange(0, rot_dim, 2, dtype=F32) / rot_dim)
    ang = positions.astype(F32).reshape(-1, 1) * inv
    cos, sin = jnp.cos(ang), jnp.sin(ang)
    m = cos.shape[0]
    rest = width - rot_dim
    cos_h = jnp.concatenate([cos, cos, jnp.ones((m, rest), F32)], axis=1)
    sin_h = jnp.concatenate([-sin, sin, jnp.zeros((m, rest), F32)], axis=1)
    reps = LANES // width
    return jnp.tile(cos_h, (1, reps)), jnp.tile(sin_h, (1, reps))


def kernel(x, positions, norm_pre, norm_post, w_in, gla_w_g2, gla_b_g, gla_head_g, rwkv_mu, rwkv_w0,
           rwkv_w_w2, rwkv_a0, rwkv_w_a2, rwkv_k_k, rwkv_k_a, rwkv_r_k, rwkv_lnx_g, rwkv_lnx_b,
           sgu_ln_g, sgu_ln_b, sgu_w_s, sgu_b_s, w_proj, w_out):
    bsz, seq, d = x.shape
    depth = w_in.shape[0]
    m = bsz * seq
    x2 = x.reshape(m, d)
    ca, sa = _rope_tables(positions, DSA_ROT, DSA_HD)
    ci, si = _rope_tables(positions, IDX_ROT, IDX_HD)
    lane_grp = np.arange(LANES) // RWKV_HD
    bd = jnp.asarray((lane_grp[:, None] == lane_grp[None, :]).astype(np.float32))

    for l in range(depth):
        wa, wb, wc, wd, wm = _split_cols(w_in[l], GROUP_COLS)
        a_q, a_k, a_v, a_glr, a_gate = _split_cols(wa, A_COLS)
        b_q, b_k, b_v, b_iq, b_ik, b_iw, b_gate = _split_cols(wb, B_COLS)
        c_r, c_k, c_v, c_wlr, c_alr, c_gate = _split_cols(wc, C_COLS)
        w_main = jnp.concatenate([t.astype(BF16) for t in (wm, b_q, b_k, b_v, b_gate, c_r, c_k, c_v, c_gate, wd,
                                                            a_q, a_k, a_v, a_gate)], axis=1)
        w_small = jnp.concatenate([b_iq] + [_pad_cols(t, LANES) for t in (b_ik, b_iw, c_wlr, c_alr, a_glr)],
                                  axis=1).astype(BF16)

        p, h = _norm_matmul(x2, norm_pre[l], w_main, MM_TN)
        ps = _matmul(h, w_small, S_COLS, F32)

        ya = _gla(p, ps, _pad_rows(gla_w_g2[l], LANES), gla_b_g[l], gla_head_g[l], bsz, seq)

        qr, kr, vt, iqr, ikr, iws = _dsa_prep(p, ps, ca, sa, ci, si)
        yb = _dsa(qr, kr, vt, iqr, ikr, iws, p, bsz, seq)

        mu_r, mu_k, mu_v, mu_wlr, mu_alr, mu_gate = _split_cols(rwkv_mu[l][None, :], C_COLS)
        mu_main = jnp.concatenate([mu_r, mu_k, mu_v, mu_gate], axis=1)
        mu_s = jnp.concatenate([_pad_cols(mu_wlr, LANES), _pad_cols(mu_alr, LANES)], axis=1)
        r, wl, k2, v, kk, a, sg, bon = _rwkv_prep(
            p, ps, mu_main, mu_s, rwkv_w0[l], _pad_rows(rwkv_w_w2[l], LANES), rwkv_a0[l],
            _pad_rows(rwkv_w_a2[l], LANES), rwkv_k_k[l], rwkv_k_a[l], rwkv_r_k[l].reshape(-1), bd, seq)
        y = _rwkv_scan(r, wl, k2, v, kk, a, bsz, seq)
        yc = _rwkv_post(y, bon, sg, rwkv_lnx_g[l], rwkv_lnx_b[l], bd)

        yd = _sgu(p, sgu_ln_g[l], sgu_ln_b[l], sgu_w_s[l], sgu_b_s[l].T)

        merged = _merge(ya, yb, yc, yd, p, w_proj[l].astype(BF16))
        x2 = _out_proj(merged, w_out[l].astype(BF16), norm_post[l], x2)
    return x2.reshape(bsz, seq, d)
```

```python
import functools

import numpy as np
import jax
import jax.numpy as jnp
from jax import lax
from jax.experimental import pallas as pl
from jax.experimental.pallas import tpu as pltpu

F32 = jnp.float32
BF16 = jnp.bfloat16
I32 = jnp.int32
I16 = jnp.int16

D_MODEL = 2048
CHUNK = 64
NORM_EPS = 1e-6
ROPE_THETA = 500000.0
N_BRANCH = 4
BRANCH_W = D_MODEL // 2

GLA_HEADS = 4
GLA_DK = BRANCH_W // 2 // GLA_HEADS
GLA_DV = BRANCH_W // GLA_HEADS
GLA_RANK = 16
GLA_TAU = 16.0

DSA_HEADS = 8
DSA_HD = BRANCH_W // DSA_HEADS
DSA_ROT = DSA_HD // 4
IDX_HEADS = 8
IDX_HD = 64
IDX_ROT = IDX_HD // 4
TOPK_MAX = 256

RWKV_HD = 64
RWKV_HEADS = BRANCH_W // RWKV_HD
RWKV_DECAY_RANK = 96
RWKV_A_RANK = 96
RWKV_DECAY_SCALE = 0.606531
RWKV_GN_EPS = 64e-5
RWKV_KK_EPS = 1e-12

SGU_CHUNK = 128
SGU_GROUPS = 8
SGU_GW = BRANCH_W // SGU_GROUPS
SGU_LN_EPS = 1e-5

A_COLS = (GLA_HEADS * GLA_DK, GLA_HEADS * GLA_DK, BRANCH_W, GLA_RANK, BRANCH_W)
B_COLS = (BRANCH_W, BRANCH_W, BRANCH_W, IDX_HEADS * IDX_HD, IDX_HD, IDX_HEADS, BRANCH_W)
C_COLS = (BRANCH_W, BRANCH_W, BRANCH_W, RWKV_DECAY_RANK, RWKV_A_RANK, BRANCH_W)
D_COLS = (2 * BRANCH_W, BRANCH_W)
GROUP_COLS = (sum(A_COLS), sum(B_COLS), sum(C_COLS), sum(D_COLS), N_BRANCH * D_MODEL)

LANES = 128
SUBLANES = 8
VMEM_LIMIT = 56 * 1024 * 1024
MXU_DEPTH = 256
IDX_K = MXU_DEPTH
NEG_BIG = -1e30
LOG2E = 1.4426950408889634
INT_MIN = -(2 ** 31)

P_M0 = 0
P_B0 = P_M0 + N_BRANCH * D_MODEL
P_C0 = P_B0 + 4 * BRANCH_W
P_D0 = P_C0 + 4 * BRANCH_W
P_A0 = P_D0 + 3 * BRANCH_W
P_COLS = P_A0 + 2 * GLA_HEADS * GLA_DK + 2 * BRANCH_W
S_IDX0 = 0
S_IDX_COLS = IDX_HEADS * IDX_HD + 2 * 128
S_RW0 = S_IDX0 + S_IDX_COLS
S_GLR0 = S_RW0 + 2 * 128
S_COLS = S_GLR0 + 128

MM_TM = 1024
MM_TN = 1024
GLA_TB = 512
DSA_TQ = 256
DSA_KB = 512
DSA_VROWS = DSA_HD + 16
RWKV_PREP_TB = 256
RWKV_L = 64
RWKV_UNROLL = 8
RWKV_POST_TB = 512
SGU_TB = 512
MERGE_TM = 512
MERGE_TN = 1024
OUT_TM = 512


def _cparams(sem):
    return pltpu.CompilerParams(dimension_semantics=sem, vmem_limit_bytes=VMEM_LIMIT)


def _dot(a, b):
    return jnp.dot(a, b, preferred_element_type=F32)


def _dot_nt(a, b):
    return lax.dot_general(a, b, (((1,), (1,)), ((), ())), preferred_element_type=F32)


def _dot_tn(a, b):
    return lax.dot_general(a, b, (((0,), (0,)), ((), ())), preferred_element_type=F32)


def _bf16_terms(x, n):
    terms = []
    for _ in range(n):
        t = x.astype(BF16)
        terms.append(t)
        x = x - t.astype(F32)
    return terms


def _dot_exact_lhs(a, b, nterms):
    a16 = a.astype(BF16)
    out = None
    for t in _bf16_terms(b, nterms):
        d = _dot(a16, t)
        out = d if out is None else out + d
    return out


def _dot_exact_rhs(a, b, nterms):
    b16 = b.astype(BF16)
    out = None
    for t in _bf16_terms(a, nterms):
        d = _dot(t, b16)
        out = d if out is None else out + d
    return out


def _dot_3pass(a, b, b_lo=None):
    a_hi, a_lo = _bf16_terms(a, 2)
    b_hi, b_lo = _bf16_terms(b, 2) if b_lo is None else (b, b_lo)
    return _dot(a_hi, b_hi) + _dot(a_lo, b_hi) + _dot(a_hi, b_lo)


def _silu(x):
    return x * jax.nn.sigmoid(x)


def _norm_matmul_kernel(x_ref, g_ref, b_ref, o_ref, h_ref):
    @pl.when(pl.program_id(1) == 0)
    def _():
        x = x_ref[...]
        ms = jnp.mean(x * x, axis=-1, keepdims=True)
        h_ref[...] = (x * lax.rsqrt(ms + NORM_EPS) * g_ref[...]).astype(h_ref.dtype)

    o_ref[...] = _dot(h_ref[...], b_ref[...]).astype(o_ref.dtype)


def _norm_matmul(x2, g, b, tn):
    m, k = x2.shape
    n = b.shape[1]
    tm = min(MM_TM, m)
    return pl.pallas_call(
        _norm_matmul_kernel,
        name="proj",
        grid=(m // tm, n // tn),
        in_specs=[pl.BlockSpec((tm, k), lambda i, j: (i, 0)),
                  pl.BlockSpec((1, k), lambda i, j: (0, 0)),
                  pl.BlockSpec((k, tn), lambda i, j: (0, j))],
        out_specs=[pl.BlockSpec((tm, tn), lambda i, j: (i, j)),
                   pl.BlockSpec((tm, k), lambda i, j: (i, 0))],
        out_shape=[jax.ShapeDtypeStruct((m, n), BF16), jax.ShapeDtypeStruct((m, k), BF16)],
        compiler_params=_cparams(("parallel", "arbitrary")),
    )(x2, g.reshape(1, k), b)


def _matmul_kernel(a_ref, b_ref, o_ref):
    o_ref[...] = _dot(a_ref[...], b_ref[...]).astype(o_ref.dtype)


def _matmul(a, b, tn, out_dtype=F32):
    m, k = a.shape
    n = b.shape[1]
    tm = min(MM_TM, m)
    return pl.pallas_call(
        _matmul_kernel,
        name="proj",
        grid=(m // tm, n // tn),
        in_specs=[pl.BlockSpec((tm, k), lambda i, j: (i, 0)),
                  pl.BlockSpec((k, tn), lambda i, j: (0, j))],
        out_specs=pl.BlockSpec((tm, tn), lambda i, j: (i, j)),
        out_shape=jax.ShapeDtypeStruct((m, n), out_dtype),
        compiler_params=_cparams(("parallel", "parallel")),
    )(a, b)


def _gla_kernel(q_ref, k_ref, v_ref, gate_ref, glr_ref, wg2_ref, bg_ref, hg_ref, o_ref, state_ref):
    @pl.when(pl.program_id(2) == 0)
    def _():
        state_ref[...] = jnp.zeros_like(state_ref)

    ii = lax.broadcasted_iota(I32, (CHUNK, CHUNK), 0)
    jj = lax.broadcasted_iota(I32, (CHUNK, CHUNK), 1)
    tril = (ii >= jj).astype(F32)
    chunks = [pl.ds(c * CHUNK, CHUNK) for c in range(GLA_TB // CHUNK)]
    z = _dot_3pass(glr_ref[...], wg2_ref[...]) + bg_ref[...]
    log_a = (jnp.minimum(z, 0.0) - jnp.log1p(jnp.exp(-jnp.abs(z)))) / GLA_TAU
    cum = [_dot_exact_lhs(tril, log_a[c * CHUNK:(c + 1) * CHUNK], 3) for c in range(len(chunks))]
    total = [x[CHUNK - 1:CHUNK, :] for x in cum]
    k_dec = [k_ref[sl, :].astype(F32) * jnp.exp(t - x) for sl, t, x in zip(chunks, total, cum)]
    kv = [_dot_tn(v_ref[sl, :], kd.astype(BF16)) for sl, kd in zip(chunks, k_dec)]
    states = []
    st = state_ref[...]
    for t, x in zip(total, kv):
        st = st * jnp.exp(t) + x
        states.append(st)
    state_ref[...] = st
    outs = [_dot_nt((q_ref[sl, :].astype(F32) * (GLA_DK ** -0.5)).astype(BF16), s.astype(BF16))
            for sl, s in zip(chunks, states)]
    for sl, o in zip(chunks, outs):
        o = o * lax.rsqrt(jnp.mean(o * o, axis=-1, keepdims=True) + NORM_EPS) * hg_ref[...]
        o_ref[sl, :] = (o * _silu(gate_ref[sl, :].astype(F32))).astype(o_ref.dtype)


def _gla(p, ps, wg2p, bg, hg, bsz, seq):
    m = p.shape[0]
    nt = seq // GLA_TB
    row = lambda b, h, t: b * nt + t
    qb = P_A0 // GLA_DK
    kb = qb + GLA_HEADS
    vb = (P_A0 + 2 * GLA_HEADS * GLA_DK) // GLA_DV
    gb = vb + GLA_HEADS
    lb = S_GLR0 // LANES
    return pl.pallas_call(
        _gla_kernel,
        name="gla",
        grid=(bsz, GLA_HEADS, nt),
        in_specs=[
            pl.BlockSpec((GLA_TB, GLA_DK), lambda b, h, t: (row(b, h, t), qb + h)),
            pl.BlockSpec((GLA_TB, GLA_DK), lambda b, h, t: (row(b, h, t), kb + h)),
            pl.BlockSpec((GLA_TB, GLA_DV), lambda b, h, t: (row(b, h, t), vb + h)),
            pl.BlockSpec((GLA_TB, GLA_DV), lambda b, h, t: (row(b, h, t), gb + h)),
            pl.BlockSpec((GLA_TB, LANES), lambda b, h, t: (row(b, h, t), lb)),
            pl.BlockSpec((LANES, GLA_DK), lambda b, h, t: (0, h)),
            pl.BlockSpec((1, GLA_DK), lambda b, h, t: (0, h)),
            pl.BlockSpec((1, GLA_DV), lambda b, h, t: (0, 0)),
        ],
        out_specs=pl.BlockSpec((GLA_TB, GLA_DV), lambda b, h, t: (row(b, h, t), h)),
        out_shape=jax.ShapeDtypeStruct((m, BRANCH_W), BF16),
        scratch_shapes=[pltpu.VMEM((GLA_DV, GLA_DK), F32)],
        compiler_params=_cparams(("parallel", "parallel", "arbitrary")),
    )(p, p, p, p, ps, wg2p, bg.reshape(1, -1), hg.reshape(1, -1))


def _rope_tile(x, cos_t, sin_t, half, width):
    lane = lax.broadcasted_iota(I32, x.shape, 1) % width
    partner = jnp.where(lane < half, pltpu.roll(x, LANES - half, 1), pltpu.roll(x, half, 1))
    return x * cos_t + partner * sin_t


def _dsa_prep_kernel(q_ref, k_ref, v_ref, idx_ref, ca_ref, sa_ref, ci_ref, si_ref,
                     qo_ref, ko_ref, vo_ref, iqo_ref, iko_ref, iwo_ref):
    ca, sa = ca_ref[...], sa_ref[...]
    ci, si = ci_ref[...], si_ref[...]
    for h in range(DSA_HEADS):
        sl = slice(h * DSA_HD, (h + 1) * DSA_HD)
        q = _rope_tile(q_ref[:, sl].astype(F32), ca, sa, DSA_ROT // 2, DSA_HD) * (DSA_HD ** -0.5 * LOG2E)
        qo_ref[:, sl] = q.astype(qo_ref.dtype)
        ko_ref[:, sl] = _rope_tile(k_ref[:, sl].astype(F32), ca, sa, DSA_ROT // 2, DSA_HD).astype(ko_ref.dtype)
        vo_ref[0, h * DSA_VROWS:h * DSA_VROWS + DSA_HD, :] = v_ref[:, sl].astype(F32).T.astype(vo_ref.dtype)
        extra = lax.broadcasted_iota(I32, (DSA_VROWS - DSA_HD, q_ref.shape[0]), 0) == 0
        vo_ref[0, h * DSA_VROWS + DSA_HD:(h + 1) * DSA_VROWS, :] = jnp.where(extra, 1.0, 0.0).astype(vo_ref.dtype)
    niq = IDX_HEADS * IDX_HD
    low = lax.broadcasted_iota(I32, (q_ref.shape[0], LANES), 1) < IDX_HD

    def hi_lo(x):
        hi = x.astype(BF16).astype(F32)
        return hi, x - hi

    for j in range(niq // LANES):
        x = _rope_tile(idx_ref[:, j * LANES:(j + 1) * LANES], ci, si, IDX_ROT // 2, IDX_HD)
        hi, lo = hi_lo(x)
        hi_sw = pltpu.roll(hi, IDX_HD, 1)
        lo_sw = pltpu.roll(lo, IDX_HD, 1)
        base = 2 * j * IDX_K
        iqo_ref[:, base:base + LANES] = jnp.where(low, hi, hi_sw).astype(BF16)
        iqo_ref[:, base + LANES:base + IDX_K] = jnp.where(low, lo, 0.0).astype(BF16)
        iqo_ref[:, base + IDX_K:base + IDX_K + LANES] = jnp.where(low, hi_sw, hi).astype(BF16)
        iqo_ref[:, base + IDX_K + LANES:base + 2 * IDX_K] = jnp.where(low, lo_sw, 0.0).astype(BF16)
    hi, lo = hi_lo(_rope_tile(idx_ref[:, niq:niq + LANES], ci, si, IDX_ROT // 2, IDX_HD))
    iko_ref[:, :LANES] = jnp.where(low, hi, pltpu.roll(lo, IDX_HD, 1)).astype(BF16)
    iko_ref[:, LANES:] = jnp.where(low, hi, 0.0).astype(BF16)
    iwo_ref[...] = idx_ref[:, niq + LANES:niq + 2 * LANES] * (niq ** -0.5)


def _dsa_prep(p, ps, ca, sa, ci, si):
    m = p.shape[0]
    tb = DSA_KB
    w = BRANCH_W
    rows = lambda i: (i, 0)
    b0 = P_B0 // w
    return pl.pallas_call(
        _dsa_prep_kernel,
        name="dsa_prep",
        grid=(m // tb,),
        in_specs=[pl.BlockSpec((tb, w), lambda i: (i, b0)),
                  pl.BlockSpec((tb, w), lambda i: (i, b0 + 1)),
                  pl.BlockSpec((tb, w), lambda i: (i, b0 + 2)),
                  pl.BlockSpec((tb, S_IDX_COLS), lambda i: (i, S_IDX0 // S_IDX_COLS)),
                  pl.BlockSpec((tb, LANES), rows), pl.BlockSpec((tb, LANES), rows),
                  pl.BlockSpec((tb, LANES), rows), pl.BlockSpec((tb, LANES), rows)],
        out_specs=[pl.BlockSpec((tb, w), rows), pl.BlockSpec((tb, w), rows),
                   pl.BlockSpec((1, DSA_HEADS * DSA_VROWS, tb), lambda i: (i, 0, 0)),
                   pl.BlockSpec((tb, IDX_HEADS * IDX_K), rows), pl.BlockSpec((tb, IDX_K), rows),
                   pl.BlockSpec((tb, LANES), rows)],
        out_shape=[jax.ShapeDtypeStruct((m, w), BF16), jax.ShapeDtypeStruct((m, w), BF16),
                   jax.ShapeDtypeStruct((m // tb, DSA_HEADS * DSA_VROWS, tb), BF16),
                   jax.ShapeDtypeStruct((m, IDX_HEADS * IDX_K), BF16),
                   jax.ShapeDtypeStruct((m, IDX_K), BF16), jax.ShapeDtypeStruct((m, LANES), F32)],
        compiler_params=_cparams(("parallel",)),
    )(p, p, p, ps, ca, sa, ci, si)


def _dsa_kernel(q_ref, k_ref, vt_ref, iq_ref, ik_ref, iw_ref, gate_ref, o_ref, key_scr, hi_scr, tie_scr,
                *, topk):
    tq, kb_w = DSA_TQ, DSA_KB
    q0 = pl.program_id(1) * tq
    nkb = (q0 + tq + kb_w - 1) // kb_w
    qcol = lax.broadcasted_iota(I32, (1, tq), 1)
    qlim = ((q0 + qcol) // CHUNK + 1) * CHUNK
    sub_k = lax.broadcasted_iota(I32, (kb_w, 1), 0)

    def score_body(kb, carry):
        ik = ik_ref[pl.ds(pl.multiple_of(kb * kb_w, kb_w), kb_w), :]
        dots = [_dot_nt(ik, iq_ref[:, h * IDX_K:(h + 1) * IDX_K]) for h in range(IDX_HEADS)]
        acc = jnp.zeros((kb_w, tq), F32)
        for h, d in enumerate(dots):
            acc = acc + iw_ref[h:h + 1, :] * jnp.maximum(d, 0.0)
        acc = acc + 0.0
        bits = lax.bitcast_convert_type(acc, I32)
        key = bits ^ ((bits >> 31) & 0x7FFFFFFF)
        sidx = kb * kb_w + sub_k
        key = jnp.where(sidx < qlim, key, INT_MIN)
        key_scr[kb] = key
        hi_scr[kb] = (key >> 16).astype(I16)
        return carry

    lax.fori_loop(0, nkb, score_body, 0)

    def count_tiles(load, pred, rows_per_vreg, dtype):
        def body(kb, c):
            m = pred(load(kb), kb * kb_w + sub_k).astype(dtype)
            if dtype == I32:
                nparts = 8
                rows_part = kb_w // nparts
                parts = [jnp.sum(m[j * rows_part:(j + 1) * rows_part]
                                 .reshape(rows_part // rows_per_vreg, rows_per_vreg, tq), axis=0)
                         for j in range(nparts)]
            else:
                parts = [m[j * rows_per_vreg:(j + 1) * rows_per_vreg] for j in range(kb_w // rows_per_vreg)]
            while len(parts) > 1:
                parts = [a + b for a, b in zip(parts[::2], parts[1::2])]
            return c + parts[0]
        c = lax.fori_loop(0, nkb, body, jnp.zeros((rows_per_vreg, tq), dtype))
        return jnp.sum(c.astype(I32), axis=0, keepdims=True)

    def count(pred):
        return count_tiles(lambda kb: key_scr[kb], pred, SUBLANES, I32)

    def count_hi(pred):
        return count_tiles(lambda kb: hi_scr[kb], pred, 2 * SUBLANES, I16)

    def hi_body(it, carry):
        res, cnt_res = carry
        trial = res | lax.shift_left(jnp.int32(1), 31 - it)
        cand_hi = ((trial ^ INT_MIN) >> 16).astype(I16)
        cnt = count_hi(lambda kt, s: kt >= cand_hi)
        ok = cnt >= topk
        return jnp.where(ok, trial, res), jnp.where(ok, cnt, cnt_res)

    res_hi, cnt_ge_hi = lax.fori_loop(0, 16, hi_body,
                                      (jnp.zeros((1, tq), I32), jnp.full((1, tq), nkb * kb_w, I32)))
    thr_hi = ((res_hi ^ INT_MIN) >> 16).astype(I16)
    cnt_gt_hi = count_hi(lambda kt, s: kt > thr_hi)
    lo_min = -(2 ** 15)

    def lo_prep(kb, carry):
        lo = ((key_scr[kb] & 0xFFFF) + lo_min).astype(I16)
        hi_scr[kb] = jnp.where(hi_scr[kb] == thr_hi, lo, I16(lo_min))
        return carry

    lax.fori_loop(0, nkb, lo_prep, 0)
    places = topk - cnt_gt_hi

    def lo_body(it, carry):
        res, cnt_res = carry
        trial = res | lax.shift_left(jnp.int32(1), 15 - it)
        cand_lo = (trial + lo_min).astype(I16)
        cnt = count_hi(lambda kt, s: kt >= cand_lo)
        ok = cnt >= places
        return jnp.where(ok, trial, res), jnp.where(ok, cnt, cnt_res)

    res_lo, cnt_ge_lo = lax.fori_loop(0, 16, lo_body, (jnp.zeros((1, tq), I32), cnt_ge_hi - cnt_gt_hi))
    thr = (res_hi ^ INT_MIN) | res_lo
    cnt_ge = cnt_gt_hi + cnt_ge_lo
    seq_len = key_scr.shape[0] * kb_w
    tie_scr[...] = jnp.full(tie_scr.shape, seq_len, I32)
    has_tie = (cnt_ge > topk) & (thr != INT_MIN)

    @pl.when(jnp.max(jnp.where(has_tie, 1.0, 0.0)) > 0.5)
    def _():
        need = topk - count(lambda kt, s: kt > thr)
        nbits = max(1, int(np.ceil(np.log2(seq_len))))

        def tie_body(it, resj):
            trial = resj | lax.shift_left(jnp.int32(1), nbits - 1 - it)
            c = count(lambda kt, s: (kt == thr) & (s < trial))
            return jnp.where(c < need, trial, resj)

        tie_scr[...] = lax.fori_loop(0, nbits, tie_body, jnp.zeros((1, tq), I32))

    last_tie = tie_scr[...]

    def softmax_mask(kb):
        kt = key_scr[kb]
        sidx = kb * kb_w + sub_k
        sel = ((kt > thr) | ((kt == thr) & (sidx <= last_tie))) & (kt != INT_MIN)
        return jnp.where(sel, 0.0, NEG_BIG).astype(BF16)

    heads = [slice(h * DSA_HD, (h + 1) * DSA_HD) for h in range(DSA_HEADS)]

    def att_body(kb, stats):
        rows = pl.ds(pl.multiple_of(kb * kb_w, kb_w), kb_w)
        bias = softmax_mask(kb)
        s = [_dot_nt(k_ref[rows, sl], q_ref[:, sl]).astype(BF16) + bias for sl in heads]
        m_n = [jnp.maximum(c[0], jnp.max(x, axis=0, keepdims=True).astype(F32)) for c, x in zip(stats, s)]
        p = [jnp.exp2(x - m.astype(BF16)) for x, m in zip(s, m_n)]
        pv = [_dot(vt_ref[kb, h * DSA_VROWS:(h + 1) * DSA_VROWS, :], x) for h, x in enumerate(p)]
        alpha = [jnp.exp2(c[0] - m) for c, m in zip(stats, m_n)]
        return tuple((m, a * c[1] + y) for m, a, c, y in zip(m_n, alpha, stats, pv))

    init = tuple((jnp.full((1, tq), NEG_BIG, F32), jnp.zeros((DSA_VROWS, tq), F32)) for _ in heads)
    final = lax.fori_loop(0, nkb, att_body, init)
    for sl, (_, acc_f) in zip(heads, final):
        o = (acc_f[:DSA_HD] / acc_f[DSA_HD:DSA_HD + 1]).T
        o_ref[:, sl] = (o * _silu(gate_ref[:, sl].astype(F32))).astype(o_ref.dtype)


def _dsa(qr, kr, vt, iqr, ikr, iws, p, bsz, seq):
    m = qr.shape[0]
    w = BRANCH_W
    nq = seq // DSA_TQ
    nkb = seq // DSA_KB
    topk = min(TOPK_MAX, seq // 4)
    qrow = lambda b, i: (b * nq + i, 0)
    return pl.pallas_call(
        functools.partial(_dsa_kernel, topk=topk),
        name="dsa",
        grid=(bsz, nq),
        in_specs=[pl.BlockSpec((DSA_TQ, w), qrow),
                  pl.BlockSpec((seq, w), lambda b, i: (b, 0), pipeline_mode=pl.Buffered(1)),
                  pl.BlockSpec((nkb, DSA_HEADS * DSA_VROWS, DSA_KB), lambda b, i: (b, 0, 0),
                               pipeline_mode=pl.Buffered(1)),
                  pl.BlockSpec((DSA_TQ, IDX_HEADS * IDX_K), qrow),
                  pl.BlockSpec((seq, IDX_K), lambda b, i: (b, 0), pipeline_mode=pl.Buffered(1)),
                  pl.BlockSpec((IDX_HEADS, DSA_TQ), lambda b, i: (0, b * nq + i)),
                  pl.BlockSpec((DSA_TQ, w), lambda b, i: (b * nq + i, P_B0 // w + 3))],
        out_specs=pl.BlockSpec((DSA_TQ, w), qrow),
        out_shape=jax.ShapeDtypeStruct((m, w), BF16),
        scratch_shapes=[pltpu.VMEM((nkb, DSA_KB, DSA_TQ), I32),
                        pltpu.VMEM((nkb, DSA_KB, DSA_TQ), I16),
                        pltpu.VMEM((1, DSA_TQ), I32)],
        compiler_params=_cparams(("parallel", "arbitrary")),
    )(qr, kr, vt, iqr, ikr, iws[:, :IDX_HEADS].T, p)


def _head_sum(x, bd):
    parts = [_dot_exact_rhs(x[:, j * LANES:(j + 1) * LANES], bd, 2) for j in range(x.shape[1] // LANES)]
    return jnp.concatenate(parts, axis=1)


def _rwkv_prep_kernel(c_ref, cs_ref, pc_ref, pcs_ref, mu_ref, mus_ref, w0_ref, ww2_ref, ww2lo_ref,
                      a0_ref, wa2_ref, wa2lo_ref, kkw_ref, ka_ref, rk_ref, bd_ref,
                      r_o, wl_o, k_o, v_o, kk_o, a_o, sg_o, bon_o, *, blocks_per_seq):
    tb = c_ref.shape[0]
    first = (pl.program_id(0) % blocks_per_seq) == 0
    rowi = lax.broadcasted_iota(I32, (tb, 1), 0)

    def shift_lerp(c, prev_rows, mu):
        last = prev_rows.shape[0] - 1
        prev = jnp.where(first, 0.0, prev_rows[last:last + 1, :])
        sh = jnp.where(rowi == 0, prev, pltpu.roll(c, 1, 0))
        return c + (sh - c) * mu

    w = BRANCH_W
    c = shift_lerp(c_ref[...].astype(F32), pc_ref[...].astype(F32), mu_ref[...])
    cs = shift_lerp(cs_ref[...], pcs_ref[...], mus_ref[...])
    r, k, v, gate = c[:, :w], c[:, w:2 * w], c[:, 2 * w:3 * w], c[:, 3 * w:]
    w_log = -RWKV_DECAY_SCALE * jax.nn.sigmoid(
        w0_ref[...] + _dot_3pass(jnp.tanh(cs[:, :LANES]), ww2_ref[...], ww2lo_ref[...]))
    a = jax.nn.sigmoid(a0_ref[...] + _dot_3pass(cs[:, LANES:], wa2_ref[...], wa2lo_ref[...]))
    kk = k * kkw_ref[...]
    k2 = k * (1.0 + (a - 1.0) * ka_ref[...])
    bd = bd_ref[...]
    kk = kk * lax.rsqrt(_head_sum(kk * kk, bd) + RWKV_KK_EPS)
    r_o[...] = r.astype(r_o.dtype)
    wl_o[...] = w_log
    k_o[...] = k2.astype(k_o.dtype)
    v_o[...] = v.astype(v_o.dtype)
    kk_o[...] = kk.astype(kk_o.dtype)
    a_o[...] = a.astype(a_o.dtype)
    sg_o[...] = _silu(gate).astype(sg_o.dtype)
    bon_o[...] = (_head_sum(r * k2 * rk_ref[...], bd) * v).astype(bon_o.dtype)


def _rwkv_prep(p, ps, mu_main, mu_s, w0, ww2p, a0, wa2p, kkw, ka, rk, bd, seq):
    m = p.shape[0]
    tb = RWKV_PREP_TB
    w = BRANCH_W
    pr16, pr8 = 2 * SUBLANES, SUBLANES
    cblk, sblk = P_C0 // (4 * w), S_RW0 // (2 * LANES)
    const = lambda i: (0, 0)
    vec = pl.BlockSpec((1, w), const)
    lowrank = pl.BlockSpec((LANES, w), const)
    ww2_hi, ww2_lo = _bf16_terms(ww2p, 2)
    wa2_hi, wa2_lo = _bf16_terms(wa2p, 2)
    out = pl.BlockSpec((tb, w), lambda i: (i, 0))
    dts = [BF16, F32, BF16, BF16, BF16, BF16, BF16, BF16]
    return pl.pallas_call(
        functools.partial(_rwkv_prep_kernel, blocks_per_seq=seq // tb),
        name="rwkv_prep",
        grid=(m // tb,),
        in_specs=[pl.BlockSpec((tb, 4 * w), lambda i: (i, cblk)),
                  pl.BlockSpec((tb, 2 * LANES), lambda i: (i, sblk)),
                  pl.BlockSpec((pr16, 4 * w), lambda i: (jnp.maximum(i * (tb // pr16) - 1, 0), cblk)),
                  pl.BlockSpec((pr8, 2 * LANES), lambda i: (jnp.maximum(i * (tb // pr8) - 1, 0), sblk)),
                  pl.BlockSpec((1, 4 * w), const), pl.BlockSpec((1, 2 * LANES), const),
                  vec, lowrank, lowrank, vec, lowrank, lowrank,
                  vec, vec, vec, pl.BlockSpec((LANES, LANES), const)],
        out_specs=[out] * 8,
        out_shape=[jax.ShapeDtypeStruct((m, w), dt) for dt in dts],
        compiler_params=_cparams(("parallel",)),
    )(p, ps, p, ps, mu_main, mu_s, w0.reshape(1, w), ww2_hi, ww2_lo, a0.reshape(1, w), wa2_hi, wa2_lo,
      kkw.reshape(1, w), ka.reshape(1, w), rk.reshape(1, w), bd)


def _rwkv_scan_kernel(r_ref, wl_ref, k_ref, v_ref, kk_ref, a_ref, y_ref, lin_scr, yq_scr, h_scr, *, unroll, nseq):
    n = RWKV_L
    two = 2 * n
    nc = r_ref.shape[0] // n
    i2 = lax.broadcasted_iota(I32, (two, two), 0)
    j2 = lax.broadcasted_iota(I32, (two, two), 1)
    same_head = (i2 // RWKV_HD) == (j2 // RWKV_HD)
    eye2 = (i2 == j2).astype(F32)
    ti = lax.broadcasted_iota(I32, (n, LANES), 0)
    ci = lax.broadcasted_iota(I32, (n, LANES), 1) % n
    strict = ti > ci
    incl = ti >= ci
    same16 = (ti // 16) == (ci // 16)
    eye = (ti == ci).astype(F32)
    ii = lax.broadcasted_iota(I32, (n, n), 0)
    jj = lax.broadcasted_iota(I32, (n, n), 1)
    tril = (ii >= jj).astype(F32)
    head0 = lax.broadcasted_iota(I32, (n, LANES), 1) < RWKV_HD

    def mm(a, b):
        return _dot(a.astype(BF16), b.astype(BF16))

    def mm_nt(a, b):
        return _dot_nt(a.astype(BF16), b.astype(BF16))

    def mm_tn(a, b):
        return _dot_tn(a.astype(BF16), b.astype(BF16))

    def stack(x):
        return jnp.concatenate([jnp.where(head0, x, 0.0), jnp.where(head0, 0.0, x)], axis=0)

    def fold(x):
        return x[:n] + x[n:]

    def each(f, *lists):
        return [f(*args) for args in zip(*lists)]

    def tri_inv(a_low):
        a_d = each(lambda a: jnp.where(same16, a, 0.0), a_low)
        a_o = each(lambda a, d: a - d, a_low, a_d)
        ad_st = each(stack, a_d)
        a2 = each(mm, a_d, ad_st)
        tick()
        a34 = each(lambda s, d: mm(s, jnp.concatenate([d, stack(s)], axis=1)), a2, ad_st)
        tick()
        a4_st = each(lambda x: stack(x[:, LANES:]), a34)
        s4 = each(lambda d, s, x: eye + d + s + x[:, :LANES], a_d, a2, a34)
        a8 = each(lambda x, y: mm(x[:, LANES:], y), a34, a4_st)
        tick()
        s8 = each(lambda s, x: s + mm(s, x), s4, a4_st)
        tick()
        t_d = each(lambda s, x: s + mm(s, stack(x)), s8, a8)
        tick()
        nn = each(lambda t, o: mm(t, stack(o)), t_d, a_o)
        tick()
        nx = each(lambda x, t: mm(x, jnp.concatenate([stack(x), stack(t)], axis=1)), nn, t_d)
        tick()
        x1 = each(lambda t, x: t + x[:, LANES:], t_d, nx)
        return each(lambda x, y: y + mm(x[:, :LANES], stack(y)), nx, x1)

    pending = []

    def tick():
        for _ in range(-(-unroll // 8)):
            if pending:
                pending.pop(0)()

    def build_group(g):
        rows = [pl.ds(pl.multiple_of((g * unroll + u) * n, n), n) for u in range(unroll)]
        wl = [wl_ref[r, :] for r in rows]
        k = [k_ref[r, :].astype(F32) for r in rows]
        v = [v_ref[r, :].astype(F32) for r in rows]
        kk = [kk_ref[r, :].astype(F32) for r in rows]
        beta = each(lambda x, r: x * a_ref[r, :].astype(F32), kk, rows)
        cum = each(lambda x: _dot_exact_lhs(tril, x, 3), wl)
        cum_end = each(lambda x: x[n - 1:n, :], cum)
        g_inv = each(lambda x: jnp.exp(-x), cum)
        g_end = each(lambda e, x: jnp.exp(e - x), cum_end, cum)
        a_t = each(lambda x, c, w: -x * jnp.exp(c - w), kk, cum, wl)
        r_t = each(lambda r, c: r_ref[r, :].astype(F32) * jnp.exp(c), rows, cum)
        bk_st = each(lambda x, y, z: jnp.concatenate([stack(x * z), stack(y * z)], axis=0), beta, k, g_inv)
        ar = each(lambda x, y: jnp.concatenate([x, y], axis=0), a_t, r_t)
        gram = each(mm_nt, ar, bk_st)
        a_ab = each(lambda x: jnp.where(strict, x[:n, :LANES], 0.0), gram)
        a_rb = each(lambda x: jnp.where(incl, x[n:, :LANES], 0.0), gram)
        a_ak = each(lambda x: jnp.where(strict, x[:n, LANES:], 0.0), gram)
        a_rk = each(lambda x: jnp.where(incl, x[n:, LANES:], 0.0), gram)
        av2 = each(lambda x, y, z: mm(jnp.concatenate([x, y], axis=0), stack(z)), a_ak, a_rk, v)
        t_inv = tri_inv(a_ab)
        tick()
        pq = each(lambda t, x, y: mm(t, jnp.concatenate([stack(x), stack(y[:n])], axis=1)), t_inv, a_t, av2)
        yr = each(lambda x, y: mm(x, jnp.concatenate([stack(y[:, :LANES]), stack(y[:, LANES:])], axis=1)), a_rb, pq)
        kv = each(lambda x, y, z: mm_tn(x * y, z), k, g_end, v)
        gh = each(lambda x, y, z: mm_tn(x * y, z), beta, g_end, pq)
        while pending:
            tick()
        for u in range(unroll):
            ry = r_t[u] + yr[u][:, :LANES]
            g_bd = jnp.where(same_head, gh[u][:, :LANES], 0.0) + eye2 * jnp.exp(cum_end[u])
            lin_scr[u] = jnp.concatenate([ry, g_bd], axis=0)
            yq_scr[u] = yr[u][:, LANES:] + av2[u][n:]
            h_scr[u] = jnp.where(same_head, gh[u][:, LANES:] + kv[u], 0.0)

    def sweep_steps(g, state):
        def step(u):
            rows = pl.ds(pl.multiple_of((g * unroll + u) * n, n), n)
            both = mm(lin_scr[u], state[0])
            y_ref[rows, :] = both[:n] + yq_scr[u]
            state[0] = both[n:] + h_scr[u]
        return [functools.partial(step, u) for u in range(unroll)]

    ngroups = nc // unroll
    groups_per_seq = ngroups // nseq
    build_group(0)

    def body(g, st):
        state = [st]
        pending.extend(sweep_steps(g - 1, state))
        build_group(g)
        return jnp.where(g % groups_per_seq == 0, 0.0, state[0])

    st = lax.fori_loop(1, ngroups, body, jnp.zeros((two, two), F32))
    state = [st]
    for step in sweep_steps(ngroups - 1, state):
        step()


def _rwkv_scan(r, wl, k2, v, kk, a, bsz, seq):
    m = r.shape[0]
    npair = BRANCH_W // LANES
    nseq = 2 if bsz % 2 == 0 else 1
    blk = pl.BlockSpec((nseq * seq, LANES), lambda b, p: (b, p))
    u = RWKV_UNROLL
    return pl.pallas_call(
        functools.partial(_rwkv_scan_kernel, unroll=u, nseq=nseq),
        name="rwkv_scan",
        grid=(bsz // nseq, npair),
        in_specs=[blk] * 6,
        out_specs=blk,
        out_shape=jax.ShapeDtypeStruct((m, BRANCH_W), F32),
        scratch_shapes=[pltpu.VMEM((u, 3 * RWKV_L, LANES), F32), pltpu.VMEM((u, RWKV_L, LANES), F32),
                        pltpu.VMEM((u, LANES, LANES), F32)],
        compiler_params=_cparams(("parallel", "parallel")),
    )(r, wl, k2, v, kk, a)


def _rwkv_post_kernel(y_ref, bon_ref, sg_ref, g_ref, b_ref, bd_ref, o_ref):
    y = y_ref[...]
    bd = bd_ref[...]
    mu = _head_sum(y, bd) * (1.0 / RWKV_HD)
    yc = y - mu
    var = _head_sum(yc * yc, bd) * (1.0 / RWKV_HD)
    yn = yc * lax.rsqrt(var + RWKV_GN_EPS) * g_ref[...] + b_ref[...]
    o_ref[...] = ((yn + bon_ref[...].astype(F32)) * sg_ref[...].astype(F32)).astype(o_ref.dtype)


def _rwkv_post(y, bon, sg, g, b, bd):
    m, w = y.shape
    tb = RWKV_POST_TB
    rows = pl.BlockSpec((tb, w), lambda i: (i, 0))
    vec = pl.BlockSpec((1, w), lambda i: (0, 0))
    return pl.pallas_call(
        _rwkv_post_kernel,
        name="rwkv_post",
        grid=(m // tb,),
        in_specs=[rows, rows, rows, vec, vec, pl.BlockSpec((LANES, LANES), lambda i: (0, 0))],
        out_specs=rows,
        out_shape=jax.ShapeDtypeStruct((m, w), BF16),
        compiler_params=_cparams(("parallel",)),
    )(y, bon, sg, g.reshape(1, w), b.reshape(1, w), bd)


def _sgu_kernel(u_ref, v_ref, gate_ref, lng_ref, lnb_ref, ws_ref, bs_ref, o_ref):
    inv_sqrt2 = 0.7071067811865476

    def gelu(z):
        return 0.5 * z * (1.0 + lax.erf(z * inv_sqrt2))

    v = gelu(v_ref[...].astype(F32))
    mu = jnp.mean(v, axis=-1, keepdims=True)
    vc = v - mu
    var = jnp.mean(vc * vc, axis=-1, keepdims=True)
    vn = vc * lax.rsqrt(var + SGU_LN_EPS) * lng_ref[...] + lnb_ref[...]
    t = SGU_CHUNK
    ii = lax.broadcasted_iota(I32, (t, t), 0) // CHUNK
    jj = lax.broadcasted_iota(I32, (t, t), 1) // CHUNK
    mask = jj <= ii
    for g in range(SGU_GROUPS):
        ws = jnp.where(mask, ws_ref[g], 0.0)
        sl = slice(g * SGU_GW, (g + 1) * SGU_GW)
        bias = bs_ref[:, g:g + 1]
        for c in range(u_ref.shape[0] // t):
            rows = slice(c * t, (c + 1) * t)
            sv = _dot(ws, vn[rows, sl]) + bias
            u = gelu(u_ref[rows, sl].astype(F32))
            o_ref[rows, sl] = (u * sv * _silu(gate_ref[rows, sl].astype(F32))).astype(o_ref.dtype)


def _sgu(p, lng, lnb, ws, bs_t):
    m = p.shape[0]
    w = BRANCH_W
    tb = SGU_TB
    d0 = P_D0 // w
    return pl.pallas_call(
        _sgu_kernel,
        name="sgu",
        grid=(m // tb,),
        in_specs=[pl.BlockSpec((tb, w), lambda i: (i, d0)),
                  pl.BlockSpec((tb, w), lambda i: (i, d0 + 1)),
                  pl.BlockSpec((tb, w), lambda i: (i, d0 + 2)),
                  pl.BlockSpec((1, w), lambda i: (0, 0)),
                  pl.BlockSpec((1, w), lambda i: (0, 0)),
                  pl.BlockSpec((SGU_GROUPS, SGU_CHUNK, SGU_CHUNK), lambda i: (0, 0, 0)),
                  pl.BlockSpec((SGU_CHUNK, SGU_GROUPS), lambda i: (0, 0))],
        out_specs=pl.BlockSpec((tb, w), lambda i: (i, 0)),
        out_shape=jax.ShapeDtypeStruct((m, w), BF16),
        compiler_params=_cparams(("parallel",)),
    )(p, p, p, lng.reshape(1, w), lnb.reshape(1, w), ws, bs_t)


def _merge_kernel(ya_ref, yb_ref, yc_ref, yd_ref, ga_ref, gb_ref, gc_ref, gd_ref, wp_ref, o_ref):
    acc = None
    for g, (y_ref, g_ref) in enumerate(((ya_ref, ga_ref), (yb_ref, gb_ref), (yc_ref, gc_ref), (yd_ref, gd_ref))):
        term = jax.nn.sigmoid(g_ref[...].astype(F32)) * _dot(y_ref[...], wp_ref[g])
        acc = term if acc is None else acc + term
    o_ref[...] = acc.astype(o_ref.dtype)


def _merge(ya, yb, yc, yd, p, wproj):
    m = ya.shape[0]
    w = BRANCH_W
    tm, tn = MERGE_TM, MERGE_TN
    nj = D_MODEL // tn
    m0 = P_M0 // tn
    ysp = pl.BlockSpec((tm, w), lambda j, i: (i, 0))
    gsp = [pl.BlockSpec((tm, tn), lambda j, i, g=g: (i, m0 + g * nj + j)) for g in range(N_BRANCH)]
    return pl.pallas_call(
        _merge_kernel,
        name="merge",
        grid=(nj, m // tm),
        in_specs=[ysp, ysp, ysp, ysp, *gsp,
                  pl.BlockSpec((N_BRANCH, w, tn), lambda j, i: (0, 0, j))],
        out_specs=pl.BlockSpec((tm, tn), lambda j, i: (i, j)),
        out_shape=jax.ShapeDtypeStruct((m, D_MODEL), BF16),
        compiler_params=_cparams(("parallel", "parallel")),
    )(ya, yb, yc, yd, p, p, p, p, wproj)


def _out_kernel(mg_ref, wo_ref, g_ref, x_ref, o_ref):
    y = _dot(mg_ref[...], wo_ref[...])
    y = y * lax.rsqrt(jnp.mean(y * y, axis=-1, keepdims=True) + NORM_EPS) * g_ref[...]
    o_ref[...] = x_ref[...] + y


def _out_proj(merged, wout, g, x2):
    m, d = x2.shape
    tm = OUT_TM
    return pl.pallas_call(
        _out_kernel,
        name="out_proj",
        grid=(m // tm,),
        in_specs=[pl.BlockSpec((tm, d), lambda i: (i, 0)),
                  pl.BlockSpec((d, d), lambda i: (0, 0)),
                  pl.BlockSpec((1, d), lambda i: (0, 0)),
                  pl.BlockSpec((tm, d), lambda i: (i, 0))],
        out_specs=pl.BlockSpec((tm, d), lambda i: (i, 0)),
        out_shape=jax.ShapeDtypeStruct((m, d), F32),
        compiler_params=_cparams(("parallel",)),
    )(merged, wout, g.reshape(1, d), x2)


def _split_cols(w, sizes):
    offs = np.cumsum((0,) + tuple(sizes))
    return [w[:, int(offs[i]):int(offs[i + 1])] for i in range(len(sizes))]


def _pad_cols(w, n):
    return jnp.pad(w, ((0, 0), (0, n - w.shape[1])))


def _pad_rows(w, n):
    return jnp.pad(w, ((0, n - w.shape[0]), (0, 0)))


def _rope_tables(positions, rot_dim, width):
    inv = ROPE_THETA ** (-jnp.arange(0, rot_dim, 2, dtype=F32) / rot_dim)
    ang = positions.astype(F32).reshape(-1, 1) * inv
    cos, sin = jnp.cos(ang), jnp.sin(ang)
    m = cos.shape[0]
    rest = width - rot_dim
    cos_h = jnp.concatenate([cos, cos, jnp.ones((m, rest), F32)], axis=1)
    sin_h = jnp.concatenate([-sin, sin, jnp.zeros((m, rest), F32)], axis=1)
    reps = LANES // width
    return jnp.tile(cos_h, (1, reps)), jnp.tile(sin_h, (1, reps))


def kernel(x, positions, norm_pre, norm_post, w_in, gla_w_g2, gla_b_g, gla_head_g, rwkv_mu, rwkv_w0,
           rwkv_w_w2, rwkv_a0, rwkv_w_a2, rwkv_k_k, rwkv_k_a, rwkv_r_k, rwkv_lnx_g, rwkv_lnx_b,
           sgu_ln_g, sgu_ln_b, sgu_w_s, sgu_b_s, w_proj, w_out):
    bsz, seq, d = x.shape
    depth = w_in.shape[0]
    m = bsz * seq
    x2 = x.reshape(m, d)
    ca, sa = _rope_tables(positions, DSA_ROT, DSA_HD)
    ci, si = _rope_tables(positions, IDX_ROT, IDX_HD)
    lane_grp = np.arange(LANES) // RWKV_HD
    bd = jnp.asarray((lane_grp[:, None] == lane_grp[None, :]).astype(np.float32))

    for l in range(depth):
        wa, wb, wc, wd, wm = _split_cols(w_in[l], GROUP_COLS)
        a_q, a_k, a_v, a_glr, a_gate = _split_cols(wa, A_COLS)
        b_q, b_k, b_v, b_iq, b_ik, b_iw, b_gate = _split_cols(wb, B_COLS)
        c_r, c_k, c_v, c_wlr, c_alr, c_gate = _split_cols(wc, C_COLS)
        w_main = jnp.concatenate([t.astype(BF16) for t in (wm, b_q, b_k, b_v, b_gate, c_r, c_k, c_v, c_gate, wd,
                                                            a_q, a_k, a_v, a_gate)], axis=1)
        w_small = jnp.concatenate([b_iq] + [_pad_cols(t, LANES) for t in (b_ik, b_iw, c_wlr, c_alr, a_glr)],
                                  axis=1).astype(BF16)

        p, h = _norm_matmul(x2, norm_pre[l], w_main, MM_TN)
        ps = _matmul(h, w_small, S_COLS, F32)

        ya = _gla(p, ps, _pad_rows(gla_w_g2[l], LANES), gla_b_g[l], gla_head_g[l], bsz, seq)

        qr, kr, vt, iqr, ikr, iws = _dsa_prep(p, ps, ca, sa, ci, si)
        yb = _dsa(qr, kr, vt, iqr, ikr, iws, p, bsz, seq)

        mu_r, mu_k, mu_v, mu_wlr, mu_alr, mu_gate = _split_cols(rwkv_mu[l][None, :], C_COLS)
        mu_main = jnp.concatenate([mu_r, mu_k, mu_v, mu_gate], axis=1)
        mu_s = jnp.concatenate([_pad_cols(mu_wlr, LANES), _pad_cols(mu_alr, LANES)], axis=1)
        r, wl, k2, v, kk, a, sg, bon = _rwkv_prep(
            p, ps, mu_main, mu_s, rwkv_w0[l], _pad_rows(rwkv_w_w2[l], LANES), rwkv_a0[l],
            _pad_rows(rwkv_w_a2[l], LANES), rwkv_k_k[l], rwkv_k_a[l], rwkv_r_k[l].reshape(-1), bd, seq)
        y = _rwkv_scan(r, wl, k2, v, kk, a, bsz, seq)
        yc = _rwkv_post(y, bon, sg, rwkv_lnx_g[l], rwkv_lnx_b[l], bd)

        yd = _sgu(p, sgu_ln_g[l], sgu_ln_b[l], sgu_w_s[l], sgu_b_s[l].T)

        merged = _merge(ya, yb, yc, yd, p, w_proj[l].astype(BF16))
        x2 = _out_proj(merged, w_out[l].astype(BF16), norm_post[l], x2)
    return x2.reshape(bsz, seq, d)
```

```python
import functools

import numpy as np
import jax
import jax.numpy as jnp
from jax import lax
from jax.experimental import pallas as pl
from jax.experimental.pallas import tpu as pltpu

F32 = jnp.float32
BF16 = jnp.bfloat16
I32 = jnp.int32
I16 = jnp.int16

D_MODEL = 2048
CHUNK = 64
NORM_EPS = 1e-6
ROPE_THETA = 500000.0
N_BRANCH = 4
BRANCH_W = D_MODEL // 2

GLA_HEADS = 4
GLA_DK = BRANCH_W // 2 // GLA_HEADS
GLA_DV = BRANCH_W // GLA_HEADS
GLA_RANK = 16
GLA_TAU = 16.0

DSA_HEADS = 8
DSA_HD = BRANCH_W // DSA_HEADS
DSA_ROT = DSA_HD // 4
IDX_HEADS = 8
IDX_HD = 64
IDX_ROT = IDX_HD // 4
TOPK_MAX = 256

RWKV_HD = 64
RWKV_HEADS = BRANCH_W // RWKV_HD
RWKV_DECAY_RANK = 96
RWKV_A_RANK = 96
RWKV_DECAY_SCALE = 0.606531
RWKV_GN_EPS = 64e-5
RWKV_KK_EPS = 1e-12

SGU_CHUNK = 128
SGU_GROUPS = 8
SGU_GW = BRANCH_W // SGU_GROUPS
SGU_LN_EPS = 1e-5

A_COLS = (GLA_HEADS * GLA_DK, GLA_HEADS * GLA_DK, BRANCH_W, GLA_RANK, BRANCH_W)
B_COLS = (BRANCH_W, BRANCH_W, BRANCH_W, IDX_HEADS * IDX_HD, IDX_HD, IDX_HEADS, BRANCH_W)
C_COLS = (BRANCH_W, BRANCH_W, BRANCH_W, RWKV_DECAY_RANK, RWKV_A_RANK, BRANCH_W)
D_COLS = (2 * BRANCH_W, BRANCH_W)
GROUP_COLS = (sum(A_COLS), sum(B_COLS), sum(C_COLS), sum(D_COLS), N_BRANCH * D_MODEL)

LANES = 128
SUBLANES = 8
VMEM_LIMIT = 56 * 1024 * 1024
MXU_DEPTH = 256
IDX_K = MXU_DEPTH
NEG_BIG = -1e30
LOG2E = 1.4426950408889634
INT_MIN = -(2 ** 31)

P_M0 = 0
P_B0 = P_M0 + N_BRANCH * D_MODEL
P_C0 = P_B0 + 4 * BRANCH_W
P_D0 = P_C0 + 4 * BRANCH_W
P_A0 = P_D0 + 3 * BRANCH_W
P_COLS = P_A0 + 2 * GLA_HEADS * GLA_DK + 2 * BRANCH_W
S_IDX0 = 0
S_IDX_COLS = IDX_HEADS * IDX_HD + 2 * 128
S_RW0 = S_IDX0 + S_IDX_COLS
S_GLR0 = S_RW0 + 2 * 128
S_COLS = S_GLR0 + 128

MM_TM = 1024
MM_TN = 1024
GLA_TB = 2048
DSA_TQ = 256
DSA_KB = 512
DSA_VROWS = DSA_HD + 16
RWKV_PREP_TB = 256
RWKV_L = 64
RWKV_UNROLL = 8
RWKV_POST_TB = 512
SGU_TB = 512
MERGE_TM = 512
MERGE_TN = 1024
OUT_TM = 512


def _cparams(sem):
    return pltpu.CompilerParams(dimension_semantics=sem, vmem_limit_bytes=VMEM_LIMIT)


def _dot(a, b):
    return jnp.dot(a, b, preferred_element_type=F32)


def _dot_nt(a, b):
    return lax.dot_general(a, b, (((1,), (1,)), ((), ())), preferred_element_type=F32)


def _dot_tn(a, b):
    return lax.dot_general(a, b, (((0,), (0,)), ((), ())), preferred_element_type=F32)


def _bf16_terms(x, n):
    terms = []
    for _ in range(n):
        t = x.astype(BF16)
        terms.append(t)
        x = x - t.astype(F32)
    return terms


def _dot_exact_lhs(a, b, nterms):
    a16 = a.astype(BF16)
    out = None
    for t in _bf16_terms(b, nterms):
        d = _dot(a16, t)
        out = d if out is None else out + d
    return out


def _dot_exact_rhs(a, b, nterms):
    b16 = b.astype(BF16)
    out = None
    for t in _bf16_terms(a, nterms):
        d = _dot(t, b16)
        out = d if out is None else out + d
    return out


def _dot_3pass(a, b, b_lo=None):
    a_hi, a_lo = _bf16_terms(a, 2)
    b_hi, b_lo = _bf16_terms(b, 2) if b_lo is None else (b, b_lo)
    return _dot(a_hi, b_hi) + _dot(a_lo, b_hi) + _dot(a_hi, b_lo)


def _silu(x):
    return x * jax.nn.sigmoid(x)


def _norm_matmul_kernel(x_ref, g_ref, b_ref, o_ref, h_ref):
    @pl.when(pl.program_id(1) == 0)
    def _():
        x = x_ref[...]
        ms = jnp.mean(x * x, axis=-1, keepdims=True)
        h_ref[...] = (x * lax.rsqrt(ms + NORM_EPS) * g_ref[...]).astype(h_ref.dtype)

    o_ref[...] = _dot(h_ref[...], b_ref[...]).astype(o_ref.dtype)


def _norm_matmul(x2, g, b, tn):
    m, k = x2.shape
    n = b.shape[1]
    tm = min(MM_TM, m)
    return pl.pallas_call(
        _norm_matmul_kernel,
        name="proj",
        grid=(m // tm, n // tn),
        in_specs=[pl.BlockSpec((tm, k), lambda i, j: (i, 0)),
                  pl.BlockSpec((1, k), lambda i, j: (0, 0)),
                  pl.BlockSpec((k, tn), lambda i, j: (0, j))],
        out_specs=[pl.BlockSpec((tm, tn), lambda i, j: (i, j)),
                   pl.BlockSpec((tm, k), lambda i, j: (i, 0))],
        out_shape=[jax.ShapeDtypeStruct((m, n), BF16), jax.ShapeDtypeStruct((m, k), BF16)],
        compiler_params=_cparams(("parallel", "arbitrary")),
    )(x2, g.reshape(1, k), b)


def _matmul_kernel(a_ref, b_ref, o_ref):
    o_ref[...] = _dot(a_ref[...], b_ref[...]).astype(o_ref.dtype)


def _matmul(a, b, tn, out_dtype=F32):
    m, k = a.shape
    n = b.shape[1]
    tm = min(MM_TM, m)
    return pl.pallas_call(
        _matmul_kernel,
        name="proj",
        grid=(m // tm, n // tn),
        in_specs=[pl.BlockSpec((tm, k), lambda i, j: (i, 0)),
                  pl.BlockSpec((k, tn), lambda i, j: (0, j))],
        out_specs=pl.BlockSpec((tm, tn), lambda i, j: (i, j)),
        out_shape=jax.ShapeDtypeStruct((m, n), out_dtype),
        compiler_params=_cparams(("parallel", "parallel")),
    )(a, b)


def _gla_kernel(q_ref, k_ref, v_ref, gate_ref, glr_ref, wg2_ref, bg_ref, hg_ref, o_ref, state_ref):
    @pl.when(pl.program_id(2) == 0)
    def _():
        state_ref[...] = jnp.zeros_like(state_ref)

    ii = lax.broadcasted_iota(I32, (CHUNK, CHUNK), 0)
    jj = lax.broadcasted_iota(I32, (CHUNK, CHUNK), 1)
    tril = (ii >= jj).astype(F32)
    chunks = [pl.ds(c * CHUNK, CHUNK) for c in range(q_ref.shape[0] // CHUNK)]
    z = _dot_3pass(glr_ref[...], wg2_ref[...]) + bg_ref[...]
    log_a = (jnp.minimum(z, 0.0) - jnp.log1p(jnp.exp(-jnp.abs(z)))) / GLA_TAU
    cum = [_dot_exact_lhs(tril, log_a[c * CHUNK:(c + 1) * CHUNK], 3) for c in range(len(chunks))]
    total = [x[CHUNK - 1:CHUNK, :] for x in cum]
    k_dec = [k_ref[sl, :].astype(F32) * jnp.exp(t - x) for sl, t, x in zip(chunks, total, cum)]
    kv = [_dot_tn(v_ref[sl, :], kd.astype(BF16)) for sl, kd in zip(chunks, k_dec)]
    states = []
    st = state_ref[...]
    for t, x in zip(total, kv):
        st = st * jnp.exp(t) + x
        states.append(st)
    state_ref[...] = st
    outs = [_dot_nt((q_ref[sl, :].astype(F32) * (GLA_DK ** -0.5)).astype(BF16), s.astype(BF16))
            for sl, s in zip(chunks, states)]
    for sl, o in zip(chunks, outs):
        o = o * lax.rsqrt(jnp.mean(o * o, axis=-1, keepdims=True) + NORM_EPS) * hg_ref[...]
        o_ref[sl, :] = (o * _silu(gate_ref[sl, :].astype(F32))).astype(o_ref.dtype)


def _gla(p, ps, wg2p, bg, hg, bsz, seq):
    m = p.shape[0]
    tb = min(GLA_TB, seq)
    nt = seq // tb
    row = lambda b, h, t: b * nt + t
    qb = P_A0 // GLA_DK
    kb = qb + GLA_HEADS
    vb = (P_A0 + 2 * GLA_HEADS * GLA_DK) // GLA_DV
    gb = vb + GLA_HEADS
    lb = S_GLR0 // LANES
    return pl.pallas_call(
        _gla_kernel,
        name="gla",
        grid=(bsz, GLA_HEADS, nt),
        in_specs=[
            pl.BlockSpec((tb, GLA_DK), lambda b, h, t: (row(b, h, t), qb + h)),
            pl.BlockSpec((tb, GLA_DK), lambda b, h, t: (row(b, h, t), kb + h)),
            pl.BlockSpec((tb, GLA_DV), lambda b, h, t: (row(b, h, t), vb + h)),
            pl.BlockSpec((tb, GLA_DV), lambda b, h, t: (row(b, h, t), gb + h)),
            pl.BlockSpec((tb, LANES), lambda b, h, t: (row(b, h, t), lb)),
            pl.BlockSpec((LANES, GLA_DK), lambda b, h, t: (0, h)),
            pl.BlockSpec((1, GLA_DK), lambda b, h, t: (0, h)),
            pl.BlockSpec((1, GLA_DV), lambda b, h, t: (0, 0)),
        ],
        out_specs=pl.BlockSpec((tb, GLA_DV), lambda b, h, t: (row(b, h, t), h)),
        out_shape=jax.ShapeDtypeStruct((m, BRANCH_W), BF16),
        scratch_shapes=[pltpu.VMEM((GLA_DV, GLA_DK), F32)],
        compiler_params=_cparams(("parallel", "parallel", "arbitrary")),
    )(p, p, p, p, ps, wg2p, bg.reshape(1, -1), hg.reshape(1, -1))


def _rope_tile(x, cos_t, sin_t, half, width):
    lane = lax.broadcasted_iota(I32, x.shape, 1) % width
    partner = jnp.where(lane < half, pltpu.roll(x, LANES - half, 1), pltpu.roll(x, half, 1))
    return x * cos_t + partner * sin_t


def _dsa_prep_kernel(q_ref, k_ref, v_ref, idx_ref, ca_ref, sa_ref, ci_ref, si_ref,
                     qo_ref, ko_ref, vo_ref, iqo_ref, iko_ref, iwo_ref):
    ca, sa = ca_ref[...], sa_ref[...]
    ci, si = ci_ref[...], si_ref[...]
    for h in range(DSA_HEADS):
        sl = slice(h * DSA_HD, (h + 1) * DSA_HD)
        q = _rope_tile(q_ref[:, sl].astype(F32), ca, sa, DSA_ROT // 2, DSA_HD) * (DSA_HD ** -0.5 * LOG2E)
        qo_ref[:, sl] = q.astype(qo_ref.dtype)
        ko_ref[:, sl] = _rope_tile(k_ref[:, sl].astype(F32), ca, sa, DSA_ROT // 2, DSA_HD).astype(ko_ref.dtype)
        vo_ref[0, h * DSA_VROWS:h * DSA_VROWS + DSA_HD, :] = v_ref[:, sl].astype(F32).T.astype(vo_ref.dtype)
        extra = lax.broadcasted_iota(I32, (DSA_VROWS - DSA_HD, q_ref.shape[0]), 0) == 0
        vo_ref[0, h * DSA_VROWS + DSA_HD:(h + 1) * DSA_VROWS, :] = jnp.where(extra, 1.0, 0.0).astype(vo_ref.dtype)
    niq = IDX_HEADS * IDX_HD
    low = lax.broadcasted_iota(I32, (q_ref.shape[0], LANES), 1) < IDX_HD

    def hi_lo(x):
        hi = x.astype(BF16).astype(F32)
        return hi, x - hi

    for j in range(niq // LANES):
        x = _rope_tile(idx_ref[:, j * LANES:(j + 1) * LANES], ci, si, IDX_ROT // 2, IDX_HD)
        hi, lo = hi_lo(x)
        hi_sw = pltpu.roll(hi, IDX_HD, 1)
        lo_sw = pltpu.roll(lo, IDX_HD, 1)
        base = 2 * j * IDX_K
        iqo_ref[:, base:base + LANES] = jnp.where(low, hi, hi_sw).astype(BF16)
        iqo_ref[:, base + LANES:base + IDX_K] = jnp.where(low, lo, 0.0).astype(BF16)
        iqo_ref[:, base + IDX_K:base + IDX_K + LANES] = jnp.where(low, hi_sw, hi).astype(BF16)
        iqo_ref[:, base + IDX_K + LANES:base + 2 * IDX_K] = jnp.where(low, lo_sw, 0.0).astype(BF16)
    hi, lo = hi_lo(_rope_tile(idx_ref[:, niq:niq + LANES], ci, si, IDX_ROT // 2, IDX_HD))
    iko_ref[:, :LANES] = jnp.where(low, hi, pltpu.roll(lo, IDX_HD, 1)).astype(BF16)
    iko_ref[:, LANES:] = jnp.where(low, hi, 0.0).astype(BF16)
    iwo_ref[...] = idx_ref[:, niq + LANES:niq + 2 * LANES] * (niq ** -0.5)


def _dsa_prep(p, ps, ca, sa, ci, si):
    m = p.shape[0]
    tb = DSA_KB
    w = BRANCH_W
    rows = lambda i: (i, 0)
    b0 = P_B0 // w
    return pl.pallas_call(
        _dsa_prep_kernel,
        name="dsa_prep",
        grid=(m // tb,),
        in_specs=[pl.BlockSpec((tb, w), lambda i: (i, b0)),
                  pl.BlockSpec((tb, w), lambda i: (i, b0 + 1)),
                  pl.BlockSpec((tb, w), lambda i: (i, b0 + 2)),
                  pl.BlockSpec((tb, S_IDX_COLS), lambda i: (i, S_IDX0 // S_IDX_COLS)),
                  pl.BlockSpec((tb, LANES), rows), pl.BlockSpec((tb, LANES), rows),
                  pl.BlockSpec((tb, LANES), rows), pl.BlockSpec((tb, LANES), rows)],
        out_specs=[pl.BlockSpec((tb, w), rows), pl.BlockSpec((tb, w), rows),
                   pl.BlockSpec((1, DSA_HEADS * DSA_VROWS, tb), lambda i: (i, 0, 0)),
                   pl.BlockSpec((tb, IDX_HEADS * IDX_K), rows), pl.BlockSpec((tb, IDX_K), rows),
                   pl.BlockSpec((tb, LANES), rows)],
        out_shape=[jax.ShapeDtypeStruct((m, w), BF16), jax.ShapeDtypeStruct((m, w), BF16),
                   jax.ShapeDtypeStruct((m // tb, DSA_HEADS * DSA_VROWS, tb), BF16),
                   jax.ShapeDtypeStruct((m, IDX_HEADS * IDX_K), BF16),
                   jax.ShapeDtypeStruct((m, IDX_K), BF16), jax.ShapeDtypeStruct((m, LANES), F32)],
        compiler_params=_cparams(("parallel",)),
    )(p, p, p, ps, ca, sa, ci, si)


def _dsa_kernel(q_ref, k_ref, vt_ref, iq_ref, ik_ref, iw_ref, gate_ref, o_ref, key_scr, hi_scr, tie_scr,
                *, topk):
    tq, kb_w = DSA_TQ, DSA_KB
    q0 = pl.program_id(1) * tq
    nkb = (q0 + tq + kb_w - 1) // kb_w
    qcol = lax.broadcasted_iota(I32, (1, tq), 1)
    qlim = ((q0 + qcol) // CHUNK + 1) * CHUNK
    sub_k = lax.broadcasted_iota(I32, (kb_w, 1), 0)

    def score_body(kb, carry):
        ik = ik_ref[pl.ds(pl.multiple_of(kb * kb_w, kb_w), kb_w), :]
        dots = [_dot_nt(ik, iq_ref[:, h * IDX_K:(h + 1) * IDX_K]) for h in range(IDX_HEADS)]
        acc = jnp.zeros((kb_w, tq), F32)
        for h, d in enumerate(dots):
            acc = acc + iw_ref[h:h + 1, :] * jnp.maximum(d, 0.0)
        acc = acc + 0.0
        bits = lax.bitcast_convert_type(acc, I32)
        key = bits ^ ((bits >> 31) & 0x7FFFFFFF)
        sidx = kb * kb_w + sub_k
        key = jnp.where(sidx < qlim, key, INT_MIN)
        key_scr[kb] = key
        hi_scr[kb] = (key >> 16).astype(I16)
        return carry

    lax.fori_loop(0, nkb, score_body, 0)

    def count_tiles(load, pred, rows_per_vreg, dtype):
        def body(kb, c):
            m = pred(load(kb), kb * kb_w + sub_k).astype(dtype)
            if dtype == I32:
                nparts = 8
                rows_part = kb_w // nparts
                parts = [jnp.sum(m[j * rows_part:(j + 1) * rows_part]
                                 .reshape(rows_part // rows_per_vreg, rows_per_vreg, tq), axis=0)
                         for j in range(nparts)]
            else:
                parts = [m[j * rows_per_vreg:(j + 1) * rows_per_vreg] for j in range(kb_w // rows_per_vreg)]
            while len(parts) > 1:
                parts = [a + b for a, b in zip(parts[::2], parts[1::2])]
            return c + parts[0]
        c = lax.fori_loop(0, nkb, body, jnp.zeros((rows_per_vreg, tq), dtype))
        return jnp.sum(c.astype(I32), axis=0, keepdims=True)

    def count(pred):
        return count_tiles(lambda kb: key_scr[kb], pred, SUBLANES, I32)

    def count_hi(pred):
        return count_tiles(lambda kb: hi_scr[kb], pred, 2 * SUBLANES, I16)

    def hi_body(it, carry):
        res, cnt_res = carry
        trial = res | lax.shift_left(jnp.int32(1), 31 - it)
        cand_hi = ((trial ^ INT_MIN) >> 16).astype(I16)
        cnt = count_hi(lambda kt, s: kt >= cand_hi)
        ok = cnt >= topk
        return jnp.where(ok, trial, res), jnp.where(ok, cnt, cnt_res)

    res_hi, cnt_ge_hi = lax.fori_loop(0, 16, hi_body,
                                      (jnp.zeros((1, tq), I32), jnp.full((1, tq), nkb * kb_w, I32)))
    thr_hi = ((res_hi ^ INT_MIN) >> 16).astype(I16)
    cnt_gt_hi = count_hi(lambda kt, s: kt > thr_hi)
    lo_min = -(2 ** 15)

    def lo_prep(kb, carry):
        lo = ((key_scr[kb] & 0xFFFF) + lo_min).astype(I16)
        hi_scr[kb] = jnp.where(hi_scr[kb] == thr_hi, lo, I16(lo_min))
        return carry

    lax.fori_loop(0, nkb, lo_prep, 0)
    places = topk - cnt_gt_hi

    def lo_body(it, carry):
        res, cnt_res = carry
        trial = res | lax.shift_left(jnp.int32(1), 15 - it)
        cand_lo = (trial + lo_min).astype(I16)
        cnt = count_hi(lambda kt, s: kt >= cand_lo)
        ok = cnt >= places
        return jnp.where(ok, trial, res), jnp.where(ok, cnt, cnt_res)

    res_lo, cnt_ge_lo = lax.fori_loop(0, 16, lo_body, (jnp.zeros((1, tq), I32), cnt_ge_hi - cnt_gt_hi))
    thr = (res_hi ^ INT_MIN) | res_lo
    cnt_ge = cnt_gt_hi + cnt_ge_lo
    seq_len = key_scr.shape[0] * kb_w
    tie_scr[...] = jnp.full(tie_scr.shape, seq_len, I32)
    has_tie = (cnt_ge > topk) & (thr != INT_MIN)

    @pl.when(jnp.max(jnp.where(has_tie, 1.0, 0.0)) > 0.5)
    def _():
        need = topk - count(lambda kt, s: kt > thr)
        nbits = max(1, int(np.ceil(np.log2(seq_len))))

        def tie_body(it, resj):
            trial = resj | lax.shift_left(jnp.int32(1), nbits - 1 - it)
            c = count(lambda kt, s: (kt == thr) & (s < trial))
            return jnp.where(c < need, trial, resj)

        tie_scr[...] = lax.fori_loop(0, nbits, tie_body, jnp.zeros((1, tq), I32))

    last_tie = tie_scr[...]

    def softmax_mask(kb):
        kt = key_scr[kb]
        sidx = kb * kb_w + sub_k
        sel = ((kt > thr) | ((kt == thr) & (sidx <= last_tie))) & (kt != INT_MIN)
        return jnp.where(sel, 0.0, NEG_BIG).astype(BF16)

    heads = [slice(h * DSA_HD, (h + 1) * DSA_HD) for h in range(DSA_HEADS)]

    def att_body(kb, stats):
        rows = pl.ds(pl.multiple_of(kb * kb_w, kb_w), kb_w)
        bias = softmax_mask(kb)
        s = [_dot_nt(k_ref[rows, sl], q_ref[:, sl]).astype(BF16) + bias for sl in heads]
        m_n = [jnp.maximum(c[0], jnp.max(x, axis=0, keepdims=True).astype(F32)) for c, x in zip(stats, s)]
        p = [jnp.exp2(x - m.astype(BF16)) for x, m in zip(s, m_n)]
        pv = [_dot(vt_ref[kb, h * DSA_VROWS:(h + 1) * DSA_VROWS, :], x) for h, x in enumerate(p)]
        alpha = [jnp.exp2(c[0] - m) for c, m in zip(stats, m_n)]
        return tuple((m, a * c[1] + y) for m, a, c, y in zip(m_n, alpha, stats, pv))

    init = tuple((jnp.full((1, tq), NEG_BIG, F32), jnp.zeros((DSA_VROWS, tq), F32)) for _ in heads)
    final = lax.fori_loop(0, nkb, att_body, init)
    for sl, (_, acc_f) in zip(heads, final):
        o = (acc_f[:DSA_HD] / acc_f[DSA_HD:DSA_HD + 1]).T
        o_ref[:, sl] = (o * _silu(gate_ref[:, sl].astype(F32))).astype(o_ref.dtype)


def _dsa(qr, kr, vt, iqr, ikr, iws, p, bsz, seq):
    m = qr.shape[0]
    w = BRANCH_W
    nq = seq // DSA_TQ
    nkb = seq // DSA_KB
    topk = min(TOPK_MAX, seq // 4)
    qrow = lambda b, i: (b * nq + i, 0)
    return pl.pallas_call(
        functools.partial(_dsa_kernel, topk=topk),
        name="dsa",
        grid=(bsz, nq),
        in_specs=[pl.BlockSpec((DSA_TQ, w), qrow),
                  pl.BlockSpec((seq, w), lambda b, i: (b, 0), pipeline_mode=pl.Buffered(1)),
                  pl.BlockSpec((nkb, DSA_HEADS * DSA_VROWS, DSA_KB), lambda b, i: (b, 0, 0),
                               pipeline_mode=pl.Buffered(1)),
                  pl.BlockSpec((DSA_TQ, IDX_HEADS * IDX_K), qrow),
                  pl.BlockSpec((seq, IDX_K), lambda b, i: (b, 0), pipeline_mode=pl.Buffered(1)),
                  pl.BlockSpec((IDX_HEADS, DSA_TQ), lambda b, i: (0, b * nq + i)),
                  pl.BlockSpec((DSA_TQ, w), lambda b, i: (b * nq + i, P_B0 // w + 3))],
        out_specs=pl.BlockSpec((DSA_TQ, w), qrow),
        out_shape=jax.ShapeDtypeStruct((m, w), BF16),
        scratch_shapes=[pltpu.VMEM((nkb, DSA_KB, DSA_TQ), I32),
                        pltpu.VMEM((nkb, DSA_KB, DSA_TQ), I16),
                        pltpu.VMEM((1, DSA_TQ), I32)],
        compiler_params=_cparams(("parallel", "arbitrary")),
    )(qr, kr, vt, iqr, ikr, iws[:, :IDX_HEADS].T, p)


def _head_sum(x, bd):
    parts = [_dot_exact_rhs(x[:, j * LANES:(j + 1) * LANES], bd, 2) for j in range(x.shape[1] // LANES)]
    return jnp.concatenate(parts, axis=1)


def _rwkv_prep_kernel(c_ref, cs_ref, pc_ref, pcs_ref, mu_ref, mus_ref, w0_ref, ww2_ref, ww2lo_ref,
                      a0_ref, wa2_ref, wa2lo_ref, kkw_ref, ka_ref, rk_ref, bd_ref,
                      r_o, wl_o, k_o, v_o, kk_o, a_o, sg_o, bon_o, *, blocks_per_seq):
    tb = c_ref.shape[0]
    first = (pl.program_id(0) % blocks_per_seq) == 0
    rowi = lax.broadcasted_iota(I32, (tb, 1), 0)

    def shift_lerp(c, prev_rows, mu):
        last = prev_rows.shape[0] - 1
        prev = jnp.where(first, 0.0, prev_rows[last:last + 1, :])
        sh = jnp.where(rowi == 0, prev, pltpu.roll(c, 1, 0))
        return c + (sh - c) * mu

    w = BRANCH_W
    c = shift_lerp(c_ref[...].astype(F32), pc_ref[...].astype(F32), mu_ref[...])
    cs = shift_lerp(cs_ref[...], pcs_ref[...], mus_ref[...])
    r, k, v, gate = c[:, :w], c[:, w:2 * w], c[:, 2 * w:3 * w], c[:, 3 * w:]
    w_log = -RWKV_DECAY_SCALE * jax.nn.sigmoid(
        w0_ref[...] + _dot_3pass(jnp.tanh(cs[:, :LANES]), ww2_ref[...], ww2lo_ref[...]))
    a = jax.nn.sigmoid(a0_ref[...] + _dot_3pass(cs[:, LANES:], wa2_ref[...], wa2lo_ref[...]))
    kk = k * kkw_ref[...]
    k2 = k * (1.0 + (a - 1.0) * ka_ref[...])
    bd = bd_ref[...]
    kk = kk * lax.rsqrt(_head_sum(kk * kk, bd) + RWKV_KK_EPS)
    r_o[...] = r.astype(r_o.dtype)
    wl_o[...] = w_log
    k_o[...] = k2.astype(k_o.dtype)
    v_o[...] = v.astype(v_o.dtype)
    kk_o[...] = kk.astype(kk_o.dtype)
    a_o[...] = a.astype(a_o.dtype)
    sg_o[...] = _silu(gate).astype(sg_o.dtype)
    bon_o[...] = (_head_sum(r * k2 * rk_ref[...], bd) * v).astype(bon_o.dtype)


def _rwkv_prep(p, ps, mu_main, mu_s, w0, ww2p, a0, wa2p, kkw, ka, rk, bd, seq):
    m = p.shape[0]
    tb = RWKV_PREP_TB
    w = BRANCH_W
    pr16, pr8 = 2 * SUBLANES, SUBLANES
    cblk, sblk = P_C0 // (4 * w), S_RW0 // (2 * LANES)
    const = lambda i: (0, 0)
    vec = pl.BlockSpec((1, w), const)
    lowrank = pl.BlockSpec((LANES, w), const)
    ww2_hi, ww2_lo = _bf16_terms(ww2p, 2)
    wa2_hi, wa2_lo = _bf16_terms(wa2p, 2)
    out = pl.BlockSpec((tb, w), lambda i: (i, 0))
    dts = [BF16, F32, BF16, BF16, BF16, BF16, BF16, BF16]
    return pl.pallas_call(
        functools.partial(_rwkv_prep_kernel, blocks_per_seq=seq // tb),
        name="rwkv_prep",
        grid=(m // tb,),
        in_specs=[pl.BlockSpec((tb, 4 * w), lambda i: (i, cblk)),
                  pl.BlockSpec((tb, 2 * LANES), lambda i: (i, sblk)),
                  pl.BlockSpec((pr16, 4 * w), lambda i: (jnp.maximum(i * (tb // pr16) - 1, 0), cblk)),
                  pl.BlockSpec((pr8, 2 * LANES), lambda i: (jnp.maximum(i * (tb // pr8) - 1, 0), sblk)),
                  pl.BlockSpec((1, 4 * w), const), pl.BlockSpec((1, 2 * LANES), const),
                  vec, lowrank, lowrank, vec, lowrank, lowrank,
                  vec, vec, vec, pl.BlockSpec((LANES, LANES), const)],
        out_specs=[out] * 8,
        out_shape=[jax.ShapeDtypeStruct((m, w), dt) for dt in dts],
        compiler_params=_cparams(("parallel",)),
    )(p, ps, p, ps, mu_main, mu_s, w0.reshape(1, w), ww2_hi, ww2_lo, a0.reshape(1, w), wa2_hi, wa2_lo,
      kkw.reshape(1, w), ka.reshape(1, w), rk.reshape(1, w), bd)


def _rwkv_scan_kernel(r_ref, wl_ref, k_ref, v_ref, kk_ref, a_ref, y_ref, lin_scr, yq_scr, h_scr, *, unroll, nseq):
    n = RWKV_L
    two = 2 * n
    nc = r_ref.shape[0] // n
    i2 = lax.broadcasted_iota(I32, (two, two), 0)
    j2 = lax.broadcasted_iota(I32, (two, two), 1)
    same_head = (i2 // RWKV_HD) == (j2 // RWKV_HD)
    eye2 = (i2 == j2).astype(F32)
    ti = lax.broadcasted_iota(I32, (n, LANES), 0)
    ci = lax.broadcasted_iota(I32, (n, LANES), 1) % n
    strict = ti > ci
    incl = ti >= ci
    same16 = (ti // 16) == (ci // 16)
    eye = (ti == ci).astype(F32)
    ii = lax.broadcasted_iota(I32, (n, n), 0)
    jj = lax.broadcasted_iota(I32, (n, n), 1)
    tril = (ii >= jj).astype(F32)
    head0 = lax.broadcasted_iota(I32, (n, LANES), 1) < RWKV_HD

    def mm(a, b):
        return _dot(a.astype(BF16), b.astype(BF16))

    def mm_nt(a, b):
        return _dot_nt(a.astype(BF16), b.astype(BF16))

    def mm_tn(a, b):
        return _dot_tn(a.astype(BF16), b.astype(BF16))

    def stack(x):
        return jnp.concatenate([jnp.where(head0, x, 0.0), jnp.where(head0, 0.0, x)], axis=0)

    def fold(x):
        return x[:n] + x[n:]

    def each(f, *lists):
        return [f(*args) for args in zip(*lists)]

    def tri_inv(a_low):
        a_d = each(lambda a: jnp.where(same16, a, 0.0), a_low)
        a_o = each(lambda a, d: a - d, a_low, a_d)
        ad_st = each(stack, a_d)
        a2 = each(mm, a_d, ad_st)
        tick()
        a34 = each(lambda s, d: mm(s, jnp.concatenate([d, stack(s)], axis=1)), a2, ad_st)
        tick()
        a4_st = each(lambda x: stack(x[:, LANES:]), a34)
        s4 = each(lambda d, s, x: eye + d + s + x[:, :LANES], a_d, a2, a34)
        a8 = each(lambda x, y: mm(x[:, LANES:], y), a34, a4_st)
        tick()
        s8 = each(lambda s, x: s + mm(s, x), s4, a4_st)
        tick()
        t_d = each(lambda s, x: s + mm(s, stack(x)), s8, a8)
        tick()
        nn = each(lambda t, o: mm(t, stack(o)), t_d, a_o)
        tick()
        nx = each(lambda x, t: mm(x, jnp.concatenate([stack(x), stack(t)], axis=1)), nn, t_d)
        tick()
        x1 = each(lambda t, x: t + x[:, LANES:], t_d, nx)
        return each(lambda x, y: y + mm(x[:, :LANES], stack(y)), nx, x1)

    pending = []

    def tick():
        for _ in range(-(-unroll // 8)):
            if pending:
                pending.pop(0)()

    def build_group(g):
        rows = [pl.ds(pl.multiple_of((g * unroll + u) * n, n), n) for u in range(unroll)]
        wl = [wl_ref[r, :] for r in rows]
        k = [k_ref[r, :].astype(F32) for r in rows]
        v = [v_ref[r, :].astype(F32) for r in rows]
        kk = [kk_ref[r, :].astype(F32) for r in rows]
        beta = each(lambda x, r: x * a_ref[r, :].astype(F32), kk, rows)
        cum = each(lambda x: _dot_exact_lhs(tril, x, 3), wl)
        cum_end = each(lambda x: x[n - 1:n, :], cum)
        g_inv = each(lambda x: jnp.exp(-x), cum)
        g_end = each(lambda e, x: jnp.exp(e - x), cum_end, cum)
        a_t = each(lambda x, c, w: -x * jnp.exp(c - w), kk, cum, wl)
        r_t = each(lambda r, c: r_ref[r, :].astype(F32) * jnp.exp(c), rows, cum)
        bk_st = each(lambda x, y, z: jnp.concatenate([stack(x * z), stack(y * z)], axis=0), beta, k, g_inv)
        ar = each(lambda x, y: jnp.concatenate([x, y], axis=0), a_t, r_t)
        gram = each(mm_nt, ar, bk_st)
        a_ab = each(lambda x: jnp.where(strict, x[:n, :LANES], 0.0), gram)
        a_rb = each(lambda x: jnp.where(incl, x[n:, :LANES], 0.0), gram)
        a_ak = each(lambda x: jnp.where(strict, x[:n, LANES:], 0.0), gram)
        a_rk = each(lambda x: jnp.where(incl, x[n:, LANES:], 0.0), gram)
        av2 = each(lambda x, y, z: mm(jnp.concatenate([x, y], axis=0), stack(z)), a_ak, a_rk, v)
        t_inv = tri_inv(a_ab)
        tick()
        pq = each(lambda t, x, y: mm(t, jnp.concatenate([stack(x), stack(y[:n])], axis=1)), t_inv, a_t, av2)
        yr = each(lambda x, y: mm(x, jnp.concatenate([stack(y[:, :LANES]), stack(y[:, LANES:])], axis=1)), a_rb, pq)
        kv = each(lambda x, y, z: mm_tn(x * y, z), k, g_end, v)
        gh = each(lambda x, y, z: mm_tn(x * y, z), beta, g_end, pq)
        while pending:
            tick()
        for u in range(unroll):
            ry = r_t[u] + yr[u][:, :LANES]
            g_bd = jnp.where(same_head, gh[u][:, :LANES], 0.0) + eye2 * jnp.exp(cum_end[u])
            lin_scr[u] = jnp.concatenate([ry, g_bd], axis=0)
            yq_scr[u] = yr[u][:, LANES:] + av2[u][n:]
            h_scr[u] = jnp.where(same_head, gh[u][:, LANES:] + kv[u], 0.0)

    def sweep_steps(g, state):
        def step(u):
            rows = pl.ds(pl.multiple_of((g * unroll + u) * n, n), n)
            both = mm(lin_scr[u], state[0])
            y_ref[rows, :] = both[:n] + yq_scr[u]
            state[0] = both[n:] + h_scr[u]
        return [functools.partial(step, u) for u in range(unroll)]

    ngroups = nc // unroll
    groups_per_seq = ngroups // nseq
    build_group(0)

    def body(g, st):
        state = [st]
        pending.extend(sweep_steps(g - 1, state))
        build_group(g)
        return jnp.where(g % groups_per_seq == 0, 0.0, state[0])

    st = lax.fori_loop(1, ngroups, body, jnp.zeros((two, two), F32))
    state = [st]
    for step in sweep_steps(ngroups - 1, state):
        step()


def _rwkv_scan(r, wl, k2, v, kk, a, bsz, seq):
    m = r.shape[0]
    npair = BRANCH_W // LANES
    nseq = 2 if bsz % 2 == 0 else 1
    blk = pl.BlockSpec((nseq * seq, LANES), lambda b, p: (b, p))
    u = RWKV_UNROLL
    return pl.pallas_call(
        functools.partial(_rwkv_scan_kernel, unroll=u, nseq=nseq),
        name="rwkv_scan",
        grid=(bsz // nseq, npair),
        in_specs=[blk] * 6,
        out_specs=blk,
        out_shape=jax.ShapeDtypeStruct((m, BRANCH_W), F32),
        scratch_shapes=[pltpu.VMEM((u, 3 * RWKV_L, LANES), F32), pltpu.VMEM((u, RWKV_L, LANES), F32),
                        pltpu.VMEM((u, LANES, LANES), F32)],
        compiler_params=_cparams(("parallel", "parallel")),
    )(r, wl, k2, v, kk, a)


def _rwkv_post_kernel(y_ref, bon_ref, sg_ref, g_ref, b_ref, bd_ref, o_ref):
    y = y_ref[...]
    bd = bd_ref[...]
    mu = _head_sum(y, bd) * (1.0 / RWKV_HD)
    yc = y - mu
    var = _head_sum(yc * yc, bd) * (1.0 / RWKV_HD)
    yn = yc * lax.rsqrt(var + RWKV_GN_EPS) * g_ref[...] + b_ref[...]
    o_ref[...] = ((yn + bon_ref[...].astype(F32)) * sg_ref[...].astype(F32)).astype(o_ref.dtype)


def _rwkv_post(y, bon, sg, g, b, bd):
    m, w = y.shape
    tb = RWKV_POST_TB
    rows = pl.BlockSpec((tb, w), lambda i: (i, 0))
    vec = pl.BlockSpec((1, w), lambda i: (0, 0))
    return pl.pallas_call(
        _rwkv_post_kernel,
        name="rwkv_post",
        grid=(m // tb,),
        in_specs=[rows, rows, rows, vec, vec, pl.BlockSpec((LANES, LANES), lambda i: (0, 0))],
        out_specs=rows,
        out_shape=jax.ShapeDtypeStruct((m, w), BF16),
        compiler_params=_cparams(("parallel",)),
    )(y, bon, sg, g.reshape(1, w), b.reshape(1, w), bd)


def _sgu_kernel(u_ref, v_ref, gate_ref, lng_ref, lnb_ref, ws_ref, bs_ref, o_ref):
    inv_sqrt2 = 0.7071067811865476

    def gelu(z):
        return 0.5 * z * (1.0 + lax.erf(z * inv_sqrt2))

    v = gelu(v_ref[...].astype(F32))
    mu = jnp.mean(v, axis=-1, keepdims=True)
    vc = v - mu
    var = jnp.mean(vc * vc, axis=-1, keepdims=True)
    vn = vc * lax.rsqrt(var + SGU_LN_EPS) * lng_ref[...] + lnb_ref[...]
    t = SGU_CHUNK
    ii = lax.broadcasted_iota(I32, (t, t), 0) // CHUNK
    jj = lax.broadcasted_iota(I32, (t, t), 1) // CHUNK
    mask = jj <= ii
    for g in range(SGU_GROUPS):
        ws = jnp.where(mask, ws_ref[g], 0.0)
        sl = slice(g * SGU_GW, (g + 1) * SGU_GW)
        bias = bs_ref[:, g:g + 1]
        for c in range(u_ref.shape[0] // t):
            rows = slice(c * t, (c + 1) * t)
            sv = _dot(ws, vn[rows, sl]) + bias
            u = gelu(u_ref[rows, sl].astype(F32))
            o_ref[rows, sl] = (u * sv * _silu(gate_ref[rows, sl].astype(F32))).astype(o_ref.dtype)


def _sgu(p, lng, lnb, ws, bs_t):
    m = p.shape[0]
    w = BRANCH_W
    tb = SGU_TB
    d0 = P_D0 // w
    return pl.pallas_call(
        _sgu_kernel,
        name="sgu",
        grid=(m // tb,),
        in_specs=[pl.BlockSpec((tb, w), lambda i: (i, d0)),
                  pl.BlockSpec((tb, w), lambda i: (i, d0 + 1)),
                  pl.BlockSpec((tb, w), lambda i: (i, d0 + 2)),
                  pl.BlockSpec((1, w), lambda i: (0, 0)),
                  pl.BlockSpec((1, w), lambda i: (0, 0)),
                  pl.BlockSpec((SGU_GROUPS, SGU_CHUNK, SGU_CHUNK), lambda i: (0, 0, 0)),
                  pl.BlockSpec((SGU_CHUNK, SGU_GROUPS), lambda i: (0, 0))],
        out_specs=pl.BlockSpec((tb, w), lambda i: (i, 0)),
        out_shape=jax.ShapeDtypeStruct((m, w), BF16),
        compiler_params=_cparams(("parallel",)),
    )(p, p, p, lng.reshape(1, w), lnb.reshape(1, w), ws, bs_t)


def _merge_kernel(ya_ref, yb_ref, yc_ref, yd_ref, ga_ref, gb_ref, gc_ref, gd_ref, wp_ref, o_ref):
    acc = None
    for g, (y_ref, g_ref) in enumerate(((ya_ref, ga_ref), (yb_ref, gb_ref), (yc_ref, gc_ref), (yd_ref, gd_ref))):
        term = jax.nn.sigmoid(g_ref[...].astype(F32)) * _dot(y_ref[...], wp_ref[g])
        acc = term if acc is None else acc + term
    o_ref[...] = acc.astype(o_ref.dtype)


def _merge(ya, yb, yc, yd, p, wproj):
    m = ya.shape[0]
    w = BRANCH_W
    tm, tn = MERGE_TM, MERGE_TN
    nj = D_MODEL // tn
    m0 = P_M0 // tn
    ysp = pl.BlockSpec((tm, w), lambda j, i: (i, 0))
    gsp = [pl.BlockSpec((tm, tn), lambda j, i, g=g: (i, m0 + g * nj + j)) for g in range(N_BRANCH)]
    return pl.pallas_call(
        _merge_kernel,
        name="merge",
        grid=(nj, m // tm),
        in_specs=[ysp, ysp, ysp, ysp, *gsp,
                  pl.BlockSpec((N_BRANCH, w, tn), lambda j, i: (0, 0, j))],
        out_specs=pl.BlockSpec((tm, tn), lambda j, i: (i, j)),
        out_shape=jax.ShapeDtypeStruct((m, D_MODEL), BF16),
        compiler_params=_cparams(("parallel", "parallel")),
    )(ya, yb, yc, yd, p, p, p, p, wproj)


def _out_kernel(mg_ref, wo_ref, g_ref, x_ref, o_ref):
    y = _dot(mg_ref[...], wo_ref[...])
    y = y * lax.rsqrt(jnp.mean(y * y, axis=-1, keepdims=True) + NORM_EPS) * g_ref[...]
    o_ref[...] = x_ref[...] + y


def _out_proj(merged, wout, g, x2):
    m, d = x2.shape
    tm = OUT_TM
    return pl.pallas_call(
        _out_kernel,
        name="out_proj",
        grid=(m // tm,),
        in_specs=[pl.BlockSpec((tm, d), lambda i: (i, 0)),
                  pl.BlockSpec((d, d), lambda i: (0, 0)),
                  pl.BlockSpec((1, d), lambda i: (0, 0)),
                  pl.BlockSpec((tm, d), lambda i: (i, 0))],
        out_specs=pl.BlockSpec((tm, d), lambda i: (i, 0)),
        out_shape=jax.ShapeDtypeStruct((m, d), F32),
        compiler_params=_cparams(("parallel",)),
    )(merged, wout, g.reshape(1, d), x2)


def _split_cols(w, sizes):
    offs = np.cumsum((0,) + tuple(sizes))
    return [w[:, int(offs[i]):int(offs[i + 1])] for i in range(len(sizes))]


def _pad_cols(w, n):
    return jnp.pad(w, ((0, 0), (0, n - w.shape[1])))


def _pad_rows(w, n):
    return jnp.pad(w, ((0, n - w.shape[0]), (0, 0)))


def _rope_tables(positions, rot_dim, width):
    inv = ROPE_THETA ** (-jnp.arange(0, rot_dim, 2, dtype=F32) / rot_dim)
    ang = positions.astype(F32).reshape(-1, 1) * inv
    cos, sin = jnp.cos(ang), jnp.sin(ang)
    m = cos.shape[0]
    rest = width - rot_dim
    cos_h = jnp.concatenate([cos, cos, jnp.ones((m, rest), F32)], axis=1)
    sin_h = jnp.concatenate([-sin, sin, jnp.zeros((m, rest), F32)], axis=1)
    reps = LANES // width
    return jnp.tile(cos_h, (1, reps)), jnp.tile(sin_h, (1, reps))


def kernel(x, positions, norm_pre, norm_post, w_in, gla_w_g2, gla_b_g, gla_head_g, rwkv_mu, rwkv_w0,
           rwkv_w_w2, rwkv_a0, rwkv_w_a2, rwkv_k_k, rwkv_k_a, rwkv_r_k, rwkv_lnx_g, rwkv_lnx_b,
           sgu_ln_g, sgu_ln_b, sgu_w_s, sgu_b_s, w_proj, w_out):
    bsz, seq, d = x.shape
    depth = w_in.shape[0]
    m = bsz * seq
    x2 = x.reshape(m, d)
    ca, sa = _rope_tables(positions, DSA_ROT, DSA_HD)
    ci, si = _rope_tables(positions, IDX_ROT, IDX_HD)
    lane_grp = np.arange(LANES) // RWKV_HD
    bd = jnp.asarray((lane_grp[:, None] == lane_grp[None, :]).astype(np.float32))

    for l in range(depth):
        wa, wb, wc, wd, wm = _split_cols(w_in[l], GROUP_COLS)
        a_q, a_k, a_v, a_glr, a_gate = _split_cols(wa, A_COLS)
        b_q, b_k, b_v, b_iq, b_ik, b_iw, b_gate = _split_cols(wb, B_COLS)
        c_r, c_k, c_v, c_wlr, c_alr, c_gate = _split_cols(wc, C_COLS)
        w_main = jnp.concatenate([t.astype(BF16) for t in (wm, b_q, b_k, b_v, b_gate, c_r, c_k, c_v, c_gate, wd,
                                                            a_q, a_k, a_v, a_gate)], axis=1)
        w_small = jnp.concatenate([b_iq] + [_pad_cols(t, LANES) for t in (b_ik, b_iw, c_wlr, c_alr, a_glr)],
                                  axis=1).astype(BF16)

        p, h = _norm_matmul(x2, norm_pre[l], w_main, MM_TN)
        ps = _matmul(h, w_small, S_COLS, F32)

        ya = _gla(p, ps, _pad_rows(gla_w_g2[l], LANES), gla_b_g[l], gla_head_g[l], bsz, seq)

        qr, kr, vt, iqr, ikr, iws = _dsa_prep(p, ps, ca, sa, ci, si)
        yb = _dsa(qr, kr, vt, iqr, ikr, iws, p, bsz, seq)

        mu_r, mu_k, mu_v, mu_wlr, mu_alr, mu_gate = _split_cols(rwkv_mu[l][None, :], C_COLS)
        mu_main = jnp.concatenate([mu_r, mu_k, mu_v, mu_gate], axis=1)
        mu_s = jnp.concatenate([_pad_cols(mu_wlr, LANES), _pad_cols(mu_alr, LANES)], axis=1)
        r, wl, k2, v, kk, a, sg, bon = _rwkv_prep(
            p, ps, mu_main, mu_s, rwkv_w0[l], _pad_rows(rwkv_w_w2[l], LANES), rwkv_a0[l],
            _pad_rows(rwkv_w_a2[l], LANES), rwkv_k_k[l], rwkv_k_a[l], rwkv_r_k[l].reshape(-1), bd, seq)
        y = _rwkv_scan(r, wl, k2, v, kk, a, bsz, seq)
        yc = _rwkv_post(y, bon, sg, rwkv_lnx_g[l], rwkv_lnx_b[l], bd)

        yd = _sgu(p, sgu_ln_g[l], sgu_ln_b[l], sgu_w_s[l], sgu_b_s[l].T)

        merged = _merge(ya, yb, yc, yd, p, w_proj[l].astype(BF16))
        x2 = _out_proj(merged, w_out[l].astype(BF16), norm_post[l], x2)
    return x2.reshape(bsz, seq, d)
```

```python
import functools

import numpy as np
import jax
import jax.numpy as jnp
from jax import lax
from jax.experimental import pallas as pl
from jax.experimental.pallas import tpu as pltpu

F32 = jnp.float32
BF16 = jnp.bfloat16
I32 = jnp.int32
I16 = jnp.int16

D_MODEL = 2048
CHUNK = 64
NORM_EPS = 1e-6
ROPE_THETA = 500000.0
N_BRANCH = 4
BRANCH_W = D_MODEL // 2

GLA_HEADS = 4
GLA_DK = BRANCH_W // 2 // GLA_HEADS
GLA_DV = BRANCH_W // GLA_HEADS
GLA_RANK = 16
GLA_TAU = 16.0

DSA_HEADS = 8
DSA_HD = BRANCH_W // DSA_HEADS
DSA_ROT = DSA_HD // 4
IDX_HEADS = 8
IDX_HD = 64
IDX_ROT = IDX_HD // 4
TOPK_MAX = 256

RWKV_HD = 64
RWKV_HEADS = BRANCH_W // RWKV_HD
RWKV_DECAY_RANK = 96
RWKV_A_RANK = 96
RWKV_DECAY_SCALE = 0.606531
RWKV_GN_EPS = 64e-5
RWKV_KK_EPS = 1e-12

SGU_CHUNK = 128
SGU_GROUPS = 8
SGU_GW = BRANCH_W // SGU_GROUPS
SGU_LN_EPS = 1e-5

A_COLS = (GLA_HEADS * GLA_DK, GLA_HEADS * GLA_DK, BRANCH_W, GLA_RANK, BRANCH_W)
B_COLS = (BRANCH_W, BRANCH_W, BRANCH_W, IDX_HEADS * IDX_HD, IDX_HD, IDX_HEADS, BRANCH_W)
C_COLS = (BRANCH_W, BRANCH_W, BRANCH_W, RWKV_DECAY_RANK, RWKV_A_RANK, BRANCH_W)
D_COLS = (2 * BRANCH_W, BRANCH_W)
GROUP_COLS = (sum(A_COLS), sum(B_COLS), sum(C_COLS), sum(D_COLS), N_BRANCH * D_MODEL)

LANES = 128
SUBLANES = 8
VMEM_LIMIT = 56 * 1024 * 1024
MXU_DEPTH = 256
IDX_K = MXU_DEPTH
NEG_BIG = -1e30
LOG2E = 1.4426950408889634
INT_MIN = -(2 ** 31)

P_M0 = 0
P_B0 = P_M0 + N_BRANCH * D_MODEL
P_C0 = P_B0 + 4 * BRANCH_W
P_D0 = P_C0 + 4 * BRANCH_W
P_A0 = P_D0 + 3 * BRANCH_W
P_COLS = P_A0 + 2 * GLA_HEADS * GLA_DK + 2 * BRANCH_W
S_IDX0 = 0
S_IDX_COLS = IDX_HEADS * IDX_HD + 2 * 128
S_RW0 = S_IDX0 + S_IDX_COLS
S_GLR0 = S_RW0 + 2 * 128
S_COLS = S_GLR0 + 128

MM_TM = 1024
MM_TN = 1024
GLA_TB = 2048
DSA_TQ = 256
DSA_KB = 512
DSA_VROWS = DSA_HD + 16
RWKV_PREP_TB = 256
RWKV_L = 64
RWKV_UNROLL = 16
RWKV_POST_TB = 512
SGU_TB = 512
MERGE_TM = 512
MERGE_TN = 1024
OUT_TM = 512


def _cparams(sem):
    return pltpu.CompilerParams(dimension_semantics=sem, vmem_limit_bytes=VMEM_LIMIT)


def _dot(a, b):
    return jnp.dot(a, b, preferred_element_type=F32)


def _dot_nt(a, b):
    return lax.dot_general(a, b, (((1,), (1,)), ((), ())), preferred_element_type=F32)


def _dot_tn(a, b):
    return lax.dot_general(a, b, (((0,), (0,)), ((), ())), preferred_element_type=F32)


def _bf16_terms(x, n):
    terms = []
    for _ in range(n):
        t = x.astype(BF16)
        terms.append(t)
        x = x - t.astype(F32)
    return terms


def _dot_exact_lhs(a, b, nterms):
    a16 = a.astype(BF16)
    out = None
    for t in _bf16_terms(b, nterms):
        d = _dot(a16, t)
        out = d if out is None else out + d
    return out


def _dot_exact_rhs(a, b, nterms):
    b16 = b.astype(BF16)
    out = None
    for t in _bf16_terms(a, nterms):
        d = _dot(t, b16)
        out = d if out is None else out + d
    return out


def _dot_3pass(a, b, b_lo=None):
    a_hi, a_lo = _bf16_terms(a, 2)
    b_hi, b_lo = _bf16_terms(b, 2) if b_lo is None else (b, b_lo)
    return _dot(a_hi, b_hi) + _dot(a_lo, b_hi) + _dot(a_hi, b_lo)


def _silu(x):
    return x * jax.nn.sigmoid(x)


def _norm_matmul_kernel(x_ref, g_ref, b_ref, o_ref, h_ref):
    @pl.when(pl.program_id(1) == 0)
    def _():
        x = x_ref[...]
        ms = jnp.mean(x * x, axis=-1, keepdims=True)
        h_ref[...] = (x * lax.rsqrt(ms + NORM_EPS) * g_ref[...]).astype(h_ref.dtype)

    o_ref[...] = _dot(h_ref[...], b_ref[...]).astype(o_ref.dtype)


def _norm_matmul(x2, g, b, tn):
    m, k = x2.shape
    n = b.shape[1]
    tm = min(MM_TM, m)
    return pl.pallas_call(
        _norm_matmul_kernel,
        name="proj",
        grid=(m // tm, n // tn),
        in_specs=[pl.BlockSpec((tm, k), lambda i, j: (i, 0)),
                  pl.BlockSpec((1, k), lambda i, j: (0, 0)),
                  pl.BlockSpec((k, tn), lambda i, j: (0, j))],
        out_specs=[pl.BlockSpec((tm, tn), lambda i, j: (i, j)),
                   pl.BlockSpec((tm, k), lambda i, j: (i, 0))],
        out_shape=[jax.ShapeDtypeStruct((m, n), BF16), jax.ShapeDtypeStruct((m, k), BF16)],
        compiler_params=_cparams(("parallel", "arbitrary")),
    )(x2, g.reshape(1, k), b)


def _matmul_kernel(a_ref, b_ref, o_ref):
    o_ref[...] = _dot(a_ref[...], b_ref[...]).astype(o_ref.dtype)


def _matmul(a, b, tn, out_dtype=F32):
    m, k = a.shape
    n = b.shape[1]
    tm = min(MM_TM, m)
    return pl.pallas_call(
        _matmul_kernel,
        name="proj",
        grid=(m // tm, n // tn),
        in_specs=[pl.BlockSpec((tm, k), lambda i, j: (i, 0)),
                  pl.BlockSpec((k, tn), lambda i, j: (0, j))],
        out_specs=pl.BlockSpec((tm, tn), lambda i, j: (i, j)),
        out_shape=jax.ShapeDtypeStruct((m, n), out_dtype),
        compiler_params=_cparams(("parallel", "parallel")),
    )(a, b)


def _gla_kernel(q_ref, k_ref, v_ref, gate_ref, glr_ref, wg2_ref, bg_ref, hg_ref, o_ref, state_ref):
    @pl.when(pl.program_id(2) == 0)
    def _():
        state_ref[...] = jnp.zeros_like(state_ref)

    ii = lax.broadcasted_iota(I32, (CHUNK, CHUNK), 0)
    jj = lax.broadcasted_iota(I32, (CHUNK, CHUNK), 1)
    tril = (ii >= jj).astype(F32)
    chunks = [pl.ds(c * CHUNK, CHUNK) for c in range(q_ref.shape[0] // CHUNK)]
    z = _dot_3pass(glr_ref[...], wg2_ref[...]) + bg_ref[...]
    log_a = (jnp.minimum(z, 0.0) - jnp.log1p(jnp.exp(-jnp.abs(z)))) / GLA_TAU
    cum = [_dot_exact_lhs(tril, log_a[c * CHUNK:(c + 1) * CHUNK], 3) for c in range(len(chunks))]
    total = [x[CHUNK - 1:CHUNK, :] for x in cum]
    k_dec = [k_ref[sl, :].astype(F32) * jnp.exp(t - x) for sl, t, x in zip(chunks, total, cum)]
    kv = [_dot_tn(v_ref[sl, :], kd.astype(BF16)) for sl, kd in zip(chunks, k_dec)]
    states = []
    st = state_ref[...]
    for t, x in zip(total, kv):
        st = st * jnp.exp(t) + x
        states.append(st)
    state_ref[...] = st
    outs = [_dot_nt((q_ref[sl, :].astype(F32) * (GLA_DK ** -0.5)).astype(BF16), s.astype(BF16))
            for sl, s in zip(chunks, states)]
    for sl, o in zip(chunks, outs):
        o = o * lax.rsqrt(jnp.mean(o * o, axis=-1, keepdims=True) + NORM_EPS) * hg_ref[...]
        o_ref[sl, :] = (o * _silu(gate_ref[sl, :].astype(F32))).astype(o_ref.dtype)


def _gla(p, ps, wg2p, bg, hg, bsz, seq):
    m = p.shape[0]
    tb = min(GLA_TB, seq)
    nt = seq // tb
    row = lambda b, h, t: b * nt + t
    qb = P_A0 // GLA_DK
    kb = qb + GLA_HEADS
    vb = (P_A0 + 2 * GLA_HEADS * GLA_DK) // GLA_DV
    gb = vb + GLA_HEADS
    lb = S_GLR0 // LANES
    return pl.pallas_call(
        _gla_kernel,
        name="gla",
        grid=(bsz, GLA_HEADS, nt),
        in_specs=[
            pl.BlockSpec((tb, GLA_DK), lambda b, h, t: (row(b, h, t), qb + h)),
            pl.BlockSpec((tb, GLA_DK), lambda b, h, t: (row(b, h, t), kb + h)),
            pl.BlockSpec((tb, GLA_DV), lambda b, h, t: (row(b, h, t), vb + h)),
            pl.BlockSpec((tb, GLA_DV), lambda b, h, t: (row(b, h, t), gb + h)),
            pl.BlockSpec((tb, LANES), lambda b, h, t: (row(b, h, t), lb)),
            pl.BlockSpec((LANES, GLA_DK), lambda b, h, t: (0, h)),
            pl.BlockSpec((1, GLA_DK), lambda b, h, t: (0, h)),
            pl.BlockSpec((1, GLA_DV), lambda b, h, t: (0, 0)),
        ],
        out_specs=pl.BlockSpec((tb, GLA_DV), lambda b, h, t: (row(b, h, t), h)),
        out_shape=jax.ShapeDtypeStruct((m, BRANCH_W), BF16),
        scratch_shapes=[pltpu.VMEM((GLA_DV, GLA_DK), F32)],
        compiler_params=_cparams(("parallel", "parallel", "arbitrary")),
    )(p, p, p, p, ps, wg2p, bg.reshape(1, -1), hg.reshape(1, -1))


def _rope_tile(x, cos_t, sin_t, half, width):
    lane = lax.broadcasted_iota(I32, x.shape, 1) % width
    partner = jnp.where(lane < half, pltpu.roll(x, LANES - half, 1), pltpu.roll(x, half, 1))
    return x * cos_t + partner * sin_t


def _dsa_prep_kernel(q_ref, k_ref, v_ref, idx_ref, ca_ref, sa_ref, ci_ref, si_ref,
                     qo_ref, ko_ref, vo_ref, iqo_ref, iko_ref, iwo_ref):
    ca, sa = ca_ref[...], sa_ref[...]
    ci, si = ci_ref[...], si_ref[...]
    for h in range(DSA_HEADS):
        sl = slice(h * DSA_HD, (h + 1) * DSA_HD)
        q = _rope_tile(q_ref[:, sl].astype(F32), ca, sa, DSA_ROT // 2, DSA_HD) * (DSA_HD ** -0.5 * LOG2E)
        qo_ref[:, sl] = q.astype(qo_ref.dtype)
        ko_ref[:, sl] = _rope_tile(k_ref[:, sl].astype(F32), ca, sa, DSA_ROT // 2, DSA_HD).astype(ko_ref.dtype)
        vo_ref[0, h * DSA_VROWS:h * DSA_VROWS + DSA_HD, :] = v_ref[:, sl].astype(F32).T.astype(vo_ref.dtype)
        extra = lax.broadcasted_iota(I32, (DSA_VROWS - DSA_HD, q_ref.shape[0]), 0) == 0
        vo_ref[0, h * DSA_VROWS + DSA_HD:(h + 1) * DSA_VROWS, :] = jnp.where(extra, 1.0, 0.0).astype(vo_ref.dtype)
    niq = IDX_HEADS * IDX_HD
    low = lax.broadcasted_iota(I32, (q_ref.shape[0], LANES), 1) < IDX_HD

    def hi_lo(x):
        hi = x.astype(BF16).astype(F32)
        return hi, x - hi

    for j in range(niq // LANES):
        x = _rope_tile(idx_ref[:, j * LANES:(j + 1) * LANES], ci, si, IDX_ROT // 2, IDX_HD)
        hi, lo = hi_lo(x)
        hi_sw = pltpu.roll(hi, IDX_HD, 1)
        lo_sw = pltpu.roll(lo, IDX_HD, 1)
        base = 2 * j * IDX_K
        iqo_ref[:, base:base + LANES] = jnp.where(low, hi, hi_sw).astype(BF16)
        iqo_ref[:, base + LANES:base + IDX_K] = jnp.where(low, lo, 0.0).astype(BF16)
        iqo_ref[:, base + IDX_K:base + IDX_K + LANES] = jnp.where(low, hi_sw, hi).astype(BF16)
        iqo_ref[:, base + IDX_K + LANES:base + 2 * IDX_K] = jnp.where(low, lo_sw, 0.0).astype(BF16)
    hi, lo = hi_lo(_rope_tile(idx_ref[:, niq:niq + LANES], ci, si, IDX_ROT // 2, IDX_HD))
    iko_ref[:, :LANES] = jnp.where(low, hi, pltpu.roll(lo, IDX_HD, 1)).astype(BF16)
    iko_ref[:, LANES:] = jnp.where(low, hi, 0.0).astype(BF16)
    iwo_ref[...] = idx_ref[:, niq + LANES:niq + 2 * LANES] * (niq ** -0.5)


def _dsa_prep(p, ps, ca, sa, ci, si):
    m = p.shape[0]
    tb = DSA_KB
    w = BRANCH_W
    rows = lambda i: (i, 0)
    b0 = P_B0 // w
    return pl.pallas_call(
        _dsa_prep_kernel,
        name="dsa_prep",
        grid=(m // tb,),
        in_specs=[pl.BlockSpec((tb, w), lambda i: (i, b0)),
                  pl.BlockSpec((tb, w), lambda i: (i, b0 + 1)),
                  pl.BlockSpec((tb, w), lambda i: (i, b0 + 2)),
                  pl.BlockSpec((tb, S_IDX_COLS), lambda i: (i, S_IDX0 // S_IDX_COLS)),
                  pl.BlockSpec((tb, LANES), rows), pl.BlockSpec((tb, LANES), rows),
                  pl.BlockSpec((tb, LANES), rows), pl.BlockSpec((tb, LANES), rows)],
        out_specs=[pl.BlockSpec((tb, w), rows), pl.BlockSpec((tb, w), rows),
                   pl.BlockSpec((1, DSA_HEADS * DSA_VROWS, tb), lambda i: (i, 0, 0)),
                   pl.BlockSpec((tb, IDX_HEADS * IDX_K), rows), pl.BlockSpec((tb, IDX_K), rows),
                   pl.BlockSpec((tb, LANES), rows)],
        out_shape=[jax.ShapeDtypeStruct((m, w), BF16), jax.ShapeDtypeStruct((m, w), BF16),
                   jax.ShapeDtypeStruct((m // tb, DSA_HEADS * DSA_VROWS, tb), BF16),
                   jax.ShapeDtypeStruct((m, IDX_HEADS * IDX_K), BF16),
                   jax.ShapeDtypeStruct((m, IDX_K), BF16), jax.ShapeDtypeStruct((m, LANES), F32)],
        compiler_params=_cparams(("parallel",)),
    )(p, p, p, ps, ca, sa, ci, si)


def _dsa_kernel(q_ref, k_ref, vt_ref, iq_ref, ik_ref, iw_ref, gate_ref, o_ref, key_scr, hi_scr, tie_scr,
                *, topk):
    tq, kb_w = DSA_TQ, DSA_KB
    q0 = pl.program_id(1) * tq
    nkb = (q0 + tq + kb_w - 1) // kb_w
    qcol = lax.broadcasted_iota(I32, (1, tq), 1)
    qlim = ((q0 + qcol) // CHUNK + 1) * CHUNK
    sub_k = lax.broadcasted_iota(I32, (kb_w, 1), 0)

    def score_body(kb, carry):
        ik = ik_ref[pl.ds(pl.multiple_of(kb * kb_w, kb_w), kb_w), :]
        dots = [_dot_nt(ik, iq_ref[:, h * IDX_K:(h + 1) * IDX_K]) for h in range(IDX_HEADS)]
        acc = jnp.zeros((kb_w, tq), F32)
        for h, d in enumerate(dots):
            acc = acc + iw_ref[h:h + 1, :] * jnp.maximum(d, 0.0)
        acc = acc + 0.0
        bits = lax.bitcast_convert_type(acc, I32)
        key = bits ^ ((bits >> 31) & 0x7FFFFFFF)
        sidx = kb * kb_w + sub_k
        key = jnp.where(sidx < qlim, key, INT_MIN)
        key_scr[kb] = key
        hi_scr[kb] = (key >> 16).astype(I16)
        return carry

    lax.fori_loop(0, nkb, score_body, 0)

    def count_tiles(load, pred, rows_per_vreg, dtype):
        def body(kb, c):
            m = pred(load(kb), kb * kb_w + sub_k).astype(dtype)
            if dtype == I32:
                nparts = 8
                rows_part = kb_w // nparts
                parts = [jnp.sum(m[j * rows_part:(j + 1) * rows_part]
                                 .reshape(rows_part // rows_per_vreg, rows_per_vreg, tq), axis=0)
                         for j in range(nparts)]
            else:
                parts = [m[j * rows_per_vreg:(j + 1) * rows_per_vreg] for j in range(kb_w // rows_per_vreg)]
            while len(parts) > 1:
                parts = [a + b for a, b in zip(parts[::2], parts[1::2])]
            return c + parts[0]
        c = lax.fori_loop(0, nkb, body, jnp.zeros((rows_per_vreg, tq), dtype))
        return jnp.sum(c.astype(I32), axis=0, keepdims=True)

    def count(pred):
        return count_tiles(lambda kb: key_scr[kb], pred, SUBLANES, I32)

    def count_hi(pred):
        return count_tiles(lambda kb: hi_scr[kb], pred, 2 * SUBLANES, I16)

    def hi_body(it, carry):
        res, cnt_res = carry
        trial = res | lax.shift_left(jnp.int32(1), 31 - it)
        cand_hi = ((trial ^ INT_MIN) >> 16).astype(I16)
        cnt = count_hi(lambda kt, s: kt >= cand_hi)
        ok = cnt >= topk
        return jnp.where(ok, trial, res), jnp.where(ok, cnt, cnt_res)

    res_hi, cnt_ge_hi = lax.fori_loop(0, 16, hi_body,
                                      (jnp.zeros((1, tq), I32), jnp.full((1, tq), nkb * kb_w, I32)))
    thr_hi = ((res_hi ^ INT_MIN) >> 16).astype(I16)
    cnt_gt_hi = count_hi(lambda kt, s: kt > thr_hi)
    lo_min = -(2 ** 15)

    def lo_prep(kb, carry):
        lo = ((key_scr[kb] & 0xFFFF) + lo_min).astype(I16)
        hi_scr[kb] = jnp.where(hi_scr[kb] == thr_hi, lo, I16(lo_min))
        return carry

    lax.fori_loop(0, nkb, lo_prep, 0)
    places = topk - cnt_gt_hi

    def lo_body(it, carry):
        res, cnt_res = carry
        trial = res | lax.shift_left(jnp.int32(1), 15 - it)
        cand_lo = (trial + lo_min).astype(I16)
        cnt = count_hi(lambda kt, s: kt >= cand_lo)
        ok = cnt >= places
        return jnp.where(ok, trial, res), jnp.where(ok, cnt, cnt_res)

    res_lo, cnt_ge_lo = lax.fori_loop(0, 16, lo_body, (jnp.zeros((1, tq), I32), cnt_ge_hi - cnt_gt_hi))
    thr = (res_hi ^ INT_MIN) | res_lo
    cnt_ge = cnt_gt_hi + cnt_ge_lo
    seq_len = key_scr.shape[0] * kb_w
    tie_scr[...] = jnp.full(tie_scr.shape, seq_len, I32)
    has_tie = (cnt_ge > topk) & (thr != INT_MIN)

    @pl.when(jnp.max(jnp.where(has_tie, 1.0, 0.0)) > 0.5)
    def _():
        need = topk - count(lambda kt, s: kt > thr)
        nbits = max(1, int(np.ceil(np.log2(seq_len))))

        def tie_body(it, resj):
            trial = resj | lax.shift_left(jnp.int32(1), nbits - 1 - it)
            c = count(lambda kt, s: (kt == thr) & (s < trial))
            return jnp.where(c < need, trial, resj)

        tie_scr[...] = lax.fori_loop(0, nbits, tie_body, jnp.zeros((1, tq), I32))

    last_tie = tie_scr[...]

    def softmax_mask(kb):
        kt = key_scr[kb]
        sidx = kb * kb_w + sub_k
        sel = ((kt > thr) | ((kt == thr) & (sidx <= last_tie))) & (kt != INT_MIN)
        return jnp.where(sel, 0.0, NEG_BIG).astype(BF16)

    heads = [slice(h * DSA_HD, (h + 1) * DSA_HD) for h in range(DSA_HEADS)]

    def att_body(kb, stats):
        rows = pl.ds(pl.multiple_of(kb * kb_w, kb_w), kb_w)
        bias = softmax_mask(kb)
        s = [_dot_nt(k_ref[rows, sl], q_ref[:, sl]).astype(BF16) + bias for sl in heads]
        m_n = [jnp.maximum(c[0], jnp.max(x, axis=0, keepdims=True).astype(F32)) for c, x in zip(stats, s)]
        p = [jnp.exp2(x - m.astype(BF16)) for x, m in zip(s, m_n)]
        pv = [_dot(vt_ref[kb, h * DSA_VROWS:(h + 1) * DSA_VROWS, :], x) for h, x in enumerate(p)]
        alpha = [jnp.exp2(c[0] - m) for c, m in zip(stats, m_n)]
        return tuple((m, a * c[1] + y) for m, a, c, y in zip(m_n, alpha, stats, pv))

    init = tuple((jnp.full((1, tq), NEG_BIG, F32), jnp.zeros((DSA_VROWS, tq), F32)) for _ in heads)
    final = lax.fori_loop(0, nkb, att_body, init)
    for sl, (_, acc_f) in zip(heads, final):
        o = (acc_f[:DSA_HD] / acc_f[DSA_HD:DSA_HD + 1]).T
        o_ref[:, sl] = (o * _silu(gate_ref[:, sl].astype(F32))).astype(o_ref.dtype)


def _dsa(qr, kr, vt, iqr, ikr, iws, p, bsz, seq):
    m = qr.shape[0]
    w = BRANCH_W
    nq = seq // DSA_TQ
    nkb = seq // DSA_KB
    topk = min(TOPK_MAX, seq // 4)
    qrow = lambda b, i: (b * nq + i, 0)
    return pl.pallas_call(
        functools.partial(_dsa_kernel, topk=topk),
        name="dsa",
        grid=(bsz, nq),
        in_specs=[pl.BlockSpec((DSA_TQ, w), qrow),
                  pl.BlockSpec((seq, w), lambda b, i: (b, 0), pipeline_mode=pl.Buffered(1)),
                  pl.BlockSpec((nkb, DSA_HEADS * DSA_VROWS, DSA_KB), lambda b, i: (b, 0, 0),
                               pipeline_mode=pl.Buffered(1)),
                  pl.BlockSpec((DSA_TQ, IDX_HEADS * IDX_K), qrow),
                  pl.BlockSpec((seq, IDX_K), lambda b, i: (b, 0), pipeline_mode=pl.Buffered(1)),
                  pl.BlockSpec((IDX_HEADS, DSA_TQ), lambda b, i: (0, b * nq + i)),
                  pl.BlockSpec((DSA_TQ, w), lambda b, i: (b * nq + i, P_B0 // w + 3))],
        out_specs=pl.BlockSpec((DSA_TQ, w), qrow),
        out_shape=jax.ShapeDtypeStruct((m, w), BF16),
        scratch_shapes=[pltpu.VMEM((nkb, DSA_KB, DSA_TQ), I32),
                        pltpu.VMEM((nkb, DSA_KB, DSA_TQ), I16),
                        pltpu.VMEM((1, DSA_TQ), I32)],
        compiler_params=_cparams(("parallel", "arbitrary")),
    )(qr, kr, vt, iqr, ikr, iws[:, :IDX_HEADS].T, p)


def _head_sum(x, bd):
    parts = [_dot_exact_rhs(x[:, j * LANES:(j + 1) * LANES], bd, 2) for j in range(x.shape[1] // LANES)]
    return jnp.concatenate(parts, axis=1)


def _rwkv_prep_kernel(c_ref, cs_ref, pc_ref, pcs_ref, mu_ref, mus_ref, w0_ref, ww2_ref, ww2lo_ref,
                      a0_ref, wa2_ref, wa2lo_ref, kkw_ref, ka_ref, rk_ref, bd_ref,
                      r_o, wl_o, k_o, v_o, kk_o, a_o, sg_o, bon_o, *, blocks_per_seq):
    tb = c_ref.shape[0]
    first = (pl.program_id(0) % blocks_per_seq) == 0
    rowi = lax.broadcasted_iota(I32, (tb, 1), 0)

    def shift_lerp(c, prev_rows, mu):
        last = prev_rows.shape[0] - 1
        prev = jnp.where(first, 0.0, prev_rows[last:last + 1, :])
        sh = jnp.where(rowi == 0, prev, pltpu.roll(c, 1, 0))
        return c + (sh - c) * mu

    w = BRANCH_W
    c = shift_lerp(c_ref[...].astype(F32), pc_ref[...].astype(F32), mu_ref[...])
    cs = shift_lerp(cs_ref[...], pcs_ref[...], mus_ref[...])
    r, k, v, gate = c[:, :w], c[:, w:2 * w], c[:, 2 * w:3 * w], c[:, 3 * w:]
    w_log = -RWKV_DECAY_SCALE * jax.nn.sigmoid(
        w0_ref[...] + _dot_3pass(jnp.tanh(cs[:, :LANES]), ww2_ref[...], ww2lo_ref[...]))
    a = jax.nn.sigmoid(a0_ref[...] + _dot_3pass(cs[:, LANES:], wa2_ref[...], wa2lo_ref[...]))
    kk = k * kkw_ref[...]
    k2 = k * (1.0 + (a - 1.0) * ka_ref[...])
    bd = bd_ref[...]
    kk = kk * lax.rsqrt(_head_sum(kk * kk, bd) + RWKV_KK_EPS)
    r_o[...] = r.astype(r_o.dtype)
    wl_o[...] = w_log
    k_o[...] = k2.astype(k_o.dtype)
    v_o[...] = v.astype(v_o.dtype)
    kk_o[...] = kk.astype(kk_o.dtype)
    a_o[...] = a.astype(a_o.dtype)
    sg_o[...] = _silu(gate).astype(sg_o.dtype)
    bon_o[...] = (_head_sum(r * k2 * rk_ref[...], bd) * v).astype(bon_o.dtype)


def _rwkv_prep(p, ps, mu_main, mu_s, w0, ww2p, a0, wa2p, kkw, ka, rk, bd, seq):
    m = p.shape[0]
    tb = RWKV_PREP_TB
    w = BRANCH_W
    pr16, pr8 = 2 * SUBLANES, SUBLANES
    cblk, sblk = P_C0 // (4 * w), S_RW0 // (2 * LANES)
    const = lambda i: (0, 0)
    vec = pl.BlockSpec((1, w), const)
    lowrank = pl.BlockSpec((LANES, w), const)
    ww2_hi, ww2_lo = _bf16_terms(ww2p, 2)
    wa2_hi, wa2_lo = _bf16_terms(wa2p, 2)
    out = pl.BlockSpec((tb, w), lambda i: (i, 0))
    dts = [BF16, F32, BF16, BF16, BF16, BF16, BF16, BF16]
    return pl.pallas_call(
        functools.partial(_rwkv_prep_kernel, blocks_per_seq=seq // tb),
        name="rwkv_prep",
        grid=(m // tb,),
        in_specs=[pl.BlockSpec((tb, 4 * w), lambda i: (i, cblk)),
                  pl.BlockSpec((tb, 2 * LANES), lambda i: (i, sblk)),
                  pl.BlockSpec((pr16, 4 * w), lambda i: (jnp.maximum(i * (tb // pr16) - 1, 0), cblk)),
                  pl.BlockSpec((pr8, 2 * LANES), lambda i: (jnp.maximum(i * (tb // pr8) - 1, 0), sblk)),
                  pl.BlockSpec((1, 4 * w), const), pl.BlockSpec((1, 2 * LANES), const),
                  vec, lowrank, lowrank, vec, lowrank, lowrank,
                  vec, vec, vec, pl.BlockSpec((LANES, LANES), const)],
        out_specs=[out] * 8,
        out_shape=[jax.ShapeDtypeStruct((m, w), dt) for dt in dts],
        compiler_params=_cparams(("parallel",)),
    )(p, ps, p, ps, mu_main, mu_s, w0.reshape(1, w), ww2_hi, ww2_lo, a0.reshape(1, w), wa2_hi, wa2_lo,
      kkw.reshape(1, w), ka.reshape(1, w), rk.reshape(1, w), bd)


def _rwkv_scan_kernel(r_ref, wl_ref, k_ref, v_ref, kk_ref, a_ref, y_ref, lin_scr, yq_scr, h_scr, *, unroll, nseq):
    n = RWKV_L
    two = 2 * n
    nc = r_ref.shape[0] // n
    i2 = lax.broadcasted_iota(I32, (two, two), 0)
    j2 = lax.broadcasted_iota(I32, (two, two), 1)
    same_head = (i2 // RWKV_HD) == (j2 // RWKV_HD)
    eye2 = (i2 == j2).astype(F32)
    ti = lax.broadcasted_iota(I32, (n, LANES), 0)
    ci = lax.broadcasted_iota(I32, (n, LANES), 1) % n
    strict = ti > ci
    incl = ti >= ci
    same16 = (ti // 16) == (ci // 16)
    eye = (ti == ci).astype(F32)
    ii = lax.broadcasted_iota(I32, (n, n), 0)
    jj = lax.broadcasted_iota(I32, (n, n), 1)
    tril = (ii >= jj).astype(F32)
    head0 = lax.broadcasted_iota(I32, (n, LANES), 1) < RWKV_HD

    def mm(a, b):
        return _dot(a.astype(BF16), b.astype(BF16))

    def mm_nt(a, b):
        return _dot_nt(a.astype(BF16), b.astype(BF16))

    def mm_tn(a, b):
        return _dot_tn(a.astype(BF16), b.astype(BF16))

    def stack(x):
        return jnp.concatenate([jnp.where(head0, x, 0.0), jnp.where(head0, 0.0, x)], axis=0)

    def fold(x):
        return x[:n] + x[n:]

    def each(f, *lists):
        return [f(*args) for args in zip(*lists)]

    def tri_inv(a_low):
        a_d = each(lambda a: jnp.where(same16, a, 0.0), a_low)
        a_o = each(lambda a, d: a - d, a_low, a_d)
        ad_st = each(stack, a_d)
        a2 = each(mm, a_d, ad_st)
        tick()
        a34 = each(lambda s, d: mm(s, jnp.concatenate([d, stack(s)], axis=1)), a2, ad_st)
        tick()
        a4_st = each(lambda x: stack(x[:, LANES:]), a34)
        s4 = each(lambda d, s, x: eye + d + s + x[:, :LANES], a_d, a2, a34)
        a8 = each(lambda x, y: mm(x[:, LANES:], y), a34, a4_st)
        tick()
        s8 = each(lambda s, x: s + mm(s, x), s4, a4_st)
        tick()
        t_d = each(lambda s, x: s + mm(s, stack(x)), s8, a8)
        tick()
        nn = each(lambda t, o: mm(t, stack(o)), t_d, a_o)
        tick()
        nx = each(lambda x, t: mm(x, jnp.concatenate([stack(x), stack(t)], axis=1)), nn, t_d)
        tick()
        x1 = each(lambda t, x: t + x[:, LANES:], t_d, nx)
        return each(lambda x, y: y + mm(x[:, :LANES], stack(y)), nx, x1)

    pending = []

    def tick():
        for _ in range(-(-unroll // 8)):
            if pending:
                pending.pop(0)()

    def build_group(g):
        rows = [pl.ds(pl.multiple_of((g * unroll + u) * n, n), n) for u in range(unroll)]
        wl = [wl_ref[r, :] for r in rows]
        k = [k_ref[r, :].astype(F32) for r in rows]
        v = [v_ref[r, :].astype(F32) for r in rows]
        kk = [kk_ref[r, :].astype(F32) for r in rows]
        beta = each(lambda x, r: x * a_ref[r, :].astype(F32), kk, rows)
        cum = each(lambda x: _dot_exact_lhs(tril, x, 3), wl)
        cum_end = each(lambda x: x[n - 1:n, :], cum)
        g_inv = each(lambda x: jnp.exp(-x), cum)
        g_end = each(lambda e, x: jnp.exp(e - x), cum_end, cum)
        a_t = each(lambda x, c, w: -x * jnp.exp(c - w), kk, cum, wl)
        r_t = each(lambda r, c: r_ref[r, :].astype(F32) * jnp.exp(c), rows, cum)
        bk_st = each(lambda x, y, z: jnp.concatenate([stack(x * z), stack(y * z)], axis=0), beta, k, g_inv)
        ar = each(lambda x, y: jnp.concatenate([x, y], axis=0), a_t, r_t)
        gram = each(mm_nt, ar, bk_st)
        a_ab = each(lambda x: jnp.where(strict, x[:n, :LANES], 0.0), gram)
        a_rb = each(lambda x: jnp.where(incl, x[n:, :LANES], 0.0), gram)
        a_ak = each(lambda x: jnp.where(strict, x[:n, LANES:], 0.0), gram)
        a_rk = each(lambda x: jnp.where(incl, x[n:, LANES:], 0.0), gram)
        av2 = each(lambda x, y, z: mm(jnp.concatenate([x, y], axis=0), stack(z)), a_ak, a_rk, v)
        t_inv = tri_inv(a_ab)
        tick()
        pq = each(lambda t, x, y: mm(t, jnp.concatenate([stack(x), stack(y[:n])], axis=1)), t_inv, a_t, av2)
        yr = each(lambda x, y: mm(x, jnp.concatenate([stack(y[:, :LANES]), stack(y[:, LANES:])], axis=1)), a_rb, pq)
        kv = each(lambda x, y, z: mm_tn(x * y, z), k, g_end, v)
        gh = each(lambda x, y, z: mm_tn(x * y, z), beta, g_end, pq)
        while pending:
            tick()
        for u in range(unroll):
            ry = r_t[u] + yr[u][:, :LANES]
            g_bd = jnp.where(same_head, gh[u][:, :LANES], 0.0) + eye2 * jnp.exp(cum_end[u])
            lin_scr[u] = jnp.concatenate([ry, g_bd], axis=0)
            yq_scr[u] = yr[u][:, LANES:] + av2[u][n:]
            h_scr[u] = jnp.where(same_head, gh[u][:, LANES:] + kv[u], 0.0)

    def sweep_steps(g, state):
        def step(u):
            rows = pl.ds(pl.multiple_of((g * unroll + u) * n, n), n)
            both = mm(lin_scr[u], state[0])
            y_ref[rows, :] = both[:n] + yq_scr[u]
            state[0] = both[n:] + h_scr[u]
        return [functools.partial(step, u) for u in range(unroll)]

    ngroups = nc // unroll
    groups_per_seq = ngroups // nseq
    build_group(0)

    def body(g, st):
        state = [st]
        pending.extend(sweep_steps(g - 1, state))
        build_group(g)
        return jnp.where(g % groups_per_seq == 0, 0.0, state[0])

    st = lax.fori_loop(1, ngroups, body, jnp.zeros((two, two), F32))
    state = [st]
    for step in sweep_steps(ngroups - 1, state):
        step()


def _rwkv_scan(r, wl, k2, v, kk, a, bsz, seq):
    m = r.shape[0]
    npair = BRANCH_W // LANES
    nseq = 2 if bsz % 2 == 0 else 1
    blk = pl.BlockSpec((nseq * seq, LANES), lambda b, p: (b, p))
    u = RWKV_UNROLL
    return pl.pallas_call(
        functools.partial(_rwkv_scan_kernel, unroll=u, nseq=nseq),
        name="rwkv_scan",
        grid=(bsz // nseq, npair),
        in_specs=[blk] * 6,
        out_specs=blk,
        out_shape=jax.ShapeDtypeStruct((m, BRANCH_W), F32),
        scratch_shapes=[pltpu.VMEM((u, 3 * RWKV_L, LANES), F32), pltpu.VMEM((u, RWKV_L, LANES), F32),
                        pltpu.VMEM((u, LANES, LANES), F32)],
        compiler_params=_cparams(("parallel", "parallel")),
    )(r, wl, k2, v, kk, a)


def _rwkv_post_kernel(y_ref, bon_ref, sg_ref, g_ref, b_ref, bd_ref, o_ref):
    y = y_ref[...]
    bd = bd_ref[...]
    mu = _head_sum(y, bd) * (1.0 / RWKV_HD)
    yc = y - mu
    var = _head_sum(yc * yc, bd) * (1.0 / RWKV_HD)
    yn = yc * lax.rsqrt(var + RWKV_GN_EPS) * g_ref[...] + b_ref[...]
    o_ref[...] = ((yn + bon_ref[...].astype(F32)) * sg_ref[...].astype(F32)).astype(o_ref.dtype)


def _rwkv_post(y, bon, sg, g, b, bd):
    m, w = y.shape
    tb = RWKV_POST_TB
    rows = pl.BlockSpec((tb, w), lambda i: (i, 0))
    vec = pl.BlockSpec((1, w), lambda i: (0, 0))
    return pl.pallas_call(
        _rwkv_post_kernel,
        name="rwkv_post",
        grid=(m // tb,),
        in_specs=[rows, rows, rows, vec, vec, pl.BlockSpec((LANES, LANES), lambda i: (0, 0))],
        out_specs=rows,
        out_shape=jax.ShapeDtypeStruct((m, w), BF16),
        compiler_params=_cparams(("parallel",)),
    )(y, bon, sg, g.reshape(1, w), b.reshape(1, w), bd)


def _sgu_kernel(u_ref, v_ref, gate_ref, lng_ref, lnb_ref, ws_ref, bs_ref, o_ref):
    inv_sqrt2 = 0.7071067811865476

    def gelu(z):
        return 0.5 * z * (1.0 + lax.erf(z * inv_sqrt2))

    v = gelu(v_ref[...].astype(F32))
    mu = jnp.mean(v, axis=-1, keepdims=True)
    vc = v - mu
    var = jnp.mean(vc * vc, axis=-1, keepdims=True)
    vn = vc * lax.rsqrt(var + SGU_LN_EPS) * lng_ref[...] + lnb_ref[...]
    t = SGU_CHUNK
    ii = lax.broadcasted_iota(I32, (t, t), 0) // CHUNK
    jj = lax.broadcasted_iota(I32, (t, t), 1) // CHUNK
    mask = jj <= ii
    for g in range(SGU_GROUPS):
        ws = jnp.where(mask, ws_ref[g], 0.0)
        sl = slice(g * SGU_GW, (g + 1) * SGU_GW)
        bias = bs_ref[:, g:g + 1]
        for c in range(u_ref.shape[0] // t):
            rows = slice(c * t, (c + 1) * t)
            sv = _dot(ws, vn[rows, sl]) + bias
            u = gelu(u_ref[rows, sl].astype(F32))
            o_ref[rows, sl] = (u * sv * _silu(gate_ref[rows, sl].astype(F32))).astype(o_ref.dtype)


def _sgu(p, lng, lnb, ws, bs_t):
    m = p.shape[0]
    w = BRANCH_W
    tb = SGU_TB
    d0 = P_D0 // w
    return pl.pallas_call(
        _sgu_kernel,
        name="sgu",
        grid=(m // tb,),
        in_specs=[pl.BlockSpec((tb, w), lambda i: (i, d0)),
                  pl.BlockSpec((tb, w), lambda i: (i, d0 + 1)),
                  pl.BlockSpec((tb, w), lambda i: (i, d0 + 2)),
                  pl.BlockSpec((1, w), lambda i: (0, 0)),
                  pl.BlockSpec((1, w), lambda i: (0, 0)),
                  pl.BlockSpec((SGU_GROUPS, SGU_CHUNK, SGU_CHUNK), lambda i: (0, 0, 0)),
                  pl.BlockSpec((SGU_CHUNK, SGU_GROUPS), lambda i: (0, 0))],
        out_specs=pl.BlockSpec((tb, w), lambda i: (i, 0)),
        out_shape=jax.ShapeDtypeStruct((m, w), BF16),
        compiler_params=_cparams(("parallel",)),
    )(p, p, p, lng.reshape(1, w), lnb.reshape(1, w), ws, bs_t)


def _merge_kernel(ya_ref, yb_ref, yc_ref, yd_ref, ga_ref, gb_ref, gc_ref, gd_ref, wp_ref, o_ref):
    acc = None
    for g, (y_ref, g_ref) in enumerate(((ya_ref, ga_ref), (yb_ref, gb_ref), (yc_ref, gc_ref), (yd_ref, gd_ref))):
        term = jax.nn.sigmoid(g_ref[...].astype(F32)) * _dot(y_ref[...], wp_ref[g])
        acc = term if acc is None else acc + term
    o_ref[...] = acc.astype(o_ref.dtype)


def _merge(ya, yb, yc, yd, p, wproj):
    m = ya.shape[0]
    w = BRANCH_W
    tm, tn = MERGE_TM, MERGE_TN
    nj = D_MODEL // tn
    m0 = P_M0 // tn
    ysp = pl.BlockSpec((tm, w), lambda j, i: (i, 0))
    gsp = [pl.BlockSpec((tm, tn), lambda j, i, g=g: (i, m0 + g * nj + j)) for g in range(N_BRANCH)]
    return pl.pallas_call(
        _merge_kernel,
        name="merge",
        grid=(nj, m // tm),
        in_specs=[ysp, ysp, ysp, ysp, *gsp,
                  pl.BlockSpec((N_BRANCH, w, tn), lambda j, i: (0, 0, j))],
        out_specs=pl.BlockSpec((tm, tn), lambda j, i: (i, j)),
        out_shape=jax.ShapeDtypeStruct((m, D_MODEL), BF16),
        compiler_params=_cparams(("parallel", "parallel")),
    )(ya, yb, yc, yd, p, p, p, p, wproj)


def _out_kernel(mg_ref, wo_ref, g_ref, x_ref, o_ref):
    y = _dot(mg_ref[...], wo_ref[...])
    y = y * lax.rsqrt(jnp.mean(y * y, axis=-1, keepdims=True) + NORM_EPS) * g_ref[...]
    o_ref[...] = x_ref[...] + y


def _out_proj(merged, wout, g, x2):
    m, d = x2.shape
    tm = OUT_TM
    return pl.pallas_call(
        _out_kernel,
        name="out_proj",
        grid=(m // tm,),
        in_specs=[pl.BlockSpec((tm, d), lambda i: (i, 0)),
                  pl.BlockSpec((d, d), lambda i: (0, 0)),
                  pl.BlockSpec((1, d), lambda i: (0, 0)),
                  pl.BlockSpec((tm, d), lambda i: (i, 0))],
        out_specs=pl.BlockSpec((tm, d), lambda i: (i, 0)),
        out_shape=jax.ShapeDtypeStruct((m, d), F32),
        compiler_params=_cparams(("parallel",)),
    )(merged, wout, g.reshape(1, d), x2)


def _split_cols(w, sizes):
    offs = np.cumsum((0,) + tuple(sizes))
    return [w[:, int(offs[i]):int(offs[i + 1])] for i in range(len(sizes))]


def _pad_cols(w, n):
    return jnp.pad(w, ((0, 0), (0, n - w.shape[1])))


def _pad_rows(w, n):
    return jnp.pad(w, ((0, n - w.shape[0]), (0, 0)))


def _rope_tables(positions, rot_dim, width):
    inv = ROPE_THETA ** (-jnp.arange(0, rot_dim, 2, dtype=F32) / rot_dim)
    ang = positions.astype(F32).reshape(-1, 1) * inv
    cos, sin = jnp.cos(ang), jnp.sin(ang)
    m = cos.shape[0]
    rest = width - rot_dim
    cos_h = jnp.concatenate([cos, cos, jnp.ones((m, rest), F32)], axis=1)
    sin_h = jnp.concatenate([-sin, sin, jnp.zeros((m, rest), F32)], axis=1)
    reps = LANES // width
    return jnp.tile(cos_h, (1, reps)), jnp.tile(sin_h, (1, reps))


def kernel(x, positions, norm_pre, norm_post, w_in, gla_w_g2, gla_b_g, gla_head_g, rwkv_mu, rwkv_w0,
           rwkv_w_w2, rwkv_a0, rwkv_w_a2, rwkv_k_k, rwkv_k_a, rwkv_r_k, rwkv_lnx_g, rwkv_lnx_b,
           sgu_ln_g, sgu_ln_b, sgu_w_s, sgu_b_s, w_proj, w_out):
    bsz, seq, d = x.shape
    depth = w_in.shape[0]
    m = bsz * seq
    x2 = x.reshape(m, d)
    ca, sa = _rope_tables(positions, DSA_ROT, DSA_HD)
    ci, si = _rope_tables(positions, IDX_ROT, IDX_HD)
    lane_grp = np.arange(LANES) // RWKV_HD
    bd = jnp.asarray((lane_grp[:, None] == lane_grp[None, :]).astype(np.float32))

    for l in range(depth):
        wa, wb, wc, wd, wm = _split_cols(w_in[l], GROUP_COLS)
        a_q, a_k, a_v, a_glr, a_gate = _split_cols(wa, A_COLS)
        b_q, b_k, b_v, b_iq, b_ik, b_iw, b_gate = _split_cols(wb, B_COLS)
        c_r, c_k, c_v, c_wlr, c_alr, c_gate = _split_cols(wc, C_COLS)
        w_main = jnp.concatenate([t.astype(BF16) for t in (wm, b_q, b_k, b_v, b_gate, c_r, c_k, c_v, c_gate, wd,
                                                            a_q, a_k, a_v, a_gate)], axis=1)
        w_small = jnp.concatenate([b_iq] + [_pad_cols(t, LANES) for t in (b_ik, b_iw, c_wlr, c_alr, a_glr)],
                                  axis=1).astype(BF16)

        p, h = _norm_matmul(x2, norm_pre[l], w_main, MM_TN)
        ps = _matmul(h, w_small, S_COLS, F32)

        ya = _gla(p, ps, _pad_rows(gla_w_g2[l], LANES), gla_b_g[l], gla_head_g[l], bsz, seq)

        qr, kr, vt, iqr, ikr, iws = _dsa_prep(p, ps, ca, sa, ci, si)
        yb = _dsa(qr, kr, vt, iqr, ikr, iws, p, bsz, seq)

        mu_r, mu_k, mu_v, mu_wlr, mu_alr, mu_gate = _split_cols(rwkv_mu[l][None, :], C_COLS)
        mu_main = jnp.concatenate([mu_r, mu_k, mu_v, mu_gate], axis=1)
        mu_s = jnp.concatenate([_pad_cols(mu_wlr, LANES), _pad_cols(mu_alr, LANES)], axis=1)
        r, wl, k2, v, kk, a, sg, bon = _rwkv_prep(
            p, ps, mu_main, mu_s, rwkv_w0[l], _pad_rows(rwkv_w_w2[l], LANES), rwkv_a0[l],
            _pad_rows(rwkv_w_a2[l], LANES), rwkv_k_k[l], rwkv_k_a[l], rwkv_r_k[l].reshape(-1), bd, seq)
        y = _rwkv_scan(r, wl, k2, v, kk, a, bsz, seq)
        yc = _rwkv_post(y, bon, sg, rwkv_lnx_g[l], rwkv_lnx_b[l], bd)

        yd = _sgu(p, sgu_ln_g[l], sgu_ln_b[l], sgu_w_s[l], sgu_b_s[l].T)

        merged = _merge(ya, yb, yc, yd, p, w_proj[l].astype(BF16))
        x2 = _out_proj(merged, w_out[l].astype(BF16), norm_post[l], x2)
    return x2.reshape(bsz, seq, d)
```
